```python
import jax, jax.numpy as jnp
from jax import lax
import numpy as np

D_MODEL = 2048
BATCH = 8
SEQ = 4096
DEPTH = 4

N_HEADS = 16
HEAD_DIM = 128
D_LIN = N_HEADS * HEAD_DIM
CONV_QKV = 4
CHUNK = 64
POOL_WINDOWS = (2, 4, 8, 16)
N_POOL_GROUPS = len(POOL_WINDOWS)
D_POOL = D_MODEL
POOL_GROUP_DIM = D_POOL // N_POOL_GROUPS
D_FF = 5632
CONV_FFN = 3
EPS = 1e-6
D_IN_PROJ = 4 * D_LIN + 2 * N_HEADS + D_POOL + 2 * D_MODEL

kernel_name = "hybrid_gdn_pool_convglu_trunk"


def rmsnorm(x, w):
    xf = x.astype(jnp.float32)
    y = xf * lax.rsqrt(jnp.mean(xf * xf, axis=-1, keepdims=True) + EPS)
    return (y * w.astype(jnp.float32)).astype(x.dtype)


def l2norm(x):
    xf = x.astype(jnp.float32)
    return xf * lax.rsqrt(jnp.sum(xf * xf, axis=-1, keepdims=True) + EPS)


def causal_dwconv(x, w):
    width = w.shape[0]
    seq = x.shape[1]
    xp = jnp.pad(x, ((0, 0), (width - 1, 0), (0, 0)))
    y = xp[:, 0:seq] * w[0]
    for j in range(1, width):
        y = y + xp[:, j:j + seq] * w[j]
    return y


def gated_delta_rule(q, k, v, g, beta):
    f32 = jnp.float32
    q, k, v, g, beta = (t.astype(f32) for t in (q, k, v, g, beta))
    bsz, seq, nh, dk = q.shape
    dv = v.shape[-1]
    pad = (-seq) % CHUNK
    if pad:
        q, k, v = (jnp.pad(t, ((0, 0), (0, pad), (0, 0), (0, 0))) for t in (q, k, v))
        g, beta = (jnp.pad(t, ((0, 0), (0, pad), (0, 0))) for t in (g, beta))
    n_chunks = (seq + pad) // CHUNK

    def to_chunks(t):
        return t.reshape(bsz, n_chunks, CHUNK, nh, t.shape[-1]).transpose(0, 3, 1, 2, 4)

    q, k, v = to_chunks(q), to_chunks(k), to_chunks(v)
    g = g.reshape(bsz, n_chunks, CHUNK, nh).transpose(0, 3, 1, 2)
    beta = beta.reshape(bsz, n_chunks, CHUNK, nh).transpose(0, 3, 1, 2)
    g = jnp.cumsum(g, axis=-1)

    tril = jnp.tril(jnp.ones((CHUNK, CHUNK), dtype=bool))
    strict = jnp.tril(jnp.ones((CHUNK, CHUNK), dtype=bool), k=-1)
    diff = g[..., :, None] - g[..., None, :]
    decay = jnp.exp(jnp.where(tril, diff, -jnp.inf))

    k_beta = k * beta[..., None]
    v_beta = v * beta[..., None]
    lmat = jnp.where(strict, jnp.einsum('bhnid,bhnjd->bhnij', k_beta, k) * decay, 0.0)
    amat = lmat + jnp.eye(CHUNK, dtype=f32)
    rhs = jnp.concatenate([v_beta, k_beta * jnp.exp(g)[..., None]], axis=-1)
    sol = lax.linalg.triangular_solve(amat, rhs, left_side=True, lower=True, unit_diagonal=True)
    u_val = sol[..., :dv]
    w_cum = sol[..., dv:]

    attn_intra = jnp.einsum('bhnid,bhnjd->bhnij', q, k) * decay
    q_decay = q * jnp.exp(g)[..., None]
    k_to_end = k * jnp.exp(g[..., -1:] - g)[..., None]
    chunk_decay = jnp.exp(g[..., -1])

    xs = tuple(jnp.moveaxis(t, 2, 0) for t in (u_val, w_cum, q_decay, attn_intra, k_to_end, chunk_decay))

    def step(state, inp):
        u_c, w_c, qd_c, at_c, ke_c, dec_c = inp
        v_new = u_c - jnp.einsum('bhcd,bhdv->bhcv', w_c, state)
        o_c = jnp.einsum('bhcd,bhdv->bhcv', qd_c, state) + jnp.einsum('bhij,bhjv->bhiv', at_c, v_new)
        state = state * dec_c[..., None, None] + jnp.einsum('bhcd,bhcv->bhdv', ke_c, v_new)
        return state, o_c

    state0 = jnp.zeros((bsz, nh, dk, dv), dtype=f32)
    _, o = lax.scan(step, state0, xs)
    o = o.transpose(1, 0, 3, 2, 4).reshape(bsz, n_chunks * CHUNK, nh, dv)
    return o[:, :seq]


def causal_multiscale_pool(u):
    seq = u.shape[1]
    uf = u.astype(jnp.float32)
    cs = jnp.pad(jnp.cumsum(uf, axis=1), ((0, 0), (1, 0), (0, 0), (0, 0)))
    t = jnp.arange(seq)
    outs = []
    for gi, win in enumerate(POOL_WINDOWS):
        hi = cs[:, 1:, gi]
        lo = cs[:, jnp.maximum(t + 1 - win, 0), gi]
        cnt = jnp.minimum(t + 1, win).astype(jnp.float32)
        outs.append((hi - lo) / cnt[None, :, None])
    pooled = jnp.stack(outs, axis=2)
    return (pooled - uf).astype(u.dtype)


def _fwd_setup_inputs(seed: int = 0) -> dict:
    key = jax.random.key(seed)
    ks = jax.random.split(key, 20)
    f32 = jnp.float32

    def nrm(k, shape, scale):
        return jax.random.normal(k, shape, f32) * scale

    def gain(k, shape):
        return 1.0 + 0.02 * jax.random.normal(k, shape, f32)

    x = jax.random.normal(ks[0], (BATCH, SEQ, D_MODEL), f32)
    a_init = jax.random.uniform(ks[4], (DEPTH, N_HEADS), f32, 1.0, 16.0)
    dt = jnp.exp(jax.random.uniform(ks[5], (DEPTH, N_HEADS), f32, np.log(1e-3), np.log(1e-1)))
    dt_bias = dt + jnp.log(-jnp.expm1(-dt))
    return {
        "x": x,
        "norm_mix_w": gain(ks[1], (DEPTH, D_MODEL)),
        "w_in": nrm(ks[2], (DEPTH, D_MODEL, D_IN_PROJ), D_MODEL ** -0.5),
        "conv_qkv_w": nrm(ks[3], (DEPTH, CONV_QKV, 3 * D_LIN), CONV_QKV ** -0.5),
        "a_log": jnp.log(a_init),
        "dt_bias": dt_bias,
        "gdn_norm_w": gain(ks[6], (DEPTH, HEAD_DIM)),
        "pool_w": nrm(ks[7], (DEPTH, N_POOL_GROUPS, POOL_GROUP_DIM, POOL_GROUP_DIM), POOL_GROUP_DIM ** -0.5),
        "pool_scale": gain(ks[8], (DEPTH, D_POOL)),
        "w_out": nrm(ks[9], (DEPTH, D_MODEL, D_MODEL), D_MODEL ** -0.5),
        "norm_ffn_w": gain(ks[10], (DEPTH, D_MODEL)),
        "w_up": nrm(ks[11], (DEPTH, D_MODEL, 2 * D_FF), D_MODEL ** -0.5),
        "conv_ffn_w": nrm(ks[12], (DEPTH, CONV_FFN, D_FF), CONV_FFN ** -0.5),
        "conv_ffn_b": nrm(ks[13], (DEPTH, D_FF), 0.02),
        "w_down": nrm(ks[14], (DEPTH, D_FF, D_MODEL), D_FF ** -0.5),
        "norm_final_w": gain(ks[15], (D_MODEL,)),
    }


def _fwd_reference(x, norm_mix_w, w_in, conv_qkv_w, a_log, dt_bias, gdn_norm_w, pool_w, pool_scale,
              w_out, norm_ffn_w, w_up, conv_ffn_w, conv_ffn_b, w_down, norm_final_w):
    bsz, seq, _ = x.shape
    splits = np.cumsum([D_LIN, D_LIN, D_LIN, D_LIN, N_HEADS, N_HEADS, D_POOL, D_MODEL]).tolist()
    for l in range(DEPTH):
        h = rmsnorm(x, norm_mix_w[l])
        proj = jnp.einsum('bsd,de->bse', h, w_in[l])
        q, k, v, z, b_raw, a_raw, p_in, g_a, g_b = jnp.split(proj, splits, axis=-1)

        qkv = jax.nn.silu(causal_dwconv(jnp.concatenate([q, k, v], axis=-1), conv_qkv_w[l]))
        q, k, v = jnp.split(qkv, 3, axis=-1)
        q = l2norm(q.reshape(bsz, seq, N_HEADS, HEAD_DIM)) * (HEAD_DIM ** -0.5)
        k = l2norm(k.reshape(bsz, seq, N_HEADS, HEAD_DIM))
        v = v.reshape(bsz, seq, N_HEADS, HEAD_DIM)
        beta = jax.nn.sigmoid(b_raw.astype(jnp.float32))
        g_log = -jnp.exp(a_log[l].astype(jnp.float32)) * jax.nn.softplus(
            a_raw.astype(jnp.float32) + dt_bias[l].astype(jnp.float32))
        o = gated_delta_rule(q, k, v, g_log, beta).astype(x.dtype)
        o = rmsnorm(o, gdn_norm_w[l]) * jax.nn.silu(z.reshape(bsz, seq, N_HEADS, HEAD_DIM))
        y_a = o.reshape(bsz, seq, D_LIN)

        pooled = causal_multiscale_pool(p_in.reshape(bsz, seq, N_POOL_GROUPS, POOL_GROUP_DIM))
        y_b = jnp.einsum('bsgc,gcd->bsgd', pooled, pool_w[l]).reshape(bsz, seq, D_POOL) * pool_scale[l]

        mixed = jax.nn.sigmoid(g_a) * y_a + jax.nn.sigmoid(g_b) * y_b
        x = x + jnp.einsum('bsd,de->bse', mixed, w_out[l])

        h = rmsnorm(x, norm_ffn_w[l])
        gate, up = jnp.split(jnp.einsum('bsd,df->bsf', h, w_up[l]), 2, axis=-1)
        gate = causal_dwconv(gate, conv_ffn_w[l]) + conv_ffn_b[l]
        x = x + jnp.einsum('bsf,fd->bsd', jax.nn.gelu(gate, approximate=False) * up, w_down[l])
    return rmsnorm(x, norm_final_w)


import jax as _jax
import jax.numpy as _jnp

TWIN_FORMAT = 'train_step'
FWD_PARAMS = ['x', 'norm_mix_w', 'w_in', 'conv_qkv_w', 'a_log', 'dt_bias', 'gdn_norm_w', 'pool_w', 'pool_scale', 'w_out', 'norm_ffn_w', 'w_up', 'conv_ffn_w', 'conv_ffn_b', 'w_down', 'norm_final_w']
TWIN_WEIGHTS = ['norm_mix_w', 'w_in', 'conv_qkv_w', 'a_log', 'dt_bias', 'gdn_norm_w', 'pool_w', 'pool_scale', 'w_out', 'norm_ffn_w', 'w_up', 'conv_ffn_w', 'conv_ffn_b', 'w_down', 'norm_final_w']
TWIN_DIFF_INPUT = 'x'
TWIN_INPUTS = ['x', 'norm_mix_w', 'w_in', 'conv_qkv_w', 'a_log', 'dt_bias', 'gdn_norm_w', 'pool_w', 'pool_scale', 'w_out', 'norm_ffn_w', 'w_up', 'conv_ffn_w', 'conv_ffn_b', 'w_down', 'norm_final_w', 'loss_target', 'm_norm_mix_w', 'm_w_in', 'm_conv_qkv_w', 'm_a_log', 'm_dt_bias', 'm_gdn_norm_w', 'm_pool_w', 'm_pool_scale', 'm_w_out', 'm_norm_ffn_w', 'm_w_up', 'm_conv_ffn_w', 'm_conv_ffn_b', 'm_w_down', 'm_norm_final_w', 'v_norm_mix_w', 'v_w_in', 'v_conv_qkv_w', 'v_a_log', 'v_dt_bias', 'v_gdn_norm_w', 'v_pool_w', 'v_pool_scale', 'v_w_out', 'v_norm_ffn_w', 'v_w_up', 'v_conv_ffn_w', 'v_conv_ffn_b', 'v_w_down', 'v_norm_final_w']
TWIN_OUTPUTS = ['loss', 'grad_x', 'grad_norm_mix_w', 'grad_w_in', 'grad_conv_qkv_w', 'grad_a_log', 'grad_dt_bias', 'grad_gdn_norm_w', 'grad_pool_w', 'grad_pool_scale', 'grad_w_out', 'grad_norm_ffn_w', 'grad_w_up', 'grad_conv_ffn_w', 'grad_conv_ffn_b', 'grad_w_down', 'grad_norm_final_w', 'delta_norm_mix_w', 'delta_w_in', 'delta_conv_qkv_w', 'delta_a_log', 'delta_dt_bias', 'delta_gdn_norm_w', 'delta_pool_w', 'delta_pool_scale', 'delta_w_out', 'delta_norm_ffn_w', 'delta_w_up', 'delta_conv_ffn_w', 'delta_conv_ffn_b', 'delta_w_down', 'delta_norm_final_w', 'new_m_norm_mix_w', 'new_m_w_in', 'new_m_conv_qkv_w', 'new_m_a_log', 'new_m_dt_bias', 'new_m_gdn_norm_w', 'new_m_pool_w', 'new_m_pool_scale', 'new_m_w_out', 'new_m_norm_ffn_w', 'new_m_w_up', 'new_m_conv_ffn_w', 'new_m_conv_ffn_b', 'new_m_w_down', 'new_m_norm_final_w', 'new_v_norm_mix_w', 'new_v_w_in', 'new_v_conv_qkv_w', 'new_v_a_log', 'new_v_dt_bias', 'new_v_gdn_norm_w', 'new_v_pool_w', 'new_v_pool_scale', 'new_v_w_out', 'new_v_norm_ffn_w', 'new_v_w_up', 'new_v_conv_ffn_w', 'new_v_conv_ffn_b', 'new_v_w_down', 'new_v_norm_final_w']
TWIN_LEAF_KINDS = {'loss': 'loss', 'grad_x': 'grad_x', 'grad_norm_mix_w': 'grad_w', 'grad_w_in': 'grad_w', 'grad_conv_qkv_w': 'grad_w', 'grad_a_log': 'grad_w', 'grad_dt_bias': 'grad_w', 'grad_gdn_norm_w': 'grad_w', 'grad_pool_w': 'grad_w', 'grad_pool_scale': 'grad_w', 'grad_w_out': 'grad_w', 'grad_norm_ffn_w': 'grad_w', 'grad_w_up': 'grad_w', 'grad_conv_ffn_w': 'grad_w', 'grad_conv_ffn_b': 'grad_w', 'grad_w_down': 'grad_w', 'grad_norm_final_w': 'grad_w', 'delta_norm_mix_w': 'delta_w', 'delta_w_in': 'delta_w', 'delta_conv_qkv_w': 'delta_w', 'delta_a_log': 'delta_w', 'delta_dt_bias': 'delta_w', 'delta_gdn_norm_w': 'delta_w', 'delta_pool_w': 'delta_w', 'delta_pool_scale': 'delta_w', 'delta_w_out': 'delta_w', 'delta_norm_ffn_w': 'delta_w', 'delta_w_up': 'delta_w', 'delta_conv_ffn_w': 'delta_w', 'delta_conv_ffn_b': 'delta_w', 'delta_w_down': 'delta_w', 'delta_norm_final_w': 'delta_w', 'new_m_norm_mix_w': 'new_m', 'new_m_w_in': 'new_m', 'new_m_conv_qkv_w': 'new_m', 'new_m_a_log': 'new_m', 'new_m_dt_bias': 'new_m', 'new_m_gdn_norm_w': 'new_m', 'new_m_pool_w': 'new_m', 'new_m_pool_scale': 'new_m', 'new_m_w_out': 'new_m', 'new_m_norm_ffn_w': 'new_m', 'new_m_w_up': 'new_m', 'new_m_conv_ffn_w': 'new_m', 'new_m_conv_ffn_b': 'new_m', 'new_m_w_down': 'new_m', 'new_m_norm_final_w': 'new_m', 'new_v_norm_mix_w': 'new_v', 'new_v_w_in': 'new_v', 'new_v_conv_qkv_w': 'new_v', 'new_v_a_log': 'new_v', 'new_v_dt_bias': 'new_v', 'new_v_gdn_norm_w': 'new_v', 'new_v_pool_w': 'new_v', 'new_v_pool_scale': 'new_v', 'new_v_w_out': 'new_v', 'new_v_norm_ffn_w': 'new_v', 'new_v_w_up': 'new_v', 'new_v_conv_ffn_w': 'new_v', 'new_v_conv_ffn_b': 'new_v', 'new_v_w_down': 'new_v', 'new_v_norm_final_w': 'new_v'}


def _forward(args):
    return _fwd_reference(*[args[k] for k in FWD_PARAMS])


def _output_shape():
    def fwd():
        inp = _fwd_setup_inputs(0)
        return _fwd_reference(*[inp[k] for k in FWD_PARAMS])
    out = _jax.eval_shape(fwd)
    return out.shape, out.dtype

N_MICROBATCH = 1
ADAM_LR = 0.001
ADAM_B1 = 0.9
ADAM_B2 = 0.999
ADAM_EPS = 1e-08
ADAM_WD = 0.01
ADAM_STEP = 10
PER_EXAMPLE_BATCH_AXIS = {'x': 0, 'loss_target': 0}
SHARED_INPUTS = []
_WEIGHT_DTYPES = {'norm_mix_w': _jnp.float32, 'w_in': _jnp.float32, 'conv_qkv_w': _jnp.float32, 'a_log': _jnp.float32, 'dt_bias': _jnp.float32, 'gdn_norm_w': _jnp.float32, 'pool_w': _jnp.float32, 'pool_scale': _jnp.float32, 'w_out': _jnp.float32, 'norm_ffn_w': _jnp.float32, 'w_up': _jnp.float32, 'conv_ffn_w': _jnp.float32, 'conv_ffn_b': _jnp.float32, 'w_down': _jnp.float32, 'norm_final_w': _jnp.float32}
MOMENT_SCALE = {'norm_mix_w': 6.300979e-02, 'w_in': 2.345386e-02, 'conv_qkv_w': 2.034755e-02, 'a_log': 1.132392e-01, 'dt_bias': 1.085807e-01, 'gdn_norm_w': 9.894071e-02, 'pool_w': 3.941816e-02, 'pool_scale': 3.962196e-02, 'w_out': 4.720887e-02, 'norm_ffn_w': 6.282593e-02, 'w_up': 2.723054e-02, 'conv_ffn_w': 2.765726e-02, 'conv_ffn_b': 2.649845e-02, 'w_down': 4.445139e-02, 'norm_final_w': 1.599134e+01}


def _to_microbatches(a, axis):
    t = _jnp.moveaxis(a, axis, 0)
    t = t.reshape((N_MICROBATCH, t.shape[0] // N_MICROBATCH) + t.shape[1:])
    return _jnp.moveaxis(t, 1, axis + 1)


def setup_inputs(seed: int = 0) -> dict:
    inp = _fwd_setup_inputs(seed)
    key = _jax.random.fold_in(_jax.random.key(seed), 7919)
    shape, _ = _output_shape()
    out = dict(inp)
    out["loss_target"] = _jax.random.normal(_jax.random.fold_in(key, 0), shape, _jnp.float32)
    for i, name in enumerate(TWIN_WEIGHTS):
        w = inp[name].astype(_jnp.float32)
        if MOMENT_SCALE is None:
            s = _jnp.sqrt(_jnp.mean(_jnp.square(w)) + 1e-30)
        else:
            s = MOMENT_SCALE[name]
        km, kv = _jax.random.split(_jax.random.fold_in(key, i + 1))
        out[name] = w
        out["m_" + name] = s * _jax.random.normal(km, w.shape, _jnp.float32)
        out["v_" + name] = (s * s) * _jax.random.uniform(kv, w.shape, _jnp.float32, 0.5, 1.5)
    if N_MICROBATCH > 1:
        for name, axis in PER_EXAMPLE_BATCH_AXIS.items():
            out[name] = _to_microbatches(out[name], axis)
    return {'x': out['x'], 'norm_mix_w': out['norm_mix_w'], 'w_in': out['w_in'], 'conv_qkv_w': out['conv_qkv_w'], 'a_log': out['a_log'], 'dt_bias': out['dt_bias'], 'gdn_norm_w': out['gdn_norm_w'], 'pool_w': out['pool_w'], 'pool_scale': out['pool_scale'], 'w_out': out['w_out'], 'norm_ffn_w': out['norm_ffn_w'], 'w_up': out['w_up'], 'conv_ffn_w': out['conv_ffn_w'], 'conv_ffn_b': out['conv_ffn_b'], 'w_down': out['w_down'], 'norm_final_w': out['norm_final_w'], 'loss_target': out['loss_target'], 'm_norm_mix_w': out['m_norm_mix_w'], 'm_w_in': out['m_w_in'], 'm_conv_qkv_w': out['m_conv_qkv_w'], 'm_a_log': out['m_a_log'], 'm_dt_bias': out['m_dt_bias'], 'm_gdn_norm_w': out['m_gdn_norm_w'], 'm_pool_w': out['m_pool_w'], 'm_pool_scale': out['m_pool_scale'], 'm_w_out': out['m_w_out'], 'm_norm_ffn_w': out['m_norm_ffn_w'], 'm_w_up': out['m_w_up'], 'm_conv_ffn_w': out['m_conv_ffn_w'], 'm_conv_ffn_b': out['m_conv_ffn_b'], 'm_w_down': out['m_w_down'], 'm_norm_final_w': out['m_norm_final_w'], 'v_norm_mix_w': out['v_norm_mix_w'], 'v_w_in': out['v_w_in'], 'v_conv_qkv_w': out['v_conv_qkv_w'], 'v_a_log': out['v_a_log'], 'v_dt_bias': out['v_dt_bias'], 'v_gdn_norm_w': out['v_gdn_norm_w'], 'v_pool_w': out['v_pool_w'], 'v_pool_scale': out['v_pool_scale'], 'v_w_out': out['v_w_out'], 'v_norm_ffn_w': out['v_norm_ffn_w'], 'v_w_up': out['v_w_up'], 'v_conv_ffn_w': out['v_conv_ffn_w'], 'v_conv_ffn_b': out['v_conv_ffn_b'], 'v_w_down': out['v_w_down'], 'v_norm_final_w': out['v_norm_final_w']}


def _loss(weights, diff, rest, loss_target):
    with _jax.named_scope("forward"):
        args = {**rest, TWIN_DIFF_INPUT: diff, **{k: w.astype(_WEIGHT_DTYPES[k]) for k, w in weights.items()}}
        y = _forward(args)
    with _jax.named_scope("loss_head"):
        err = _jnp.square(y.astype(_jnp.float32) - loss_target)
        return 0.5 * _jnp.sum(_jnp.mean(err, axis=-1)) if err.ndim else 0.5 * err


def _adamw(w, g, m, v):
    m = ADAM_B1 * m + (1.0 - ADAM_B1) * g
    v = ADAM_B2 * v + (1.0 - ADAM_B2) * _jnp.square(g)
    m_hat = m / (1.0 - ADAM_B1 ** ADAM_STEP)
    v_hat = v / (1.0 - ADAM_B2 ** ADAM_STEP)
    delta = -ADAM_LR * (m_hat / (_jnp.sqrt(v_hat) + ADAM_EPS) + ADAM_WD * w)
    return delta, m, v


def reference(x, norm_mix_w, w_in, conv_qkv_w, a_log, dt_bias, gdn_norm_w, pool_w, pool_scale, w_out, norm_ffn_w, w_up, conv_ffn_w, conv_ffn_b, w_down, norm_final_w, loss_target, m_norm_mix_w, m_w_in, m_conv_qkv_w, m_a_log, m_dt_bias, m_gdn_norm_w, m_pool_w, m_pool_scale, m_w_out, m_norm_ffn_w, m_w_up, m_conv_ffn_w, m_conv_ffn_b, m_w_down, m_norm_final_w, v_norm_mix_w, v_w_in, v_conv_qkv_w, v_a_log, v_dt_bias, v_gdn_norm_w, v_pool_w, v_pool_scale, v_w_out, v_norm_ffn_w, v_w_up, v_conv_ffn_w, v_conv_ffn_b, v_w_down, v_norm_final_w):
    given = dict(x=x, norm_mix_w=norm_mix_w, w_in=w_in, conv_qkv_w=conv_qkv_w, a_log=a_log, dt_bias=dt_bias, gdn_norm_w=gdn_norm_w, pool_w=pool_w, pool_scale=pool_scale, w_out=w_out, norm_ffn_w=norm_ffn_w, w_up=w_up, conv_ffn_w=conv_ffn_w, conv_ffn_b=conv_ffn_b, w_down=w_down, norm_final_w=norm_final_w, loss_target=loss_target, m_norm_mix_w=m_norm_mix_w, m_w_in=m_w_in, m_conv_qkv_w=m_conv_qkv_w, m_a_log=m_a_log, m_dt_bias=m_dt_bias, m_gdn_norm_w=m_gdn_norm_w, m_pool_w=m_pool_w, m_pool_scale=m_pool_scale, m_w_out=m_w_out, m_norm_ffn_w=m_norm_ffn_w, m_w_up=m_w_up, m_conv_ffn_w=m_conv_ffn_w, m_conv_ffn_b=m_conv_ffn_b, m_w_down=m_w_down, m_norm_final_w=m_norm_final_w, v_norm_mix_w=v_norm_mix_w, v_w_in=v_w_in, v_conv_qkv_w=v_conv_qkv_w, v_a_log=v_a_log, v_dt_bias=v_dt_bias, v_gdn_norm_w=v_gdn_norm_w, v_pool_w=v_pool_w, v_pool_scale=v_pool_scale, v_w_out=v_w_out, v_norm_ffn_w=v_norm_ffn_w, v_w_up=v_w_up, v_conv_ffn_w=v_conv_ffn_w, v_conv_ffn_b=v_conv_ffn_b, v_w_down=v_w_down, v_norm_final_w=v_norm_final_w)
    weights = {n: given[n] for n in TWIN_WEIGHTS}
    shared = {n: given[n] for n in SHARED_INPUTS}
    per_example = {n: given[n] for n in ['x']}
    grad_fn = _jax.value_and_grad(_loss, argnums=(0, 1))

    def one_microbatch(ex, loss_target):
        ex = dict(ex)
        diff = ex.pop(TWIN_DIFF_INPUT)
        return grad_fn(weights, diff, {**shared, **ex}, loss_target)

    if N_MICROBATCH == 1:
        loss, (grad_w, grad_x) = one_microbatch(per_example, given["loss_target"])
    else:
        def body(carry, xs):
            loss_sum, grad_sum = carry
            l_k, (gw_k, gx_k) = one_microbatch(xs[0], xs[1])
            with _jax.named_scope("update"):
                return (loss_sum + l_k, _jax.tree.map(_jnp.add, grad_sum, gw_k)), gx_k

        init = (_jnp.zeros((), _jnp.float32), _jax.tree.map(_jnp.zeros_like, weights))
        (loss, grad_w), grad_x = _jax.lax.scan(body, init, (per_example, given["loss_target"]))
    with _jax.named_scope("update"):
        delta_w, new_m, new_v = {}, {}, {}
        for n in TWIN_WEIGHTS:
            delta_w[n], new_m[n], new_v[n] = _adamw(weights[n], grad_w[n], given["m_" + n], given["v_" + n])
    return (loss, grad_x, *[grad_w[n] for n in TWIN_WEIGHTS], *[delta_w[n] for n in TWIN_WEIGHTS],
            *[new_m[n] for n in TWIN_WEIGHTS], *[new_v[n] for n in TWIN_WEIGHTS])
```

```python
import functools
import math

import jax
import jax.numpy as jnp
from jax import lax
from jax.experimental import pallas as pl
from jax.experimental.pallas import tpu as pltpu

F32 = jnp.float32
BF16 = jnp.bfloat16
EPS = 1e-6
CHUNK = 64
HEAD_DIM = 128
POOL_WINDOWS = (2, 4, 8, 16)
HALO = 16
ADAM_LR, ADAM_B1, ADAM_B2, ADAM_EPS, ADAM_WD, ADAM_STEP = 0.001, 0.9, 0.999, 1e-08, 0.01, 10
V7X_VMEM_LIMIT = 56 * 1024 * 1024
SUM_BLOCK_BYTES = 6 * 1024 * 1024
MAX_COPY_BYTES = 8 * 1024 * 1024
MESH = pl.DeviceIdType.MESH


def _tile(n, cap, align=128):
    if n <= cap:
        return n
    t = (cap // align) * align
    while t >= align:
        if n % t == 0:
            return t
        t -= align
    return n


def _params(*sem):
    return pltpu.CompilerParams(dimension_semantics=sem, vmem_limit_bytes=V7X_VMEM_LIMIT)


def _sigmoid(x):
    return 1.0 / (1.0 + jnp.exp(-x))


def _down(x, j):
    return pltpu.roll(x, j, 0)


def _up(x, j):
    return pltpu.roll(x, x.shape[0] - j, 0)


def _fold8(x):
    n, c = x.shape
    return jnp.sum(x.reshape(n // 8, 8, c), axis=0)


def _matmul(a, b, *, ta=False, tb=False, add=None, out_dtype=F32, b_k0=0, b_n0=0, n=None, a_part=None, b_part=None, name):
    a2, b2 = a.shape[-2:], b.shape[-2:]
    m, k = (a2[1], a2[0]) if ta else a2
    if n is None:
        n = b2[0] if tb else b2[1]
    tm, tn, tk = _tile(m, 1024), _tile(n, 1024), _tile(k, 512)
    nk = k // tk
    assert b_k0 % tk == 0 and b_n0 % tn == 0, (b_k0, b_n0, tk, tn)
    ko, no = b_k0 // tk, b_n0 // tn

    def spec(shape, index, part):
        if part is None:
            return pl.BlockSpec(shape, index)
        return pl.BlockSpec((None,) + shape, lambda i, j, kk: (part,) + index(i, j, kk))

    a_spec = spec((tk, tm), lambda i, j, kk: (kk, i), a_part) if ta else spec((tm, tk), lambda i, j, kk: (i, kk), a_part)
    b_spec = (spec((tn, tk), lambda i, j, kk: (j + no, kk + ko), b_part) if tb
              else spec((tk, tn), lambda i, j, kk: (kk + ko, j + no), b_part))
    o_spec = pl.BlockSpec((tm, tn), lambda i, j, kk: (i, j))
    dims = (((0 if ta else 1,), (1 if tb else 0,)), ((), ()))
    has_add = add is not None

    def body(*refs):
        if has_add:
            a_ref, b_ref, add_ref, o_ref, acc = refs
        else:
            a_ref, b_ref, o_ref, acc = refs
        kk = pl.program_id(2)

        @pl.when(kk == 0)
        def _():
            acc[...] = jnp.zeros_like(acc)

        acc[...] += lax.dot_general(a_ref[...].astype(BF16), b_ref[...].astype(BF16), dims,
                                    preferred_element_type=F32)

        @pl.when(kk == nk - 1)
        def _():
            r = acc[...]
            if has_add:
                r = r + add_ref[...]
            o_ref[...] = r.astype(out_dtype)

    ins = [a, b] + ([add] if has_add else [])
    specs = [a_spec, b_spec] + ([o_spec] if has_add else [])
    return pl.pallas_call(
        body, name=name, grid=(m // tm, n // tn, nk), in_specs=specs, out_specs=o_spec,
        out_shape=jax.ShapeDtypeStruct((m, n), out_dtype),
        scratch_shapes=[pltpu.VMEM((tm, tn), F32)],
        compiler_params=_params("parallel", "parallel", "arbitrary"))(*ins)


def _rmsnorm_fwd(x, w, *, name):
    s, d = x.shape
    ts = _tile(s, 512, 8)

    def body(x_ref, w_ref, o_ref):
        xv = x_ref[...]
        r = lax.rsqrt(jnp.mean(xv * xv, axis=-1, keepdims=True) + EPS)
        o_ref[...] = (xv * r * w_ref[...]).astype(BF16)

    return pl.pallas_call(
        body, name=name, grid=(s // ts,),
        in_specs=[pl.BlockSpec((ts, d), lambda i: (i, 0)), pl.BlockSpec((1, d), lambda i: (0, 0))],
        out_specs=pl.BlockSpec((ts, d), lambda i: (i, 0)),
        out_shape=jax.ShapeDtypeStruct((s, d), BF16), compiler_params=_params("parallel"))(x, w)


def _rmsnorm_bwd(x, w, dh, dres, *, name):
    s, d = x.shape
    ts = _tile(s, 512, 8)

    def body(x_ref, w_ref, dh_ref, dres_ref, dx_ref, dw_ref):
        xv = x_ref[...]
        r = lax.rsqrt(jnp.mean(xv * xv, axis=-1, keepdims=True) + EPS)
        xh = xv * r
        dhv = dh_ref[...]
        dxh = dhv * w_ref[...]
        dx_ref[...] = dres_ref[...] + r * (dxh - xh * jnp.mean(dxh * xh, axis=-1, keepdims=True))

        @pl.when(pl.program_id(0) == 0)
        def _():
            dw_ref[...] = jnp.zeros_like(dw_ref)

        dw_ref[...] += _fold8(dhv * xh)

    row = pl.BlockSpec((ts, d), lambda i: (i, 0))
    return pl.pallas_call(
        body, name=name, grid=(s // ts,),
        in_specs=[row, pl.BlockSpec((1, d), lambda i: (0, 0)), row, row],
        out_specs=[row, pl.BlockSpec((8, d), lambda i: (0, 0))],
        out_shape=[jax.ShapeDtypeStruct((s, d), F32), jax.ShapeDtypeStruct((8, d), F32)],
        compiler_params=_params("arbitrary"))(x, w, dh, dres)


def _final_loss(x, w, target, *, name):
    s, d = x.shape
    ts = _tile(s, 512, 8)

    def body(x_ref, w_ref, t_ref, dx_ref, dw_ref, loss_ref):
        xv = x_ref[...]
        r = lax.rsqrt(jnp.mean(xv * xv, axis=-1, keepdims=True) + EPS)
        xh = xv * r
        wv = w_ref[...]
        err = xh * wv - t_ref[...]
        dy = err * (1.0 / d)
        dxh = dy * wv
        dx_ref[...] = r * (dxh - xh * jnp.mean(dxh * xh, axis=-1, keepdims=True))

        @pl.when(pl.program_id(0) == 0)
        def _():
            dw_ref[...] = jnp.zeros_like(dw_ref)
            loss_ref[...] = jnp.zeros_like(loss_ref)

        dw_ref[...] += _fold8(dy * xh)
        e2 = _fold8(err * err)
        part = e2[:, 0:128]
        for j in range(1, d // 128):
            part = part + e2[:, j * 128:(j + 1) * 128]
        loss_ref[...] += part

    row = pl.BlockSpec((ts, d), lambda i: (i, 0))
    return pl.pallas_call(
        body, name=name, grid=(s // ts,),
        in_specs=[row, pl.BlockSpec((1, d), lambda i: (0, 0)), row],
        out_specs=[row, pl.BlockSpec((8, d), lambda i: (0, 0)), pl.BlockSpec((8, 128), lambda i: (0, 0))],
        out_shape=[jax.ShapeDtypeStruct((s, d), F32), jax.ShapeDtypeStruct((8, d), F32),
                   jax.ShapeDtypeStruct((8, 128), F32)],
        compiler_params=_params("arbitrary"))(x, w, target)


def _prev_spec(ts, tc, col):
    return pl.BlockSpec((HALO, tc), lambda i, j: (jnp.maximum(i * (ts // HALO) - 1, 0), col(j)))


def _next_spec(ts, tc, col, n_tiles):
    return pl.BlockSpec((HALO, tc), lambda i, j: (jnp.minimum((i + 1) * (ts // HALO), n_tiles * (ts // HALO) - 1), col(j)))


def _with_prev(prev, cur, first):
    return jnp.concatenate([jnp.where(first, 0.0, prev), cur], axis=0)


def _with_next(cur, nxt, last):
    return jnp.concatenate([cur, jnp.where(last, 0.0, nxt)], axis=0)


def _gelu(x):
    return 0.5 * x * (1.0 + lax.erf(x * (1.0 / math.sqrt(2.0))))


def _gelu_grad(x):
    return 0.5 * (1.0 + lax.erf(x * (1.0 / math.sqrt(2.0)))) + x * jnp.exp(-0.5 * x * x) * (1.0 / math.sqrt(2.0 * math.pi))


def _ffn_conv(prev, cur, w, first):
    xx = _with_prev(prev, cur, first)
    return w[2:3] * cur + w[1:2] * _down(xx, 1)[HALO:] + w[0:1] * _down(xx, 2)[HALO:]


def _ffn_mid_fwd(gu, cw, cb, *, name):
    s, f2 = gu.shape
    f = f2 // 2
    ts, tc = _tile(s, 512, HALO), _tile(f, 512)
    nj = f // tc

    def body(g_ref, gp_ref, u_ref, w_ref, b_ref, o_ref):
        first = pl.program_id(0) == 0
        gc = _ffn_conv(gp_ref[...], g_ref[...], w_ref[...], first) + b_ref[...]
        o_ref[...] = (_gelu(gc) * u_ref[...]).astype(BF16)

    return pl.pallas_call(
        body, name=name, grid=(s // ts, nj),
        in_specs=[pl.BlockSpec((ts, tc), lambda i, j: (i, j)), _prev_spec(ts, tc, lambda j: j),
                  pl.BlockSpec((ts, tc), lambda i, j: (i, j + nj)),
                  pl.BlockSpec((3, tc), lambda i, j: (0, j)), pl.BlockSpec((1, tc), lambda i, j: (0, j))],
        out_specs=pl.BlockSpec((ts, tc), lambda i, j: (i, j)),
        out_shape=jax.ShapeDtypeStruct((s, f), BF16), compiler_params=_params("parallel", "parallel"))(gu, gu, gu, cw, cb)


def _ffn_mid_bwd(gu, cw, cb, dact, *, name):
    s, f2 = gu.shape
    f = f2 // 2
    ts, tc = _tile(s, 512, HALO), _tile(f, 512)
    nj, ni = f // tc, s // ts

    def body(g_ref, gp_ref, gn_ref, u_ref, un_ref, d_ref, dn_ref, w_ref, b_ref, dg_ref, du_ref, dw_ref, db_ref):
        i = pl.program_id(0)
        first, last = i == 0, i == ni - 1
        w, b = w_ref[...], b_ref[...]
        g = g_ref[...]
        gx = jnp.concatenate([jnp.where(first, 0.0, gp_ref[...]), g, jnp.where(last, 0.0, gn_ref[...])], axis=0)
        g1, g2 = _down(gx, 1), _down(gx, 2)
        gc = (w[2:3] * gx + w[1:2] * g1 + w[0:1] * g2)[HALO:] + b
        ux = _with_next(u_ref[...], un_ref[...], last)
        dx = _with_next(d_ref[...], dn_ref[...], last)
        dgc = dx * ux * _gelu_grad(gc)
        du_ref[...] = (dx[:ts] * _gelu(gc[:ts])).astype(BF16)
        dg = w[2:3] * dgc + w[1:2] * _up(dgc, 1) + w[0:1] * _up(dgc, 2)
        dg_ref[...] = dg[:ts].astype(BF16)
        dgt = dgc[:ts]
        db_ref[...] = _fold8(dgt)
        dw_ref[:, 0:tc] = _fold8(dgt * g2[HALO:HALO + ts])
        dw_ref[:, tc:2 * tc] = _fold8(dgt * g1[HALO:HALO + ts])
        dw_ref[:, 2 * tc:3 * tc] = _fold8(dgt * g)

    cur = lambda off: pl.BlockSpec((ts, tc), lambda i, j: (i, j + off))
    return pl.pallas_call(
        body, name=name, grid=(ni, nj),
        in_specs=[cur(0), _prev_spec(ts, tc, lambda j: j), _next_spec(ts, tc, lambda j: j, ni),
                  cur(nj), _next_spec(ts, tc, lambda j: j + nj, ni),
                  cur(0), _next_spec(ts, tc, lambda j: j, ni),
                  pl.BlockSpec((3, tc), lambda i, j: (0, j)), pl.BlockSpec((1, tc), lambda i, j: (0, j))],
        out_specs=[cur(0), cur(0),
                   pl.BlockSpec((8, 3 * tc), lambda i, j: (i, j)), pl.BlockSpec((8, tc), lambda i, j: (i, j))],
        out_shape=[jax.ShapeDtypeStruct((s, f), BF16), jax.ShapeDtypeStruct((s, f), BF16),
                   jax.ShapeDtypeStruct((ni * 8, 3 * f), F32), jax.ShapeDtypeStruct((ni * 8, f), F32)],
        compiler_params=_params("parallel", "parallel"))(gu, gu, gu, gu, gu, dact, dact, cw, cb)


def _qkv_fwd(pq, cw, n_heads, *, name):
    s, c3 = pq.shape
    ts = _tile(s, 512, HALO)

    def body(x_ref, xp_ref, w_ref, o_ref):
        j = pl.program_id(1)
        w = w_ref[...]
        x = x_ref[...]
        xx = _with_prev(xp_ref[...], x, pl.program_id(0) == 0)
        y = w[3:4] * x + w[2:3] * _down(xx, 1)[HALO:] + w[1:2] * _down(xx, 2)[HALO:] + w[0:1] * _down(xx, 3)[HALO:]
        c = y * _sigmoid(y)
        r = lax.rsqrt(jnp.sum(c * c, axis=-1, keepdims=True) + EPS)
        scale = jnp.where(j < n_heads, HEAD_DIM ** -0.5, 1.0)
        o_ref[...] = jnp.where(j < 2 * n_heads, c * (r * scale), c)

    return pl.pallas_call(
        body, name=name, grid=(s // ts, 3 * n_heads),
        in_specs=[pl.BlockSpec((ts, HEAD_DIM), lambda i, j: (i, j)), _prev_spec(ts, HEAD_DIM, lambda j: j),
                  pl.BlockSpec((4, HEAD_DIM), lambda i, j: (0, j))],
        out_specs=pl.BlockSpec((None, ts, HEAD_DIM), lambda i, j: (j // n_heads, i, j % n_heads)),
        out_shape=jax.ShapeDtypeStruct((3, s, c3 // 3), F32),
        compiler_params=_params("parallel", "parallel"))(pq, pq, cw)


def _qkv_bwd(pq, cw, dqkv3, n_heads, *, name):
    s, c3 = pq.shape
    ts = _tile(s, 512, HALO)
    ni = s // ts
    hd = HEAD_DIM

    def body(x_ref, xp_ref, xn_ref, w_ref, d_ref, dn_ref, dx_ref, dw_ref):
        i, j = pl.program_id(0), pl.program_id(1)
        first, last = i == 0, i == ni - 1
        w = w_ref[...]
        x = x_ref[...]
        xx = jnp.concatenate([jnp.where(first, 0.0, xp_ref[...]), x, jnp.where(last, 0.0, xn_ref[...])], axis=0)
        x1, x2, x3 = _down(xx, 1), _down(xx, 2), _down(xx, 3)
        y = (w[3:4] * xx + w[2:3] * x1 + w[1:2] * x2 + w[0:1] * x3)[HALO:]
        sg = _sigmoid(y)
        c = y * sg
        dn = _with_next(d_ref[...], dn_ref[...], last)
        r = lax.rsqrt(jnp.sum(c * c, axis=-1, keepdims=True) + EPS)
        nrm = c * r
        dnn = dn * jnp.where(j < n_heads, hd ** -0.5, 1.0)
        dc = jnp.where(j < 2 * n_heads, r * (dnn - nrm * jnp.sum(dnn * nrm, axis=-1, keepdims=True)), dn)
        dy = dc * (sg * (1.0 + y * (1.0 - sg)))
        dx = w[3:4] * dy + w[2:3] * _up(dy, 1) + w[1:2] * _up(dy, 2) + w[0:1] * _up(dy, 3)
        dx_ref[...] = dx[:ts].astype(BF16)
        dyt = dy[:ts]
        dw_ref[:, 0:hd] = _fold8(dyt * x3[HALO:HALO + ts])
        dw_ref[:, hd:2 * hd] = _fold8(dyt * x2[HALO:HALO + ts])
        dw_ref[:, 2 * hd:3 * hd] = _fold8(dyt * x1[HALO:HALO + ts])
        dw_ref[:, 3 * hd:4 * hd] = _fold8(dyt * x)

    dspec = lambda rows, row_index: pl.BlockSpec(
        (None, rows, hd), lambda i, j: (j // n_heads, row_index(i), j % n_heads))
    return pl.pallas_call(
        body, name=name, grid=(ni, 3 * n_heads),
        in_specs=[pl.BlockSpec((ts, hd), lambda i, j: (i, j)), _prev_spec(ts, hd, lambda j: j),
                  _next_spec(ts, hd, lambda j: j, ni), pl.BlockSpec((4, hd), lambda i, j: (0, j)),
                  dspec(ts, lambda i: i),
                  dspec(HALO, lambda i: jnp.minimum((i + 1) * (ts // HALO), ni * (ts // HALO) - 1))],
        out_specs=[pl.BlockSpec((ts, hd), lambda i, j: (i, j)), pl.BlockSpec((8, 4 * hd), lambda i, j: (i, j))],
        out_shape=[jax.ShapeDtypeStruct((s, c3), BF16), jax.ShapeDtypeStruct((ni * 8, 4 * c3), F32)],
        compiler_params=_params("parallel", "parallel"))(pq, pq, pq, cw, dqkv3, dqkv3)


def _gate_terms(ba, al, dt, h, n_heads):
    lane = lax.broadcasted_iota(jnp.int32, ba.shape, 1)
    braw = jnp.sum(jnp.where(lane == h, ba, 0.0), axis=1, keepdims=True)
    araw = jnp.sum(jnp.where(lane == h + n_heads, ba, 0.0), axis=1, keepdims=True)
    beta = _sigmoid(braw)
    z = araw + dt
    sp = jnp.maximum(z, 0.0) + jnp.log(1.0 + jnp.exp(-jnp.abs(z)))
    ea = jnp.exp(jnp.zeros((1, 1), F32) + al)
    return beta, z, sp, ea


def _gates_fwd(ba, a_log, dt_bias, n_heads, *, name):
    s = ba.shape[0]
    ts = _tile(s, 512, CHUNK)

    def body(ba_ref, al_ref, dt_ref, o_ref):
        h = pl.program_id(1)
        beta, _, sp, ea = _gate_terms(ba_ref[...], al_ref[h], dt_ref[h], h, n_heads)
        gx = jnp.broadcast_to(-ea * sp, (ts, HEAD_DIM))
        rc = lax.broadcasted_iota(jnp.int32, (ts, HEAD_DIM), 0) & (CHUNK - 1)
        for sh in (1, 2, 4, 8, 16, 32):
            gx = gx + jnp.where(rc >= sh, _down(gx, sh), 0.0)
        o_ref[0] = jnp.broadcast_to(beta, (ts, HEAD_DIM))
        o_ref[1] = gx

    smem = pl.BlockSpec(memory_space=pltpu.SMEM)
    return pl.pallas_call(
        body, name=name, grid=(s // ts, n_heads),
        in_specs=[pl.BlockSpec((ts, 128), lambda i, h: (i, 0)), smem, smem],
        out_specs=pl.BlockSpec((2, ts, HEAD_DIM), lambda i, h: (0, i, h)),
        out_shape=jax.ShapeDtypeStruct((2, s, n_heads * HEAD_DIM), F32),
        compiler_params=_params("parallel", "parallel"))(ba, a_log, dt_bias)


def _gates_bwd(ba, a_log, dt_bias, dbg, n_heads, *, name):
    s = ba.shape[0]
    ts = _tile(s, 512, CHUNK)
    ni = s // ts

    def body(ba_ref, al_ref, dt_ref, d_ref, o_ref, p_ref):
        h = pl.program_id(1)
        beta, z, sp, ea = _gate_terms(ba_ref[...], al_ref[h], dt_ref[h], h, n_heads)
        dg = d_ref[1]
        rc = lax.broadcasted_iota(jnp.int32, (ts, HEAD_DIM), 0) & (CHUNK - 1)
        for sh in (1, 2, 4, 8, 16, 32):
            dg = dg + jnp.where(rc < CHUNK - sh, _up(dg, sh), 0.0)
        daraw = dg * (-ea * _sigmoid(z))
        dbraw = d_ref[0] * (beta * (1.0 - beta))

        @pl.when(h == 0)
        def _():
            o_ref[...] = jnp.zeros_like(o_ref)
            p_ref[...] = jnp.zeros_like(p_ref)

        lane = lax.broadcasted_iota(jnp.int32, (1, 128), 1)
        is_b, is_a = lane == h, lane == h + n_heads
        o_ref[...] += jnp.where(is_b, dbraw, 0.0) + jnp.where(is_a, daraw, 0.0)
        p_ref[...] += jnp.where(is_b, _fold8(dg * (-ea * sp)), 0.0) + jnp.where(is_a, _fold8(daraw), 0.0)

    smem = pl.BlockSpec(memory_space=pltpu.SMEM)
    return pl.pallas_call(
        body, name=name, grid=(ni, n_heads),
        in_specs=[pl.BlockSpec((ts, 128), lambda i, h: (i, 0)), smem, smem,
                  pl.BlockSpec((2, ts, HEAD_DIM), lambda i, h: (0, i, h))],
        out_specs=[pl.BlockSpec((ts, 128), lambda i, h: (i, 0)), pl.BlockSpec((8, 128), lambda i, h: (i, 0))],
        out_shape=[jax.ShapeDtypeStruct((s, 128), F32), jax.ShapeDtypeStruct((ni * 8, 128), F32)],
        compiler_params=_params("parallel", "arbitrary"))(ba, a_log, dt_bias, dbg)


BLK = 2 * CHUNK
NN = (((1,), (0,)), ((), ()))
NT = (((1,), (1,)), ((), ()))
TN = (((0,), (0,)), ((), ()))


def _dot(a, b, dims):
    return lax.dot_general(a.astype(BF16), b.astype(BF16), dims, preferred_element_type=F32)


def _dot3(a, b, dims):
    ah, bh = a.astype(BF16), b.astype(BF16)
    al, bl = (a - ah.astype(F32)).astype(BF16), (b - bh.astype(F32)).astype(BF16)
    d = lambda u, v: lax.dot_general(u, v, dims, preferred_element_type=F32)
    return d(ah, bh) + (d(ah, bl) + d(al, bh))


def _pair_masks():
    row = lax.broadcasted_iota(jnp.int32, (BLK, BLK), 0)
    col = lax.broadcasted_iota(jnp.int32, (BLK, BLK), 1)
    same = (row < CHUNK) == (col < CHUNK)
    return same & (row >= col), same & (row > col), row == col


def _pair_terms(q, k, v, b, gam, masks):
    tril, strict, eye = masks
    g_cols = jnp.sum(jnp.where(eye, gam, 0.0), axis=0, keepdims=True)
    dmat = jnp.exp(jnp.where(tril, gam - g_cols, -jnp.inf))
    eg = jnp.exp(gam)
    rowi = lax.broadcasted_iota(jnp.int32, (BLK, HEAD_DIM), 0)
    elast = jnp.exp(jnp.where(rowi < CHUNK, gam[CHUNK - 1:CHUNK], gam[BLK - 1:BLK]) - gam)
    kb, vb = k * b, v * b
    kq = _dot(jnp.concatenate([kb, q], axis=0), k, NT)
    lmat = jnp.where(strict, kq[:BLK] * dmat, 0.0)
    attn = kq[BLK:] * dmat
    rhs = jnp.concatenate([vb, kb * eg], axis=1)
    return dmat, eg, elast, kb, lmat, attn, rhs


def _unit_lower_inverse(lmat, eye):
    p = -lmat
    t = jnp.where(eye, 1.0, 0.0) + p
    p = _dot3(p, p, NN)
    for _ in range(4):
        r = _dot3(jnp.concatenate([p, t], axis=0), p, NN)
        p, t = r[:BLK], t + r[BLK:]
    return t + _dot3(t, p, NN)


def _gdn_fwd(qkv3, bg, *, name):
    _, s, dl = qkv3.shape
    n_heads = dl // HEAD_DIM
    sb = _tile(s, 1024, BLK)
    npair = sb // BLK
    c = CHUNK

    def body(qkv_ref, bg_ref, o_ref, st_ref, ti_ref, s_scr):
        @pl.when(pl.program_id(1) == 0)
        def _():
            s_scr[...] = jnp.zeros_like(s_scr)

        masks = _pair_masks()

        def pair(p, state):
            rows = pl.ds(pl.multiple_of(p * BLK, BLK), BLK)
            q, k, v = qkv_ref[0, rows, :], qkv_ref[1, rows, :], qkv_ref[2, rows, :]
            b, gam = bg_ref[0, rows, :], bg_ref[1, rows, :]
            _, eg, elast, _, lmat, attn, rhs = _pair_terms(q, k, v, b, gam, masks)
            tinv = _unit_lower_inverse(lmat, masks[2])
            ti_ref[rows, :] = tinv
            sol = _dot3(tinv, rhs, NN)
            u, w = sol[:, :HEAD_DIM], sol[:, HEAD_DIM:]
            qd, ke = q * eg, k * elast
            st_ref[2 * p] = state
            wq = _dot(jnp.concatenate([w[:c], qd[:c]], axis=0), state, NN)
            vn_a, o_a = u[:c] - wq[:c], wq[c:]
            state = state * jnp.exp(gam[c - 1:c]) + _dot(ke[:c], vn_a, TN)
            st_ref[2 * p + 1] = state
            wq = _dot(jnp.concatenate([w[c:], qd[c:]], axis=0), state, NN)
            vn_b, o_b = u[c:] - wq[:c], wq[c:]
            state = state * jnp.exp(gam[BLK - 1:BLK]) + _dot(ke[c:], vn_b, TN)
            o_ref[rows, :] = jnp.concatenate([o_a, o_b], axis=0) + _dot(attn, jnp.concatenate([vn_a, vn_b], axis=0), NN)
            return state

        s_scr[...] = lax.fori_loop(0, npair, pair, s_scr[...])

    return pl.pallas_call(
        body, name=name, grid=(n_heads, s // sb),
        in_specs=[pl.BlockSpec((3, sb, HEAD_DIM), lambda h, j: (0, j, h)),
                  pl.BlockSpec((2, sb, HEAD_DIM), lambda h, j: (0, j, h))],
        out_specs=[pl.BlockSpec((sb, HEAD_DIM), lambda h, j: (j, h)),
                   pl.BlockSpec((None, 2 * npair, HEAD_DIM, HEAD_DIM), lambda h, j: (h, j, 0, 0)),
                   pl.BlockSpec((None, sb, BLK), lambda h, j: (h, j, 0))],
        out_shape=[jax.ShapeDtypeStruct((s, dl), F32),
                   jax.ShapeDtypeStruct((n_heads, s // c, HEAD_DIM, HEAD_DIM), F32),
                   jax.ShapeDtypeStruct((n_heads, s, BLK), F32)],
        scratch_shapes=[pltpu.VMEM((HEAD_DIM, HEAD_DIM), F32)],
        compiler_params=_params("parallel", "arbitrary"))(qkv3, bg)


def _gdn_bwd(qkv3, bg, st, ti, d_o, *, name):
    _, s, dl = qkv3.shape
    n_heads = dl // HEAD_DIM
    sb = _tile(s, 1024, BLK)
    npair, nsb = sb // BLK, s // sb
    c = CHUNK

    def body(qkv_ref, bg_ref, st_ref, ti_ref, do_ref, dqkv_ref, dbg_ref, ds_scr):
        @pl.when(pl.program_id(1) == 0)
        def _():
            ds_scr[...] = jnp.zeros_like(ds_scr)

        masks = _pair_masks()
        tril, strict, eye = masks
        rowc = lax.broadcasted_iota(jnp.int32, (BLK, 1), 0)

        def total(x):
            return jnp.sum(jnp.sum(x, axis=1, keepdims=True), axis=0, keepdims=True)

        def pair(pp, ds2):
            p = npair - 1 - pp
            rows = pl.ds(pl.multiple_of(p * BLK, BLK), BLK)
            q, k, v = qkv_ref[0, rows, :], qkv_ref[1, rows, :], qkv_ref[2, rows, :]
            b, gam = bg_ref[0, rows, :], bg_ref[1, rows, :]
            tinv, dout = ti_ref[rows, :], do_ref[rows, :]
            s0, s1 = st_ref[2 * p], st_ref[2 * p + 1]
            dmat, eg, elast, kb, lmat, attn, rhs = _pair_terms(q, k, v, b, gam, masks)
            sol = _dot3(tinv, rhs, NN)
            u, w = sol[:, :HEAD_DIM], sol[:, HEAD_DIM:]
            qd, ke = q * eg, k * elast
            dec_a, dec_b = jnp.exp(gam[c - 1:c]), jnp.exp(gam[BLK - 1:BLK])
            vn = u - jnp.concatenate([_dot(w[:c], s0, NN), _dot(w[c:], s1, NN)], axis=0)
            dvn_o = _dot(attn, dout, TN)
            dvn_b = dvn_o[c:] + _dot(ke[c:], ds2, NN)
            ds1 = _dot(qd[c:], dout[c:], TN) + ds2 * dec_b - _dot(w[c:], dvn_b, TN)
            dvn_a = dvn_o[:c] + _dot(ke[:c], ds1, NN)
            ds0 = _dot(qd[:c], dout[:c], TN) + ds1 * dec_a - _dot(w[:c], dvn_a, TN)
            dvn = jnp.concatenate([dvn_a, dvn_b], axis=0)
            dke = jnp.concatenate([_dot(vn[:c], ds1, NT), _dot(vn[c:], ds2, NT)], axis=0)
            dw = -jnp.concatenate([_dot(dvn_a, s0, NT), _dot(dvn_b, s1, NT)], axis=0)
            dqd = jnp.concatenate([_dot(dout[:c], s0, NT), _dot(dout[c:], s1, NT)], axis=0)
            dattn = jnp.where(tril, _dot(dout, vn, NT), 0.0)
            drhs = _dot3(tinv, jnp.concatenate([dvn, dw], axis=1), TN)
            dl_ = jnp.where(strict, -_dot3(drhs, sol, NT), 0.0)
            dm, dqk = dl_ * dmat, dattn * dmat
            dvb, drw = drhs[:, :HEAD_DIM], drhs[:, HEAD_DIM:]
            dkb = _dot(dm, k, NN) + drw * eg
            dq = _dot(dqk, k, NN) + dqd * eg
            dk = _dot(dm, kb, TN) + _dot(dqk, q, TN) + dke * elast + dkb * b
            dbeta = jnp.sum(dvb * v + dkb * k, axis=1, keepdims=True)
            e = dl_ * lmat + dattn * attn
            e_cols = jnp.sum(jnp.where(eye, jnp.sum(e, axis=0, keepdims=True), 0.0), axis=1, keepdims=True)
            dke_ke = dke * ke
            dgam = (jnp.sum(e, axis=1, keepdims=True) - e_cols
                    + jnp.sum(drw * (kb * eg) + dqd * qd - dke_ke, axis=1, keepdims=True))
            tot_a = total(dke_ke[:c]) + total(s0 * ds1) * dec_a[:, :1]
            tot_b = total(dke_ke[c:]) + total(s1 * ds2) * dec_b[:, :1]
            dgam = dgam + jnp.where(rowc == c - 1, tot_a, 0.0) + jnp.where(rowc == BLK - 1, tot_b, 0.0)
            dqkv_ref[0, rows, :] = dq
            dqkv_ref[1, rows, :] = dk
            dqkv_ref[2, rows, :] = dvb * b
            dbg_ref[0, rows, :] = jnp.broadcast_to(dbeta, (BLK, HEAD_DIM))
            dbg_ref[1, rows, :] = jnp.broadcast_to(dgam, (BLK, HEAD_DIM))
            return ds0

        ds_scr[...] = lax.fori_loop(0, npair, pair, ds_scr[...])

    rev = lambda j: nsb - 1 - j
    return pl.pallas_call(
        body, name=name, grid=(n_heads, nsb),
        in_specs=[pl.BlockSpec((3, sb, HEAD_DIM), lambda h, j: (0, rev(j), h)),
                  pl.BlockSpec((2, sb, HEAD_DIM), lambda h, j: (0, rev(j), h)),
                  pl.BlockSpec((None, 2 * npair, HEAD_DIM, HEAD_DIM), lambda h, j: (h, rev(j), 0, 0)),
                  pl.BlockSpec((None, sb, BLK), lambda h, j: (h, rev(j), 0)),
                  pl.BlockSpec((sb, HEAD_DIM), lambda h, j: (rev(j), h))],
        out_specs=[pl.BlockSpec((3, sb, HEAD_DIM), lambda h, j: (0, rev(j), h)),
                   pl.BlockSpec((2, sb, HEAD_DIM), lambda h, j: (0, rev(j), h))],
        out_shape=[jax.ShapeDtypeStruct((3, s, dl), F32), jax.ShapeDtypeStruct((2, s, dl), F32)],
        scratch_shapes=[pltpu.VMEM((HEAD_DIM, HEAD_DIM), F32)],
        compiler_params=_params("parallel", "arbitrary"))(qkv3, bg, st, ti, d_o)


N_GROUPS = len(POOL_WINDOWS)


def _pick(g, vals):
    out = vals[-1]
    for i in range(len(vals) - 2, -1, -1):
        out = jnp.where(g == i, vals[i], out)
    return out


def _head_norm(o, nw):
    hats, outs = [], []
    for h in range(o.shape[1] // HEAD_DIM):
        sl = slice(h * HEAD_DIM, (h + 1) * HEAD_DIM)
        oh = o[:, sl]
        r = lax.rsqrt(jnp.mean(oh * oh, axis=-1, keepdims=True) + EPS)
        hats.append((oh * r, r))
        outs.append(oh * r * nw[:, sl])
    return hats, jnp.concatenate(outs, axis=1) if len(outs) > 1 else outs[0]


def _pool_counts(g, t0, n):
    t = (lax.broadcasted_iota(jnp.int32, (n, 1), 0) + t0 + 1).astype(F32)
    return jnp.minimum(t, _pick(g, [float(w) for w in POOL_WINDOWS]))


def _pool(prev, cur, first, g, t0):
    s = _with_prev(prev, cur, first)
    sums = []
    for sh in (1, 2, 4, 8):
        s = s + _down(s, sh)
        sums.append(s)
    return _pick(g, sums)[HALO:] / _pool_counts(g, t0, cur.shape[0]) - cur


def _mix_specs(ts, gw, ni):
    seg = lambda k: pl.BlockSpec((ts, gw), lambda g, i: (i, k * N_GROUPS + g))
    per = ts // HALO
    prev = lambda k: pl.BlockSpec((HALO, gw), lambda g, i: (jnp.maximum(i * per - 1, 0), k * N_GROUPS + g))
    nxt = lambda k: pl.BlockSpec((HALO, gw), lambda g, i: (jnp.minimum((i + 1) * per, ni * per - 1), k * N_GROUPS + g))
    vec = pl.BlockSpec((1, gw), lambda g, i: (0, g))
    pw = pl.BlockSpec((None, gw, gw), lambda g, i: (g, 0, 0))
    return seg, prev, nxt, vec, pw


def _mix_fwd(o, zpg, nw, pw, ps, *, name):
    s, d = o.shape
    gw = d // N_GROUPS
    ts = _tile(s, 512, HALO)
    ni = s // ts
    seg, prev, _, vec, pwspec = _mix_specs(ts, gw, ni)

    def body(o_ref, z_ref, p_ref, pp_ref, ga_ref, gb_ref, nw_ref, pw_ref, ps_ref, out_ref):
        g, i = pl.program_id(0), pl.program_id(1)
        _, on = _head_norm(o_ref[...], nw_ref[...])
        z = z_ref[...]
        ya = on * (z * _sigmoid(z))
        pooled = _pool(pp_ref[...], p_ref[...], i == 0, g, i * ts)
        yb = _dot(pooled, pw_ref[...], NN) * ps_ref[...]
        out_ref[...] = (_sigmoid(ga_ref[...]) * ya + _sigmoid(gb_ref[...]) * yb).astype(BF16)

    return pl.pallas_call(
        body, name=name, grid=(N_GROUPS, ni),
        in_specs=[seg(0), seg(0), seg(1), prev(1), seg(2), seg(3), vec, pwspec, vec],
        out_specs=seg(0), out_shape=jax.ShapeDtypeStruct((s, d), BF16),
        compiler_params=_params("parallel", "parallel"))(o, zpg, zpg, zpg, zpg, zpg, nw, pw, ps)


def _mix_bwd(o, zpg, nw, pw, ps, dmix, *, name):
    s, d = o.shape
    gw = d // N_GROUPS
    ts = _tile(s, 512, HALO)
    ni = s // ts
    seg, prev, nxt, vec, pwspec = _mix_specs(ts, gw, ni)

    def body(o_ref, z_ref, p_ref, pp_ref, ga_ref, gb_ref, gbn_ref, nw_ref, pw_ref, ps_ref, dm_ref, dmn_ref,
             do_ref, d4_ref, dpw_ref, dnw_ref, dps_ref):
        g, i = pl.program_id(0), pl.program_id(1)
        last = i == ni - 1
        nw, ps, pwv = nw_ref[...], ps_ref[...], pw_ref[...]
        ov, z = o_ref[...], z_ref[...]
        hats, on = _head_norm(ov, nw)
        sz = _sigmoid(z)
        silu = z * sz
        ya = on * silu
        pooled = _pool(pp_ref[...], p_ref[...], i == 0, g, i * ts)
        yp = _dot(pooled, pwv, NN)
        sga, sgb = _sigmoid(ga_ref[...]), _sigmoid(gb_ref[...])
        dm = dm_ref[...]
        dya, dyb = dm * sga, dm * sgb
        d4_ref[2] = (dm * ya * (sga * (1.0 - sga))).astype(BF16)
        d4_ref[3] = (dm * (yp * ps) * (sgb * (1.0 - sgb))).astype(BF16)
        dps_ref[...] = _fold8(dyb * yp)
        dyp = dyb * ps

        @pl.when(i == 0)
        def _():
            dpw_ref[...] = jnp.zeros_like(dpw_ref)

        dpw_ref[...] += _dot(pooled, dyp, TN)
        dyp_next = jnp.where(last, 0.0, dmn_ref[...] * _sigmoid(gbn_ref[...]) * ps)
        dpool = _dot(jnp.concatenate([dyp, dyp_next], axis=0), pwv, NT)
        a = dpool / _pool_counts(g, i * ts, ts + HALO)
        sums = []
        for sh in (1, 2, 4, 8):
            a = a + _up(a, sh)
            sums.append(a)
        d4_ref[1] = (_pick(g, sums)[:ts] - dpool[:ts]).astype(BF16)
        d4_ref[0] = (dya * on * (sz * (1.0 + z * (1.0 - sz)))).astype(BF16)
        don = dya * silu
        dos, dnws = [], []
        for h, (ohat, r) in enumerate(hats):
            sl = slice(h * HEAD_DIM, (h + 1) * HEAD_DIM)
            dxh = don[:, sl] * nw[:, sl]
            dos.append(r * (dxh - ohat * jnp.mean(dxh * ohat, axis=-1, keepdims=True)))
            dnws.append(_fold8(don[:, sl] * ohat))
        do_ref[...] = jnp.concatenate(dos, axis=1) if len(dos) > 1 else dos[0]
        dnw_ref[...] = jnp.concatenate(dnws, axis=1) if len(dnws) > 1 else dnws[0]

    part = pl.BlockSpec((8, gw), lambda g, i: (i, g))
    return pl.pallas_call(
        body, name=name, grid=(N_GROUPS, ni),
        in_specs=[seg(0), seg(0), seg(1), prev(1), seg(2), seg(3), nxt(3), vec, pwspec, vec, seg(0), nxt(0)],
        out_specs=[seg(0), pl.BlockSpec((4, ts, gw), lambda g, i: (0, i, g)),
                   pl.BlockSpec((None, gw, gw), lambda g, i: (g, 0, 0)), part, part],
        out_shape=[jax.ShapeDtypeStruct((s, d), F32), jax.ShapeDtypeStruct((4, s, d), BF16),
                   jax.ShapeDtypeStruct((N_GROUPS, gw, gw), F32),
                   jax.ShapeDtypeStruct((ni * 8, d), F32), jax.ShapeDtypeStruct((ni * 8, d), F32)],
        compiler_params=_params("parallel", "arbitrary"))(o, zpg, zpg, zpg, zpg, zpg, zpg, nw, pw, ps, dmix, dmix)


def _adamw(w, g, m, v, *, name):
    r, c = w.shape
    tr = _tile(r, max(8, (1 << 19) // c // 8 * 8), 8)

    def body(w_ref, g_ref, m_ref, v_ref, d_ref, mo_ref, vo_ref):
        gv = g_ref[...]
        mn = ADAM_B1 * m_ref[...] + (1.0 - ADAM_B1) * gv
        vn = ADAM_B2 * v_ref[...] + (1.0 - ADAM_B2) * (gv * gv)
        m_hat = mn / (1.0 - ADAM_B1 ** ADAM_STEP)
        v_hat = vn / (1.0 - ADAM_B2 ** ADAM_STEP)
        d_ref[...] = -ADAM_LR * (m_hat / (jnp.sqrt(v_hat) + ADAM_EPS) + ADAM_WD * w_ref[...])
        mo_ref[...] = mn
        vo_ref[...] = vn

    blk = pl.BlockSpec((tr, c), lambda i: (i, 0))
    return pl.pallas_call(
        body, name=name, grid=(r // tr,), in_specs=[blk] * 4, out_specs=[blk] * 3,
        out_shape=[jax.ShapeDtypeStruct((r, c), F32)] * 3, compiler_params=_params("parallel"))(w, g, m, v)


def _sum_parts(x, *, name):
    p, r, c = x.shape
    tc = 128 if c % 128 == 0 else c
    tr = _tile(r, max(16, SUM_BLOCK_BYTES // (p * tc * x.dtype.itemsize)), 16)

    def body(x_ref, o_ref):
        acc = x_ref[0].astype(F32)
        for i in range(1, p):
            acc = acc + x_ref[i].astype(F32)
        o_ref[...] = acc

    return pl.pallas_call(
        body, name=name, grid=(r // tr, c // tc),
        in_specs=[pl.BlockSpec((p, tr, tc), lambda i, j: (0, i, j))], out_specs=pl.BlockSpec((tr, tc), lambda i, j: (i, j)),
        out_shape=jax.ShapeDtypeStruct((r, c), F32), compiler_params=_params("parallel", "parallel"))(x)


_HBM = pl.BlockSpec(memory_space=pltpu.HBM)


def _place():
    return lax.axis_index("x"), lax.axis_index("y"), lax.axis_index("c")


def _allgather(x_shard, *, name):
    m_per, n = x_shard.shape

    def body(x_ref, out_ref, send_sems, recv_sems, local_sem):
        x, y, c = _place()
        me, sibling = (x, y, c), (x, y, 1 - c)
        chips = [(1 - x, y), (x, 1 - y), (1 - x, 1 - y)]

        def rows(px, py, pc):
            return out_ref.at[pl.ds((4 * px + 2 * py + pc) * m_per, m_per), :]

        def copy(k, block, to, src=None):
            return pltpu.make_async_remote_copy(
                src_ref=rows(*block) if src is None else src, dst_ref=rows(*block),
                send_sem=send_sems.at[k], recv_sem=recv_sems.at[k], device_id=to, device_id_type=MESH)

        mine = pltpu.make_async_copy(x_ref, rows(*me), local_sem)
        mine.start()
        first = [copy(0, me, sibling, src=x_ref)]
        first += [copy(1 + j, me, (*chip, c), src=x_ref) for j, chip in enumerate(chips)]
        for cp in first:
            cp.start()
        passed = [copy(4 + j, (*chip, c), sibling) for j, chip in enumerate(chips)]
        for j, chip in enumerate(chips):
            copy(1 + j, (*chip, c), me).wait_recv()
            passed[j].start()
        copy(0, sibling, me).wait_recv()
        for j, chip in enumerate(chips):
            copy(4 + j, (*chip, 1 - c), me).wait_recv()
        for cp in first + passed:
            cp.wait_send()
        mine.wait()

    return pl.pallas_call(
        body, name=name, out_shape=jax.ShapeDtypeStruct((8 * m_per, n), x_shard.dtype),
        in_specs=[_HBM], out_specs=_HBM,
        scratch_shapes=[pltpu.SemaphoreType.DMA((7,)), pltpu.SemaphoreType.DMA((7,)), pltpu.SemaphoreType.DMA])(x_shard)


def _all_to_all(parts, *, name):
    def body(g_ref, out_ref, send_sems, recv_sems, local_sem):
        x, y, c = _place()
        me = 4 * x + 2 * y + c
        mine = pltpu.make_async_copy(g_ref.at[me], out_ref.at[me], local_sem)
        mine.start()
        sends, peers = [], []
        for k in range(1, 8):
            px = 1 - x if k & 4 else x
            py = 1 - y if k & 2 else y
            pc = 1 - c if k & 1 else c
            peer = 4 * px + 2 * py + pc
            cp = pltpu.make_async_remote_copy(
                src_ref=g_ref.at[peer], dst_ref=out_ref.at[me], send_sem=send_sems.at[k - 1],
                recv_sem=recv_sems.at[k - 1], device_id=(px, py, pc), device_id_type=MESH)
            cp.start()
            sends.append(cp)
            peers.append((peer, (px, py, pc)))
        for k, (peer, pid) in enumerate(peers):
            pltpu.make_async_remote_copy(
                src_ref=g_ref.at[peer], dst_ref=out_ref.at[peer], send_sem=send_sems.at[k],
                recv_sem=recv_sems.at[k], device_id=pid, device_id_type=MESH).wait_recv()
        for cp in sends:
            cp.wait_send()
        mine.wait()

    return pl.pallas_call(
        body, name=name, out_shape=jax.ShapeDtypeStruct(parts.shape, parts.dtype), in_specs=[_HBM], out_specs=_HBM,
        scratch_shapes=[pltpu.SemaphoreType.DMA((7,)), pltpu.SemaphoreType.DMA((7,)), pltpu.SemaphoreType.DMA])(parts)


def _share_halves(half, *, name):
    r = half.shape[0]
    n_split = 1
    while half.size * half.dtype.itemsize > n_split * MAX_COPY_BYTES and r % (2 * n_split * 16) == 0:
        n_split *= 2
    rs = r // n_split

    def body(h_ref, out_ref, send_sems, recv_sems, local_sem):
        x, y, c = _place()
        mine = pltpu.make_async_copy(h_ref, out_ref.at[c], local_sem)
        mine.start()

        def copy(k, slot):
            rows = pl.ds(k * rs, rs)
            return pltpu.make_async_remote_copy(
                src_ref=h_ref.at[rows], dst_ref=out_ref.at[slot, rows], send_sem=send_sems.at[k],
                recv_sem=recv_sems.at[k], device_id=(x, y, 1 - c), device_id_type=MESH)

        sends = [copy(k, c) for k in range(n_split)]
        for cp in sends:
            cp.start()
        for k in range(n_split):
            copy(k, 1 - c).wait_recv()
        for cp in sends:
            cp.wait_send()
        mine.wait()

    return pl.pallas_call(
        body, name=name, out_shape=jax.ShapeDtypeStruct((2,) + half.shape, half.dtype), in_specs=[_HBM], out_specs=_HBM,
        scratch_shapes=[pltpu.SemaphoreType.DMA((n_split,)), pltpu.SemaphoreType.DMA((n_split,)),
                        pltpu.SemaphoreType.DMA])(half)


def _piece_rows(shape):
    n = math.prod(shape)
    if n % 128 == 0:
        return n // 128, 128
    assert shape[-1] <= 128, shape
    return n // shape[-1], shape[-1]


def _pack_small(arrs, row_multiple):
    pieces = []
    for a in arrs:
        rows, lanes = _piece_rows(a.shape)
        t = a.astype(F32).reshape(rows, lanes)
        pieces.append(jnp.pad(t, ((0, -rows % 8), (0, 128 - lanes))))
    buf = jnp.concatenate(pieces, axis=0)
    return jnp.pad(buf, ((0, -buf.shape[0] % row_multiple), (0, 0)))


def _unpack_small(buf, shapes):
    out, off = [], 0
    for shp in shapes:
        rows, lanes = _piece_rows(shp)
        out.append(buf[off:off + rows, :lanes].reshape(shp))
        off += rows + (-rows % 8)
    return out


def _w_in_grad_parts(g_qkv, g_zpg, g_ba, n_heads, *, name):
    d = g_qkv.shape[0]
    cw = (g_qkv.shape[1] + 4 * d + 2 * n_heads) // N_CHIPS
    tr = _tile(d // 2, 128, 16)
    per_half = d // 2 // tr

    def body(a_ref, z_ref, p_ref, ga_ref, gb_ref, ba_ref, o_ref):
        full = jnp.concatenate([a_ref[...], z_ref[...], ba_ref[...][:, :2 * n_heads], p_ref[...], ga_ref[...],
                                gb_ref[...]], axis=1)
        for j in range(N_CHIPS):
            o_ref[j] = full[:, cw * j:cw * (j + 1)]

    row = lambda c: pl.BlockSpec((tr, c), lambda i: (i, 0))
    out = pl.pallas_call(
        body, name=name, grid=(d // tr,),
        in_specs=[row(g_qkv.shape[1]), row(d), row(d), row(d), row(d), row(128)],
        out_specs=pl.BlockSpec((N_CHIPS, None, tr, cw), lambda i: (0, i // per_half, i % per_half, 0)),
        out_shape=jax.ShapeDtypeStruct((N_CHIPS, 2, d // 2, cw), BF16),
        compiler_params=_params("parallel"))(g_qkv, *g_zpg, g_ba)
    return out.reshape(8, d // 2, cw)


def _layer_fwd(x, p, n_heads):
    h = _rmsnorm_fwd(x, p["norm_mix_w"], name="norm_mix_fwd")
    pq = _matmul(h, p["w_qkv"], name="proj_qkv")
    zpg = _matmul(h, p["w_zpg"], name="proj_zpg")
    ba = _matmul(h, p["w_ba"], name="proj_ba")
    qkv3 = _qkv_fwd(pq, p["conv_qkv_w"], n_heads, name="qkv_fwd")
    bg = _gates_fwd(ba, p["a_log"], p["dt_bias"], n_heads, name="gates_fwd")
    o, st, ti = _gdn_fwd(qkv3, bg, name="gdn_fwd")
    mixed = _mix_fwd(o, zpg, p["gdn_nw"], p["pool_w"], p["pool_scale"], name="mix_fwd")
    x1 = _matmul(mixed, p["w_out"], add=x, name="out_proj")
    h2 = _rmsnorm_fwd(x1, p["norm_ffn_w"], name="norm_ffn_fwd")
    gu = _matmul(h2, p["w_up"], name="up_proj")
    act = _ffn_mid_fwd(gu, p["conv_ffn_w"], p["conv_ffn_b"], name="ffn_mid_fwd")
    x2 = _matmul(act, p["w_down"], add=x1, name="down_proj")
    return x2, (x, h, pq, zpg, ba, o, st, ti, mixed, x1, h2, gu, act)


def _layer_bwd(dx2, p, saved, n_heads):
    x, h, pq, zpg, ba, o, st, ti, mixed, x1, h2, gu, act = saved
    d = x.shape[1]
    f = act.shape[1]
    dact = _matmul(dx2, p["w_down"], tb=True, name="d_act")
    g_down = _matmul(act, dx2, ta=True, out_dtype=BF16, name="g_w_down")
    dgate, dup, dcw_p, dcb_p = _ffn_mid_bwd(gu, p["conv_ffn_w"], p["conv_ffn_b"], dact, name="ffn_mid_bwd")
    dh2 = _matmul(dgate, p["w_up"], tb=True, b_k0=0, name="d_h2_gate")
    dh2 = _matmul(dup, p["w_up"], tb=True, b_k0=f, add=dh2, name="d_h2_up")
    g_up = jnp.concatenate([_matmul(h2, dgate, ta=True, out_dtype=BF16, name="g_w_up_gate"),
                            _matmul(h2, dup, ta=True, out_dtype=BF16, name="g_w_up_up")], axis=1)
    dx1, dnf_p = _rmsnorm_bwd(x1, p["norm_ffn_w"], dh2, dx2, name="norm_ffn_bwd")
    dmix = _matmul(dx1, p["w_out"], tb=True, name="d_mixed")
    g_out = _matmul(mixed, dx1, ta=True, out_dtype=BF16, name="g_w_out")
    d_o, d4, g_pool, dnw_p, dps_p = _mix_bwd(o, zpg, p["gdn_nw"], p["pool_w"], p["pool_scale"], dmix, name="mix_bwd")
    qkv3 = _qkv_fwd(pq, p["conv_qkv_w"], n_heads, name="qkv_fwd")
    bg = _gates_fwd(ba, p["a_log"], p["dt_bias"], n_heads, name="gates_fwd")
    dqkv3, dbg = _gdn_bwd(qkv3, bg, st, ti, d_o, name="gdn_bwd")
    dpq, dcq_p = _qkv_bwd(pq, p["conv_qkv_w"], dqkv3, n_heads, name="qkv_bwd")
    dba, dgate_p = _gates_bwd(ba, p["a_log"], p["dt_bias"], dbg, n_heads, name="gates_bwd")
    dh = _matmul(dpq, p["w_qkv"], tb=True, name="d_h_qkv")
    for seg in range(4):
        dh = _matmul(d4, p["w_zpg"], tb=True, a_part=seg, b_k0=seg * d, add=dh, name="d_h_zpg")
    dh = _matmul(dba, p["w_ba"], tb=True, add=dh, name="d_h_ba")
    g_qkv = _matmul(h, dpq, ta=True, out_dtype=BF16, name="g_w_qkv")
    g_zpg = [_matmul(h, d4, ta=True, b_part=seg, out_dtype=BF16, name="g_w_zpg") for seg in range(4)]
    g_ba = _matmul(h, dba, ta=True, out_dtype=BF16, name="g_w_ba")
    dx, dnm_p = _rmsnorm_bwd(x, p["norm_mix_w"], dh, dx1, name="norm_mix_bwd")
    g_in = _w_in_grad_parts(g_qkv, g_zpg, g_ba, n_heads, name="g_w_in_parts")
    rows = lambda t: jnp.sum(t, axis=0)
    ni8, c12 = dcq_p.shape
    nj = dcw_p.shape[1] // (3 * _tile(f, 512))
    small = {
        "norm_mix_w": rows(dnm_p),
        "conv_qkv_w": rows(dcq_p).reshape(c12 // (4 * HEAD_DIM), 4, HEAD_DIM).transpose(1, 0, 2).reshape(4, c12 // 4),
        "a_log": rows(dgate_p)[:n_heads],
        "dt_bias": rows(dgate_p)[n_heads:2 * n_heads],
        "gdn_norm_w": jnp.sum(rows(dnw_p).reshape(d // HEAD_DIM, HEAD_DIM), axis=0),
        "pool_scale": rows(dps_p),
        "norm_ffn_w": rows(dnf_p),
        "conv_ffn_w": rows(dcw_p).reshape(nj, 3, f // nj).transpose(1, 0, 2).reshape(3, f),
        "conv_ffn_b": rows(dcb_p),
    }
    big = {"w_in": g_in, "pool_w": g_pool, "w_out": g_out, "w_up": g_up, "w_down": g_down}
    return dx, big, small


BIG = ("w_in", "pool_w", "w_out", "w_up", "w_down")
SMALL = ("norm_mix_w", "conv_qkv_w", "a_log", "dt_bias", "gdn_norm_w", "pool_scale", "norm_ffn_w", "conv_ffn_w",
         "conv_ffn_b", "norm_final_w")
WEIGHTS = ("norm_mix_w", "w_in", "conv_qkv_w", "a_log", "dt_bias", "gdn_norm_w", "pool_w", "pool_scale", "w_out",
           "norm_ffn_w", "w_up", "conv_ffn_w", "conv_ffn_b", "w_down", "norm_final_w")
N_CHIPS = 4


def _gather_shards(local, cc, *, name):
    m = local.shape[0] // 2
    return _allgather(lax.dynamic_slice_in_dim(local, cc * m, m, axis=0), name=name)


def _full_weights(w, l, cc):
    d = w["w_in"].shape[1]
    wi = _gather_shards(w["w_in"][l].astype(BF16), cc, name="gather_w_in").reshape(N_CHIPS, d, -1)
    pw = w["pool_w"][l].astype(BF16)
    g, r, c = pw.shape
    pool = _gather_shards(pw.reshape(g * r, c), cc, name="gather_pool_w").reshape(N_CHIPS, g, r, c)
    wu = _gather_shards(w["w_up"][l].astype(BF16), cc, name="gather_w_up").reshape(N_CHIPS, d, -1)
    return dict(
        w_in=jnp.concatenate([wi[j] for j in range(N_CHIPS)], axis=1),
        pool_w=pool.transpose(1, 0, 2, 3).reshape(g, N_CHIPS * r, c),
        w_out=_gather_shards(w["w_out"][l].astype(BF16), cc, name="gather_w_out"),
        w_up=wu.transpose(1, 0, 2).reshape(d, -1),
        w_down=_gather_shards(w["w_down"][l].astype(BF16), cc, name="gather_w_down"))


def _grad_parts(name, g):
    if name in ("w_in", "w_up"):
        r, c = g.shape
        return g.reshape(2, r // 2, N_CHIPS, c // N_CHIPS).transpose(2, 0, 1, 3).reshape(8, r // 2, c // N_CHIPS)
    if name == "pool_w":
        ng, r, c = g.shape
        t = g.reshape(2, ng // 2, N_CHIPS, r // N_CHIPS, c).transpose(2, 0, 1, 3, 4)
        return t.reshape(8, (ng // 2) * (r // N_CHIPS), c)
    r, c = g.shape
    return g.reshape(8, r // 8, c)


def _reduce_grad(name, g, shard_shape):
    parts = g if name == "w_in" else _grad_parts(name, g.astype(BF16))
    got = _all_to_all(parts, name="scatter_" + name)
    both = _share_halves(_sum_parts(got, name="sum_" + name), name="share_" + name)
    return both.reshape(shard_shape)


def kernel(x, norm_mix_w, w_in, conv_qkv_w, a_log, dt_bias, gdn_norm_w, pool_w, pool_scale, w_out, norm_ffn_w, w_up, conv_ffn_w, conv_ffn_b, w_down, norm_final_w, loss_target, m_norm_mix_w, m_w_in, m_conv_qkv_w, m_a_log, m_dt_bias, m_gdn_norm_w, m_pool_w, m_pool_scale, m_w_out, m_norm_ffn_w, m_w_up, m_conv_ffn_w, m_conv_ffn_b, m_w_down, m_norm_final_w, v_norm_mix_w, v_w_in, v_conv_qkv_w, v_a_log, v_dt_bias, v_gdn_norm_w, v_pool_w, v_pool_scale, v_w_out, v_norm_ffn_w, v_w_up, v_conv_ffn_w, v_conv_ffn_b, v_w_down, v_norm_final_w):
    w = dict(norm_mix_w=norm_mix_w, w_in=w_in, conv_qkv_w=conv_qkv_w, a_log=a_log, dt_bias=dt_bias,
             gdn_norm_w=gdn_norm_w, pool_w=pool_w, pool_scale=pool_scale, w_out=w_out, norm_ffn_w=norm_ffn_w,
             w_up=w_up, conv_ffn_w=conv_ffn_w, conv_ffn_b=conv_ffn_b, w_down=w_down, norm_final_w=norm_final_w)
    m = dict(norm_mix_w=m_norm_mix_w, w_in=m_w_in, conv_qkv_w=m_conv_qkv_w, a_log=m_a_log, dt_bias=m_dt_bias,
             gdn_norm_w=m_gdn_norm_w, pool_w=m_pool_w, pool_scale=m_pool_scale, w_out=m_w_out,
             norm_ffn_w=m_norm_ffn_w, w_up=m_w_up, conv_ffn_w=m_conv_ffn_w, conv_ffn_b=m_conv_ffn_b,
             w_down=m_w_down, norm_final_w=m_norm_final_w)
    v = dict(norm_mix_w=v_norm_mix_w, w_in=v_w_in, conv_qkv_w=v_conv_qkv_w, a_log=v_a_log, dt_bias=v_dt_bias,
             gdn_norm_w=v_gdn_norm_w, pool_w=v_pool_w, pool_scale=v_pool_scale, w_out=v_w_out,
             norm_ffn_w=v_norm_ffn_w, w_up=v_w_up, conv_ffn_w=v_conv_ffn_w, conv_ffn_b=v_conv_ffn_b,
             w_down=v_w_down, norm_final_w=v_norm_final_w)
    depth, n_heads = a_log.shape
    d = x.shape[-1]
    dl = n_heads * HEAD_DIM
    assert dl == d and gdn_norm_w.shape[1] == HEAD_DIM
    cx, cy, cc = _place()
    chip = 2 * cx + cy

    per_layer = [_full_weights(w, l, cc) for l in range(depth)]
    conv_packed = _pack_small([conv_qkv_w, conv_ffn_w], 16)
    conv_all = _allgather(conv_packed, name="gather_conv").reshape(N_CHIPS, 2, -1, 128)[:, 0]
    conv_j = [_unpack_small(conv_all[j], [conv_qkv_w.shape, conv_ffn_w.shape]) for j in range(N_CHIPS)]
    conv_q = jnp.concatenate([t[0] for t in conv_j], axis=-1)
    conv_f = jnp.concatenate([t[1] for t in conv_j], axis=-1)

    params = []
    for l in range(depth):
        full = per_layer[l]
        wi = full["w_in"]
        params.append(dict(
            norm_mix_w=norm_mix_w[l][None], norm_ffn_w=norm_ffn_w[l][None],
            w_qkv=wi[:, :3 * dl],
            w_zpg=jnp.concatenate([wi[:, 3 * dl:4 * dl], wi[:, 4 * dl + 2 * n_heads:]], axis=1),
            w_ba=jnp.concatenate([wi[:, 4 * dl:4 * dl + 2 * n_heads], jnp.zeros((d, 128 - 2 * n_heads), BF16)], axis=1),
            conv_qkv_w=conv_q[l], a_log=a_log[l], dt_bias=dt_bias[l],
            gdn_nw=jnp.tile(gdn_norm_w[l], d // HEAD_DIM)[None], pool_w=full["pool_w"], pool_scale=pool_scale[l][None],
            w_out=full["w_out"], w_up=full["w_up"], conv_ffn_w=conv_f[l], conv_ffn_b=conv_ffn_b[l][None],
            w_down=full["w_down"]))

    xs = x[0]
    saved = []
    for l in range(depth):
        xs, sv = _layer_fwd(xs, params[l], n_heads)
        saved.append(sv)
    dx, dnf_p, loss_p = _final_loss(xs, norm_final_w[None], loss_target[0], name="final_loss")
    loss = lax.psum(jnp.sum(loss_p) * (0.5 / d), ("x", "y", "c"))
    big_g = [None] * depth
    small_g = [None] * depth
    for l in reversed(range(depth)):
        dx, big_g[l], small_g[l] = _layer_bwd(dx, params[l], saved[l], n_heads)

    grads = {n: jnp.stack([_reduce_grad(n, big_g[l][n], w[n].shape[1:]) for l in range(depth)]) for n in BIG}

    small_shapes = {n: ((depth,) + small_g[0][n].shape if n != "norm_final_w" else (d,)) for n in SMALL}
    small_local = [jnp.stack([small_g[l][n] for l in range(depth)]) for n in SMALL[:-1]] + [jnp.sum(dnf_p, axis=0)]
    sp = _pack_small(small_local, 512)
    sg = _allgather(sp, name="gather_small").reshape(8, sp.shape[0], 128)
    small_sum = _unpack_small(_sum_parts(sg, name="sum_small"), [small_shapes[n] for n in SMALL])
    for n, g in zip(SMALL, small_sum):
        if n in ("conv_qkv_w", "conv_ffn_w"):
            cols = w[n].shape[-1]
            g = lax.dynamic_slice_in_dim(g, chip * cols, cols, axis=2)
        grads[n] = g

    delta, new_m, new_v = {}, {}, {}
    for n in BIG:
        shp = w[n].shape
        r2 = lambda t: t.reshape(-1, shp[-1])
        dd, mm, vv = _adamw(r2(w[n]), r2(grads[n]), r2(m[n]), r2(v[n]), name="adamw_" + n)
        delta[n], new_m[n], new_v[n] = dd.reshape(shp), mm.reshape(shp), vv.reshape(shp)
    pk = lambda src: _pack_small([src[n] for n in SMALL], 8)
    outs = _adamw(pk(w), pk(grads), pk(m), pk(v), name="adamw_small")
    for dst, buf in zip((delta, new_m, new_v), outs):
        for n, t in zip(SMALL, _unpack_small(buf, [w[n].shape for n in SMALL])):
            dst[n] = t

    return (loss, dx[None], *[grads[n] for n in WEIGHTS], *[delta[n] for n in WEIGHTS],
            *[new_m[n] for n in WEIGHTS], *[new_v[n] for n in WEIGHTS])
```

```python
import functools
import math

import jax
import jax.numpy as jnp
from jax import lax
from jax.experimental import pallas as pl
from jax.experimental.pallas import tpu as pltpu

F32 = jnp.float32
BF16 = jnp.bfloat16
EPS = 1e-6
CHUNK = 64
HEAD_DIM = 128
POOL_WINDOWS = (2, 4, 8, 16)
HALO = 16
ADAM_LR, ADAM_B1, ADAM_B2, ADAM_EPS, ADAM_WD, ADAM_STEP = 0.001, 0.9, 0.999, 1e-08, 0.01, 10
V7X_VMEM_LIMIT = 56 * 1024 * 1024
SUM_BLOCK_BYTES = 6 * 1024 * 1024
MAX_COPY_BYTES = 8 * 1024 * 1024
MATMUL_TILE_BYTES = 10 * 1024 * 1024
MESH = pl.DeviceIdType.MESH


def _tile(n, cap, align=128):
    if n <= cap:
        return n
    t = (cap // align) * align
    while t >= align:
        if n % t == 0:
            return t
        t -= align
    return n


def _params(*sem):
    return pltpu.CompilerParams(dimension_semantics=sem, vmem_limit_bytes=V7X_VMEM_LIMIT)


def _sigmoid(x):
    return 1.0 / (1.0 + jnp.exp(-x))


def _down(x, j):
    return pltpu.roll(x, j, 0)


def _up(x, j):
    return pltpu.roll(x, x.shape[0] - j, 0)


def _fold8(x):
    n, c = x.shape
    return jnp.sum(x.reshape(n // 8, 8, c), axis=0)


def _matmul(a, b, *, ta=False, tb=False, add=None, out_dtype=F32, b_k0=0, b_n0=0, n=None, a_part=None, b_part=None, name):
    a2, b2 = a.shape[-2:], b.shape[-2:]
    m, k = (a2[1], a2[0]) if ta else a2
    if n is None:
        n = b2[0] if tb else b2[1]
    tm, tn = _tile(m, 1024), _tile(n, 1024)
    per_k = tm * a.dtype.itemsize + tn * b.dtype.itemsize
    tk = _tile(k, max(128, MATMUL_TILE_BYTES // per_k // 128 * 128))
    nk = k // tk
    assert b_k0 % tk == 0 and b_n0 % tn == 0, (b_k0, b_n0, tk, tn)
    ko, no = b_k0 // tk, b_n0 // tn

    def spec(shape, index, part):
        if part is None:
            return pl.BlockSpec(shape, index)
        return pl.BlockSpec((None,) + shape, lambda i, j, kk: (part,) + index(i, j, kk))

    a_spec = spec((tk, tm), lambda i, j, kk: (kk, i), a_part) if ta else spec((tm, tk), lambda i, j, kk: (i, kk), a_part)
    b_spec = (spec((tn, tk), lambda i, j, kk: (j + no, kk + ko), b_part) if tb
              else spec((tk, tn), lambda i, j, kk: (kk + ko, j + no), b_part))
    o_spec = pl.BlockSpec((tm, tn), lambda i, j, kk: (i, j))
    dims = (((0 if ta else 1,), (1 if tb else 0,)), ((), ()))
    has_add = add is not None

    def body(*refs):
        a_ref, b_ref = refs[:2]
        add_ref = refs[2] if has_add else None
        o_ref = refs[3 if has_add else 2]
        part = lax.dot_general(a_ref[...].astype(BF16), b_ref[...].astype(BF16), dims, preferred_element_type=F32)

        def finish(r):
            if has_add:
                r = r + add_ref[...]
            o_ref[...] = r.astype(out_dtype)

        if nk == 1:
            finish(part)
            return
        acc = refs[-1]
        kk = pl.program_id(2)

        @pl.when(kk == 0)
        def _():
            acc[...] = part

        @pl.when(kk > 0)
        def _():
            acc[...] += part

        @pl.when(kk == nk - 1)
        def _():
            finish(acc[...])

    ins = [a, b] + ([add] if has_add else [])
    specs = [a_spec, b_spec] + ([o_spec] if has_add else [])
    return pl.pallas_call(
        body, name=name, grid=(m // tm, n // tn, nk), in_specs=specs, out_specs=o_spec,
        out_shape=jax.ShapeDtypeStruct((m, n), out_dtype),
        scratch_shapes=[pltpu.VMEM((tm, tn), F32)] if nk > 1 else [],
        compiler_params=_params("parallel", "parallel", "arbitrary"))(*ins)


def _rmsnorm_fwd(x, w, *, name):
    s, d = x.shape
    ts = _tile(s, 512, 8)

    def body(x_ref, w_ref, o_ref):
        xv = x_ref[...]
        r = lax.rsqrt(jnp.mean(xv * xv, axis=-1, keepdims=True) + EPS)
        o_ref[...] = (xv * r * w_ref[...]).astype(BF16)

    return pl.pallas_call(
        body, name=name, grid=(s // ts,),
        in_specs=[pl.BlockSpec((ts, d), lambda i: (i, 0)), pl.BlockSpec((1, d), lambda i: (0, 0))],
        out_specs=pl.BlockSpec((ts, d), lambda i: (i, 0)),
        out_shape=jax.ShapeDtypeStruct((s, d), BF16), compiler_params=_params("parallel"))(x, w)


def _rmsnorm_bwd(x, w, dh, dres, *, name):
    s, d = x.shape
    ts = _tile(s, 512, 8)

    def body(x_ref, w_ref, dh_ref, dres_ref, dx_ref, dw_ref):
        xv = x_ref[...]
        r = lax.rsqrt(jnp.mean(xv * xv, axis=-1, keepdims=True) + EPS)
        xh = xv * r
        dhv = dh_ref[...]
        dxh = dhv * w_ref[...]
        dx_ref[...] = dres_ref[...] + r * (dxh - xh * jnp.mean(dxh * xh, axis=-1, keepdims=True))

        @pl.when(pl.program_id(0) == 0)
        def _():
            dw_ref[...] = jnp.zeros_like(dw_ref)

        dw_ref[...] += _fold8(dhv * xh)

    row = pl.BlockSpec((ts, d), lambda i: (i, 0))
    return pl.pallas_call(
        body, name=name, grid=(s // ts,),
        in_specs=[row, pl.BlockSpec((1, d), lambda i: (0, 0)), row, row],
        out_specs=[row, pl.BlockSpec((8, d), lambda i: (0, 0))],
        out_shape=[jax.ShapeDtypeStruct((s, d), F32), jax.ShapeDtypeStruct((8, d), F32)],
        compiler_params=_params("arbitrary"))(x, w, dh, dres)


def _final_loss(x, w, target, *, name):
    s, d = x.shape
    ts = _tile(s, 512, 8)

    def body(x_ref, w_ref, t_ref, dx_ref, dw_ref, loss_ref):
        xv = x_ref[...]
        r = lax.rsqrt(jnp.mean(xv * xv, axis=-1, keepdims=True) + EPS)
        xh = xv * r
        wv = w_ref[...]
        err = xh * wv - t_ref[...]
        dy = err * (1.0 / d)
        dxh = dy * wv
        dx_ref[...] = r * (dxh - xh * jnp.mean(dxh * xh, axis=-1, keepdims=True))

        @pl.when(pl.program_id(0) == 0)
        def _():
            dw_ref[...] = jnp.zeros_like(dw_ref)
            loss_ref[...] = jnp.zeros_like(loss_ref)

        dw_ref[...] += _fold8(dy * xh)
        e2 = _fold8(err * err)
        part = e2[:, 0:128]
        for j in range(1, d // 128):
            part = part + e2[:, j * 128:(j + 1) * 128]
        loss_ref[...] += part

    row = pl.BlockSpec((ts, d), lambda i: (i, 0))
    return pl.pallas_call(
        body, name=name, grid=(s // ts,),
        in_specs=[row, pl.BlockSpec((1, d), lambda i: (0, 0)), row],
        out_specs=[row, pl.BlockSpec((8, d), lambda i: (0, 0)), pl.BlockSpec((8, 128), lambda i: (0, 0))],
        out_shape=[jax.ShapeDtypeStruct((s, d), F32), jax.ShapeDtypeStruct((8, d), F32),
                   jax.ShapeDtypeStruct((8, 128), F32)],
        compiler_params=_params("arbitrary"))(x, w, target)


def _prev_spec(ts, tc, col):
    return pl.BlockSpec((HALO, tc), lambda i, j: (jnp.maximum(i * (ts // HALO) - 1, 0), col(j)))


def _next_spec(ts, tc, col, n_tiles):
    return pl.BlockSpec((HALO, tc), lambda i, j: (jnp.minimum((i + 1) * (ts // HALO), n_tiles * (ts // HALO) - 1), col(j)))


def _with_prev(prev, cur, first):
    return jnp.concatenate([jnp.where(first, 0.0, prev), cur], axis=0)


def _with_next(cur, nxt, last):
    return jnp.concatenate([cur, jnp.where(last, 0.0, nxt)], axis=0)


def _gelu(x):
    return 0.5 * x * (1.0 + lax.erf(x * (1.0 / math.sqrt(2.0))))


def _gelu_grad(x):
    return 0.5 * (1.0 + lax.erf(x * (1.0 / math.sqrt(2.0)))) + x * jnp.exp(-0.5 * x * x) * (1.0 / math.sqrt(2.0 * math.pi))


def _ffn_conv(prev, cur, w, first):
    xx = _with_prev(prev, cur, first)
    return w[2:3] * cur + w[1:2] * _down(xx, 1)[HALO:] + w[0:1] * _down(xx, 2)[HALO:]


def _ffn_mid_fwd(gu, cw, cb, *, name):
    s, f2 = gu.shape
    f = f2 // 2
    ts, tc = _tile(s, 512, HALO), _tile(f, 512)
    nj = f // tc

    def body(g_ref, gp_ref, u_ref, w_ref, b_ref, o_ref):
        first = pl.program_id(0) == 0
        gc = _ffn_conv(gp_ref[...], g_ref[...], w_ref[...], first) + b_ref[...]
        o_ref[...] = (_gelu(gc) * u_ref[...]).astype(BF16)

    return pl.pallas_call(
        body, name=name, grid=(s // ts, nj),
        in_specs=[pl.BlockSpec((ts, tc), lambda i, j: (i, j)), _prev_spec(ts, tc, lambda j: j),
                  pl.BlockSpec((ts, tc), lambda i, j: (i, j + nj)),
                  pl.BlockSpec((3, tc), lambda i, j: (0, j)), pl.BlockSpec((1, tc), lambda i, j: (0, j))],
        out_specs=pl.BlockSpec((ts, tc), lambda i, j: (i, j)),
        out_shape=jax.ShapeDtypeStruct((s, f), BF16), compiler_params=_params("parallel", "parallel"))(gu, gu, gu, cw, cb)


def _ffn_mid_bwd(gu, cw, cb, dact, *, name):
    s, f2 = gu.shape
    f = f2 // 2
    ts, tc = _tile(s, 512, HALO), _tile(f, 512)
    nj, ni = f // tc, s // ts

    def body(g_ref, gp_ref, gn_ref, u_ref, un_ref, d_ref, dn_ref, w_ref, b_ref, dg_ref, du_ref, dw_ref, db_ref):
        i = pl.program_id(0)
        first, last = i == 0, i == ni - 1
        w, b = w_ref[...], b_ref[...]
        g = g_ref[...]
        gx = jnp.concatenate([jnp.where(first, 0.0, gp_ref[...]), g, jnp.where(last, 0.0, gn_ref[...])], axis=0)
        g1, g2 = _down(gx, 1), _down(gx, 2)
        gc = (w[2:3] * gx + w[1:2] * g1 + w[0:1] * g2)[HALO:] + b
        ux = _with_next(u_ref[...], un_ref[...], last)
        dx = _with_next(d_ref[...], dn_ref[...], last)
        dgc = dx * ux * _gelu_grad(gc)
        du_ref[...] = (dx[:ts] * _gelu(gc[:ts])).astype(BF16)
        dg = w[2:3] * dgc + w[1:2] * _up(dgc, 1) + w[0:1] * _up(dgc, 2)
        dg_ref[...] = dg[:ts].astype(BF16)
        dgt = dgc[:ts]
        db_ref[...] = _fold8(dgt)
        dw_ref[:, 0:tc] = _fold8(dgt * g2[HALO:HALO + ts])
        dw_ref[:, tc:2 * tc] = _fold8(dgt * g1[HALO:HALO + ts])
        dw_ref[:, 2 * tc:3 * tc] = _fold8(dgt * g)

    cur = lambda off: pl.BlockSpec((ts, tc), lambda i, j: (i, j + off))
    return pl.pallas_call(
        body, name=name, grid=(ni, nj),
        in_specs=[cur(0), _prev_spec(ts, tc, lambda j: j), _next_spec(ts, tc, lambda j: j, ni),
                  cur(nj), _next_spec(ts, tc, lambda j: j + nj, ni),
                  cur(0), _next_spec(ts, tc, lambda j: j, ni),
                  pl.BlockSpec((3, tc), lambda i, j: (0, j)), pl.BlockSpec((1, tc), lambda i, j: (0, j))],
        out_specs=[cur(0), cur(0),
                   pl.BlockSpec((8, 3 * tc), lambda i, j: (i, j)), pl.BlockSpec((8, tc), lambda i, j: (i, j))],
        out_shape=[jax.ShapeDtypeStruct((s, f), BF16), jax.ShapeDtypeStruct((s, f), BF16),
                   jax.ShapeDtypeStruct((ni * 8, 3 * f), F32), jax.ShapeDtypeStruct((ni * 8, f), F32)],
        compiler_params=_params("parallel", "parallel"))(gu, gu, gu, gu, gu, dact, dact, cw, cb)


def _qkv_fwd(pq, cw, n_heads, *, name):
    s, c3 = pq.shape
    ts = _tile(s, 512, HALO)

    def body(x_ref, xp_ref, w_ref, o_ref):
        j = pl.program_id(1)
        w = w_ref[...]
        x = x_ref[...]
        xx = _with_prev(xp_ref[...], x, pl.program_id(0) == 0)
        y = w[3:4] * x + w[2:3] * _down(xx, 1)[HALO:] + w[1:2] * _down(xx, 2)[HALO:] + w[0:1] * _down(xx, 3)[HALO:]
        c = y * _sigmoid(y)
        r = lax.rsqrt(jnp.sum(c * c, axis=-1, keepdims=True) + EPS)
        scale = jnp.where(j < n_heads, HEAD_DIM ** -0.5, 1.0)
        o_ref[...] = jnp.where(j < 2 * n_heads, c * (r * scale), c)

    return pl.pallas_call(
        body, name=name, grid=(s // ts, 3 * n_heads),
        in_specs=[pl.BlockSpec((ts, HEAD_DIM), lambda i, j: (i, j)), _prev_spec(ts, HEAD_DIM, lambda j: j),
                  pl.BlockSpec((4, HEAD_DIM), lambda i, j: (0, j))],
        out_specs=pl.BlockSpec((None, ts, HEAD_DIM), lambda i, j: (j // n_heads, i, j % n_heads)),
        out_shape=jax.ShapeDtypeStruct((3, s, c3 // 3), F32),
        compiler_params=_params("parallel", "parallel"))(pq, pq, cw)


def _qkv_bwd(pq, cw, dqkv3, n_heads, *, name):
    s, c3 = pq.shape
    ts = _tile(s, 512, HALO)
    ni = s // ts
    hd = HEAD_DIM

    def body(x_ref, xp_ref, xn_ref, w_ref, d_ref, dn_ref, dx_ref, dw_ref):
        i, j = pl.program_id(0), pl.program_id(1)
        first, last = i == 0, i == ni - 1
        w = w_ref[...]
        x = x_ref[...]
        xx = jnp.concatenate([jnp.where(first, 0.0, xp_ref[...]), x, jnp.where(last, 0.0, xn_ref[...])], axis=0)
        x1, x2, x3 = _down(xx, 1), _down(xx, 2), _down(xx, 3)
        y = (w[3:4] * xx + w[2:3] * x1 + w[1:2] * x2 + w[0:1] * x3)[HALO:]
        sg = _sigmoid(y)
        c = y * sg
        dn = _with_next(d_ref[...], dn_ref[...], last)
        r = lax.rsqrt(jnp.sum(c * c, axis=-1, keepdims=True) + EPS)
        nrm = c * r
        dnn = dn * jnp.where(j < n_heads, hd ** -0.5, 1.0)
        dc = jnp.where(j < 2 * n_heads, r * (dnn - nrm * jnp.sum(dnn * nrm, axis=-1, keepdims=True)), dn)
        dy = dc * (sg * (1.0 + y * (1.0 - sg)))
        dx = w[3:4] * dy + w[2:3] * _up(dy, 1) + w[1:2] * _up(dy, 2) + w[0:1] * _up(dy, 3)
        dx_ref[...] = dx[:ts].astype(BF16)
        dyt = dy[:ts]
        dw_ref[:, 0:hd] = _fold8(dyt * x3[HALO:HALO + ts])
        dw_ref[:, hd:2 * hd] = _fold8(dyt * x2[HALO:HALO + ts])
        dw_ref[:, 2 * hd:3 * hd] = _fold8(dyt * x1[HALO:HALO + ts])
        dw_ref[:, 3 * hd:4 * hd] = _fold8(dyt * x)

    dspec = lambda rows, row_index: pl.BlockSpec(
        (None, rows, hd), lambda i, j: (j // n_heads, row_index(i), j % n_heads))
    return pl.pallas_call(
        body, name=name, grid=(ni, 3 * n_heads),
        in_specs=[pl.BlockSpec((ts, hd), lambda i, j: (i, j)), _prev_spec(ts, hd, lambda j: j),
                  _next_spec(ts, hd, lambda j: j, ni), pl.BlockSpec((4, hd), lambda i, j: (0, j)),
                  dspec(ts, lambda i: i),
                  dspec(HALO, lambda i: jnp.minimum((i + 1) * (ts // HALO), ni * (ts // HALO) - 1))],
        out_specs=[pl.BlockSpec((ts, hd), lambda i, j: (i, j)), pl.BlockSpec((8, 4 * hd), lambda i, j: (i, j))],
        out_shape=[jax.ShapeDtypeStruct((s, c3), BF16), jax.ShapeDtypeStruct((ni * 8, 4 * c3), F32)],
        compiler_params=_params("parallel", "parallel"))(pq, pq, pq, cw, dqkv3, dqkv3)


def _gate_terms(ba, al, dt, h, n_heads):
    lane = lax.broadcasted_iota(jnp.int32, ba.shape, 1)
    braw = jnp.sum(jnp.where(lane == h, ba, 0.0), axis=1, keepdims=True)
    araw = jnp.sum(jnp.where(lane == h + n_heads, ba, 0.0), axis=1, keepdims=True)
    beta = _sigmoid(braw)
    z = araw + dt
    sp = jnp.maximum(z, 0.0) + jnp.log(1.0 + jnp.exp(-jnp.abs(z)))
    ea = jnp.exp(jnp.zeros((1, 1), F32) + al)
    return beta, z, sp, ea


def _gates_fwd(ba, a_log, dt_bias, n_heads, *, name):
    s = ba.shape[0]
    ts = _tile(s, 512, CHUNK)

    def body(ba_ref, al_ref, dt_ref, o_ref):
        h = pl.program_id(1)
        beta, _, sp, ea = _gate_terms(ba_ref[...], al_ref[h], dt_ref[h], h, n_heads)
        gx = jnp.broadcast_to(-ea * sp, (ts, HEAD_DIM))
        rc = lax.broadcasted_iota(jnp.int32, (ts, HEAD_DIM), 0) & (CHUNK - 1)
        for sh in (1, 2, 4, 8, 16, 32):
            gx = gx + jnp.where(rc >= sh, _down(gx, sh), 0.0)
        o_ref[0] = jnp.broadcast_to(beta, (ts, HEAD_DIM))
        o_ref[1] = gx

    smem = pl.BlockSpec(memory_space=pltpu.SMEM)
    return pl.pallas_call(
        body, name=name, grid=(s // ts, n_heads),
        in_specs=[pl.BlockSpec((ts, 128), lambda i, h: (i, 0)), smem, smem],
        out_specs=pl.BlockSpec((2, ts, HEAD_DIM), lambda i, h: (0, i, h)),
        out_shape=jax.ShapeDtypeStruct((2, s, n_heads * HEAD_DIM), F32),
        compiler_params=_params("parallel", "parallel"))(ba, a_log, dt_bias)


def _gates_bwd(ba, a_log, dt_bias, dbg, n_heads, *, name):
    s = ba.shape[0]
    ts = _tile(s, 512, CHUNK)
    ni = s // ts

    def body(ba_ref, al_ref, dt_ref, d_ref, o_ref, p_ref):
        h = pl.program_id(1)
        beta, z, sp, ea = _gate_terms(ba_ref[...], al_ref[h], dt_ref[h], h, n_heads)
        dg = d_ref[1]
        rc = lax.broadcasted_iota(jnp.int32, (ts, HEAD_DIM), 0) & (CHUNK - 1)
        for sh in (1, 2, 4, 8, 16, 32):
            dg = dg + jnp.where(rc < CHUNK - sh, _up(dg, sh), 0.0)
        daraw = dg * (-ea * _sigmoid(z))
        dbraw = d_ref[0] * (beta * (1.0 - beta))

        @pl.when(h == 0)
        def _():
            o_ref[...] = jnp.zeros_like(o_ref)
            p_ref[...] = jnp.zeros_like(p_ref)

        lane = lax.broadcasted_iota(jnp.int32, (1, 128), 1)
        is_b, is_a = lane == h, lane == h + n_heads
        o_ref[...] += jnp.where(is_b, dbraw, 0.0) + jnp.where(is_a, daraw, 0.0)
        p_ref[...] += jnp.where(is_b, _fold8(dg * (-ea * sp)), 0.0) + jnp.where(is_a, _fold8(daraw), 0.0)

    smem = pl.BlockSpec(memory_space=pltpu.SMEM)
    return pl.pallas_call(
        body, name=name, grid=(ni, n_heads),
        in_specs=[pl.BlockSpec((ts, 128), lambda i, h: (i, 0)), smem, smem,
                  pl.BlockSpec((2, ts, HEAD_DIM), lambda i, h: (0, i, h))],
        out_specs=[pl.BlockSpec((ts, 128), lambda i, h: (i, 0)), pl.BlockSpec((8, 128), lambda i, h: (i, 0))],
        out_shape=[jax.ShapeDtypeStruct((s, 128), F32), jax.ShapeDtypeStruct((ni * 8, 128), F32)],
        compiler_params=_params("parallel", "arbitrary"))(ba, a_log, dt_bias, dbg)


BLK = 2 * CHUNK
HEADS_PER_STEP = 2
NN = (((1,), (0,)), ((), ()))
NT = (((1,), (1,)), ((), ()))
TN = (((0,), (0,)), ((), ()))


def _dot(a, b, dims):
    return lax.dot_general(a.astype(BF16), b.astype(BF16), dims, preferred_element_type=F32)


def _dot3(a, b, dims):
    ah, bh = a.astype(BF16), b.astype(BF16)
    al, bl = (a - ah.astype(F32)).astype(BF16), (b - bh.astype(F32)).astype(BF16)
    d = lambda u, v: lax.dot_general(u, v, dims, preferred_element_type=F32)
    return d(ah, bh) + (d(ah, bl) + d(al, bh))


def _pair_masks():
    row = lax.broadcasted_iota(jnp.int32, (BLK, BLK), 0)
    col = lax.broadcasted_iota(jnp.int32, (BLK, BLK), 1)
    same = (row < CHUNK) == (col < CHUNK)
    return same & (row >= col), same & (row > col), row == col


def _pair_terms(q, k, v, b, gam, masks):
    tril, strict, eye = masks
    g_cols = jnp.sum(jnp.where(eye, gam, 0.0), axis=0, keepdims=True)
    dmat = jnp.exp(jnp.where(tril, gam - g_cols, -jnp.inf))
    eg = jnp.exp(gam)
    rowi = lax.broadcasted_iota(jnp.int32, (BLK, HEAD_DIM), 0)
    elast = jnp.exp(jnp.where(rowi < CHUNK, gam[CHUNK - 1:CHUNK], gam[BLK - 1:BLK]) - gam)
    kb, vb = k * b, v * b
    kq = _dot(jnp.concatenate([kb, q], axis=0), k, NT)
    lmat = jnp.where(strict, kq[:BLK] * dmat, 0.0)
    attn = kq[BLK:] * dmat
    rhs = jnp.concatenate([vb, kb * eg], axis=1)
    return dmat, eg, elast, kb, lmat, attn, rhs


def _unit_lower_inverse(lmat, eye):
    p = -lmat
    t = jnp.where(eye, 1.0, 0.0) + p
    p = _dot3(p, p, NN)
    for _ in range(4):
        r = _dot3(jnp.concatenate([p, t], axis=0), p, NN)
        p, t = r[:BLK], t + r[BLK:]
    return t + _dot3(t, p, NN)


def _gdn_fwd(qkv3, bg, *, name):
    _, s, dl = qkv3.shape
    n_heads = dl // HEAD_DIM
    sb = _tile(s, 1024, BLK)
    npair = sb // BLK
    c = CHUNK

    hpg = HEADS_PER_STEP if n_heads % HEADS_PER_STEP == 0 else 1
    wd = hpg * HEAD_DIM

    def body(qkv_ref, bg_ref, o_ref, st_ref, ti_ref, s_scr):
        @pl.when(pl.program_id(1) == 0)
        def _():
            s_scr[...] = jnp.zeros_like(s_scr)

        masks = _pair_masks()

        def head_pair(hh, p, rows, state):
            ls = slice(hh * HEAD_DIM, (hh + 1) * HEAD_DIM)
            q, k, v = qkv_ref[0, rows, ls], qkv_ref[1, rows, ls], qkv_ref[2, rows, ls]
            b, gam = bg_ref[0, rows, ls], bg_ref[1, rows, ls]
            _, eg, elast, _, lmat, attn, rhs = _pair_terms(q, k, v, b, gam, masks)
            tinv = _unit_lower_inverse(lmat, masks[2])
            ti_ref[hh, rows, :] = tinv
            sol = _dot3(tinv, rhs, NN)
            u, w = sol[:, :HEAD_DIM], sol[:, HEAD_DIM:]
            qd, ke = q * eg, k * elast
            st_ref[hh, 2 * p] = state
            wq = _dot(jnp.concatenate([w[:c], qd[:c]], axis=0), state, NN)
            vn_a, o_a = u[:c] - wq[:c], wq[c:]
            state = state * jnp.exp(gam[c - 1:c]) + _dot(ke[:c], vn_a, TN)
            st_ref[hh, 2 * p + 1] = state
            wq = _dot(jnp.concatenate([w[c:], qd[c:]], axis=0), state, NN)
            vn_b, o_b = u[c:] - wq[:c], wq[c:]
            state = state * jnp.exp(gam[BLK - 1:BLK]) + _dot(ke[c:], vn_b, TN)
            o_ref[rows, ls] = jnp.concatenate([o_a, o_b], axis=0) + _dot(attn, jnp.concatenate([vn_a, vn_b], axis=0), NN)
            return state

        def pair(p, states):
            rows = pl.ds(pl.multiple_of(p * BLK, BLK), BLK)
            return tuple(head_pair(hh, p, rows, states[hh]) for hh in range(hpg))

        states = lax.fori_loop(0, npair, pair, tuple(s_scr[hh] for hh in range(hpg)))
        for hh in range(hpg):
            s_scr[hh] = states[hh]

    return pl.pallas_call(
        body, name=name, grid=(n_heads // hpg, s // sb),
        in_specs=[pl.BlockSpec((3, sb, wd), lambda h, j: (0, j, h)),
                  pl.BlockSpec((2, sb, wd), lambda h, j: (0, j, h))],
        out_specs=[pl.BlockSpec((sb, wd), lambda h, j: (j, h)),
                   pl.BlockSpec((hpg, 2 * npair, HEAD_DIM, HEAD_DIM), lambda h, j: (h, j, 0, 0)),
                   pl.BlockSpec((hpg, sb, BLK), lambda h, j: (h, j, 0))],
        out_shape=[jax.ShapeDtypeStruct((s, dl), F32),
                   jax.ShapeDtypeStruct((n_heads, s // c, HEAD_DIM, HEAD_DIM), F32),
                   jax.ShapeDtypeStruct((n_heads, s, BLK), F32)],
        scratch_shapes=[pltpu.VMEM((hpg, HEAD_DIM, HEAD_DIM), F32)],
        compiler_params=_params("parallel", "arbitrary"))(qkv3, bg)


def _gdn_bwd(qkv3, bg, st, ti, d_o, *, name):
    _, s, dl = qkv3.shape
    n_heads = dl // HEAD_DIM
    sb = _tile(s, 1024, BLK)
    npair, nsb = sb // BLK, s // sb
    c = CHUNK
    hpg = HEADS_PER_STEP if n_heads % HEADS_PER_STEP == 0 else 1
    wd = hpg * HEAD_DIM

    def body(qkv_ref, bg_ref, st_ref, ti_ref, do_ref, dqkv_ref, dbg_ref, ds_scr):
        @pl.when(pl.program_id(1) == 0)
        def _():
            ds_scr[...] = jnp.zeros_like(ds_scr)

        masks = _pair_masks()
        tril, strict, eye = masks
        rowc = lax.broadcasted_iota(jnp.int32, (BLK, 1), 0)

        def total(x):
            return jnp.sum(jnp.sum(x, axis=1, keepdims=True), axis=0, keepdims=True)

        def head_pair(hh, p, rows, ds2):
            ls = slice(hh * HEAD_DIM, (hh + 1) * HEAD_DIM)
            q, k, v = qkv_ref[0, rows, ls], qkv_ref[1, rows, ls], qkv_ref[2, rows, ls]
            b, gam = bg_ref[0, rows, ls], bg_ref[1, rows, ls]
            tinv, dout = ti_ref[hh, rows, :], do_ref[rows, ls]
            s0, s1 = st_ref[hh, 2 * p], st_ref[hh, 2 * p + 1]
            dmat, eg, elast, kb, lmat, attn, rhs = _pair_terms(q, k, v, b, gam, masks)
            sol = _dot3(tinv, rhs, NN)
            u, w = sol[:, :HEAD_DIM], sol[:, HEAD_DIM:]
            qd, ke = q * eg, k * elast
            dec_a, dec_b = jnp.exp(gam[c - 1:c]), jnp.exp(gam[BLK - 1:BLK])
            vn = u - jnp.concatenate([_dot(w[:c], s0, NN), _dot(w[c:], s1, NN)], axis=0)
            dvn_o = _dot(attn, dout, TN)
            dvn_b = dvn_o[c:] + _dot(ke[c:], ds2, NN)
            ds1 = _dot(qd[c:], dout[c:], TN) + ds2 * dec_b - _dot(w[c:], dvn_b, TN)
            dvn_a = dvn_o[:c] + _dot(ke[:c], ds1, NN)
            ds0 = _dot(qd[:c], dout[:c], TN) + ds1 * dec_a - _dot(w[:c], dvn_a, TN)
            dvn = jnp.concatenate([dvn_a, dvn_b], axis=0)
            dke = jnp.concatenate([_dot(vn[:c], ds1, NT), _dot(vn[c:], ds2, NT)], axis=0)
            dw = -jnp.concatenate([_dot(dvn_a, s0, NT), _dot(dvn_b, s1, NT)], axis=0)
            dqd = jnp.concatenate([_dot(dout[:c], s0, NT), _dot(dout[c:], s1, NT)], axis=0)
            dattn = jnp.where(tril, _dot(dout, vn, NT), 0.0)
            drhs = _dot3(tinv, jnp.concatenate([dvn, dw], axis=1), TN)
            dl_ = jnp.where(strict, -_dot3(drhs, sol, NT), 0.0)
            dm, dqk = dl_ * dmat, dattn * dmat
            dvb, drw = drhs[:, :HEAD_DIM], drhs[:, HEAD_DIM:]
            dkb = _dot(dm, k, NN) + drw * eg
            dq = _dot(dqk, k, NN) + dqd * eg
            dk = _dot(dm, kb, TN) + _dot(dqk, q, TN) + dke * elast + dkb * b
            dbeta = jnp.sum(dvb * v + dkb * k, axis=1, keepdims=True)
            e = dl_ * lmat + dattn * attn
            e_cols = jnp.sum(jnp.where(eye, jnp.sum(e, axis=0, keepdims=True), 0.0), axis=1, keepdims=True)
            dke_ke = dke * ke
            dgam = (jnp.sum(e, axis=1, keepdims=True) - e_cols
                    + jnp.sum(drw * (kb * eg) + dqd * qd - dke_ke, axis=1, keepdims=True))
            tot_a = total(dke_ke[:c]) + total(s0 * ds1) * dec_a[:, :1]
            tot_b = total(dke_ke[c:]) + total(s1 * ds2) * dec_b[:, :1]
            dgam = dgam + jnp.where(rowc == c - 1, tot_a, 0.0) + jnp.where(rowc == BLK - 1, tot_b, 0.0)
            dqkv_ref[0, rows, ls] = dq
            dqkv_ref[1, rows, ls] = dk
            dqkv_ref[2, rows, ls] = dvb * b
            dbg_ref[0, rows, ls] = jnp.broadcast_to(dbeta, (BLK, HEAD_DIM))
            dbg_ref[1, rows, ls] = jnp.broadcast_to(dgam, (BLK, HEAD_DIM))
            return ds0

        def pair(pp, dstates):
            p = npair - 1 - pp
            rows = pl.ds(pl.multiple_of(p * BLK, BLK), BLK)
            return tuple(head_pair(hh, p, rows, dstates[hh]) for hh in range(hpg))

        dstates = lax.fori_loop(0, npair, pair, tuple(ds_scr[hh] for hh in range(hpg)))
        for hh in range(hpg):
            ds_scr[hh] = dstates[hh]

    rev = lambda j: nsb - 1 - j
    return pl.pallas_call(
        body, name=name, grid=(n_heads // hpg, nsb),
        in_specs=[pl.BlockSpec((3, sb, wd), lambda h, j: (0, rev(j), h)),
                  pl.BlockSpec((2, sb, wd), lambda h, j: (0, rev(j), h)),
                  pl.BlockSpec((hpg, 2 * npair, HEAD_DIM, HEAD_DIM), lambda h, j: (h, rev(j), 0, 0)),
                  pl.BlockSpec((hpg, sb, BLK), lambda h, j: (h, rev(j), 0)),
                  pl.BlockSpec((sb, wd), lambda h, j: (rev(j), h))],
        out_specs=[pl.BlockSpec((3, sb, wd), lambda h, j: (0, rev(j), h)),
                   pl.BlockSpec((2, sb, wd), lambda h, j: (0, rev(j), h))],
        out_shape=[jax.ShapeDtypeStruct((3, s, dl), F32), jax.ShapeDtypeStruct((2, s, dl), F32)],
        scratch_shapes=[pltpu.VMEM((hpg, HEAD_DIM, HEAD_DIM), F32)],
        compiler_params=_params("parallel", "arbitrary"))(qkv3, bg, st, ti, d_o)


N_GROUPS = len(POOL_WINDOWS)


def _pick(g, vals):
    out = vals[-1]
    for i in range(len(vals) - 2, -1, -1):
        out = jnp.where(g == i, vals[i], out)
    return out


def _head_norm(o, nw):
    hats, outs = [], []
    for h in range(o.shape[1] // HEAD_DIM):
        sl = slice(h * HEAD_DIM, (h + 1) * HEAD_DIM)
        oh = o[:, sl]
        r = lax.rsqrt(jnp.mean(oh * oh, axis=-1, keepdims=True) + EPS)
        hats.append((oh * r, r))
        outs.append(oh * r * nw[:, sl])
    return hats, jnp.concatenate(outs, axis=1) if len(outs) > 1 else outs[0]


def _pool_counts(g, t0, n):
    t = (lax.broadcasted_iota(jnp.int32, (n, 1), 0) + t0 + 1).astype(F32)
    return jnp.minimum(t, _pick(g, [float(w) for w in POOL_WINDOWS]))


def _pool(prev, cur, first, g, t0):
    s = _with_prev(prev, cur, first)
    sums = []
    for sh in (1, 2, 4, 8):
        s = s + _down(s, sh)
        sums.append(s)
    return _pick(g, sums)[HALO:] / _pool_counts(g, t0, cur.shape[0]) - cur


def _mix_specs(ts, gw, ni):
    seg = lambda k: pl.BlockSpec((ts, gw), lambda g, i: (i, k * N_GROUPS + g))
    per = ts // HALO
    prev = lambda k: pl.BlockSpec((HALO, gw), lambda g, i: (jnp.maximum(i * per - 1, 0), k * N_GROUPS + g))
    nxt = lambda k: pl.BlockSpec((HALO, gw), lambda g, i: (jnp.minimum((i + 1) * per, ni * per - 1), k * N_GROUPS + g))
    vec = pl.BlockSpec((1, gw), lambda g, i: (0, g))
    pw = pl.BlockSpec((None, gw, gw), lambda g, i: (g, 0, 0))
    return seg, prev, nxt, vec, pw


def _mix_fwd(o, zpg, nw, pw, ps, *, name):
    s, d = o.shape
    gw = d // N_GROUPS
    ts = _tile(s, 512, HALO)
    ni = s // ts
    seg, prev, _, vec, pwspec = _mix_specs(ts, gw, ni)

    def body(o_ref, z_ref, p_ref, pp_ref, ga_ref, gb_ref, nw_ref, pw_ref, ps_ref, out_ref):
        g, i = pl.program_id(0), pl.program_id(1)
        _, on = _head_norm(o_ref[...], nw_ref[...])
        z = z_ref[...]
        ya = on * (z * _sigmoid(z))
        pooled = _pool(pp_ref[...], p_ref[...], i == 0, g, i * ts)
        yb = _dot(pooled, pw_ref[...], NN) * ps_ref[...]
        out_ref[...] = (_sigmoid(ga_ref[...]) * ya + _sigmoid(gb_ref[...]) * yb).astype(BF16)

    return pl.pallas_call(
        body, name=name, grid=(N_GROUPS, ni),
        in_specs=[seg(0), seg(0), seg(1), prev(1), seg(2), seg(3), vec, pwspec, vec],
        out_specs=seg(0), out_shape=jax.ShapeDtypeStruct((s, d), BF16),
        compiler_params=_params("parallel", "parallel"))(o, zpg, zpg, zpg, zpg, zpg, nw, pw, ps)


def _mix_bwd(o, zpg, nw, pw, ps, dmix, *, name):
    s, d = o.shape
    gw = d // N_GROUPS
    ts = _tile(s, 512, HALO)
    ni = s // ts
    seg, prev, nxt, vec, pwspec = _mix_specs(ts, gw, ni)

    def body(o_ref, z_ref, p_ref, pp_ref, ga_ref, gb_ref, gbn_ref, nw_ref, pw_ref, ps_ref, dm_ref, dmn_ref,
             do_ref, d4_ref, dpw_ref, dnw_ref, dps_ref):
        g, i = pl.program_id(0), pl.program_id(1)
        last = i == ni - 1
        nw, ps, pwv = nw_ref[...], ps_ref[...], pw_ref[...]
        ov, z = o_ref[...], z_ref[...]
        hats, on = _head_norm(ov, nw)
        sz = _sigmoid(z)
        silu = z * sz
        ya = on * silu
        pooled = _pool(pp_ref[...], p_ref[...], i == 0, g, i * ts)
        yp = _dot(pooled, pwv, NN)
        sga, sgb = _sigmoid(ga_ref[...]), _sigmoid(gb_ref[...])
        dm = dm_ref[...]
        dya, dyb = dm * sga, dm * sgb
        d4_ref[2] = (dm * ya * (sga * (1.0 - sga))).astype(BF16)
        d4_ref[3] = (dm * (yp * ps) * (sgb * (1.0 - sgb))).astype(BF16)
        dps_ref[...] = _fold8(dyb * yp)
        dyp = dyb * ps

        @pl.when(i == 0)
        def _():
            dpw_ref[...] = jnp.zeros_like(dpw_ref)

        dpw_ref[...] += _dot(pooled, dyp, TN)
        dyp_next = jnp.where(last, 0.0, dmn_ref[...] * _sigmoid(gbn_ref[...]) * ps)
        dpool = _dot(jnp.concatenate([dyp, dyp_next], axis=0), pwv, NT)
        a = dpool / _pool_counts(g, i * ts, ts + HALO)
        sums = []
        for sh in (1, 2, 4, 8):
            a = a + _up(a, sh)
            sums.append(a)
        d4_ref[1] = (_pick(g, sums)[:ts] - dpool[:ts]).astype(BF16)
        d4_ref[0] = (dya * on * (sz * (1.0 + z * (1.0 - sz)))).astype(BF16)
        don = dya * silu
        dos, dnws = [], []
        for h, (ohat, r) in enumerate(hats):
            sl = slice(h * HEAD_DIM, (h + 1) * HEAD_DIM)
            dxh = don[:, sl] * nw[:, sl]
            dos.append(r * (dxh - ohat * jnp.mean(dxh * ohat, axis=-1, keepdims=True)))
            dnws.append(_fold8(don[:, sl] * ohat))
        do_ref[...] = jnp.concatenate(dos, axis=1) if len(dos) > 1 else dos[0]
        dnw_ref[...] = jnp.concatenate(dnws, axis=1) if len(dnws) > 1 else dnws[0]

    part = pl.BlockSpec((8, gw), lambda g, i: (i, g))
    return pl.pallas_call(
        body, name=name, grid=(N_GROUPS, ni),
        in_specs=[seg(0), seg(0), seg(1), prev(1), seg(2), seg(3), nxt(3), vec, pwspec, vec, seg(0), nxt(0)],
        out_specs=[seg(0), pl.BlockSpec((4, ts, gw), lambda g, i: (0, i, g)),
                   pl.BlockSpec((None, gw, gw), lambda g, i: (g, 0, 0)), part, part],
        out_shape=[jax.ShapeDtypeStruct((s, d), F32), jax.ShapeDtypeStruct((4, s, d), BF16),
                   jax.ShapeDtypeStruct((N_GROUPS, gw, gw), F32),
                   jax.ShapeDtypeStruct((ni * 8, d), F32), jax.ShapeDtypeStruct((ni * 8, d), F32)],
        compiler_params=_params("parallel", "arbitrary"))(o, zpg, zpg, zpg, zpg, zpg, zpg, nw, pw, ps, dmix, dmix)


def _adamw(w, g, m, v, *, name):
    r, c = w.shape
    tr = _tile(r, max(8, (1 << 19) // c // 8 * 8), 8)

    def body(w_ref, g_ref, m_ref, v_ref, d_ref, mo_ref, vo_ref):
        gv = g_ref[...]
        mn = ADAM_B1 * m_ref[...] + (1.0 - ADAM_B1) * gv
        vn = ADAM_B2 * v_ref[...] + (1.0 - ADAM_B2) * (gv * gv)
        m_hat = mn / (1.0 - ADAM_B1 ** ADAM_STEP)
        v_hat = vn / (1.0 - ADAM_B2 ** ADAM_STEP)
        d_ref[...] = -ADAM_LR * (m_hat / (jnp.sqrt(v_hat) + ADAM_EPS) + ADAM_WD * w_ref[...])
        mo_ref[...] = mn
        vo_ref[...] = vn

    blk = pl.BlockSpec((tr, c), lambda i: (i, 0))
    return pl.pallas_call(
        body, name=name, grid=(r // tr,), in_specs=[blk] * 4, out_specs=[blk] * 3,
        out_shape=[jax.ShapeDtypeStruct((r, c), F32)] * 3, compiler_params=_params("parallel"))(w, g, m, v)


def _sum_parts(x, *, slot=None, name):
    p, r, c = x.shape
    tc = 128 if c % 128 == 0 else c
    tr = _tile(r, max(16, SUM_BLOCK_BYTES // (p * tc * x.dtype.itemsize)), 16)

    def body(*refs):
        x_ref, o_ref = refs[-2:]
        acc = x_ref[0].astype(F32)
        for i in range(1, p):
            acc = acc + x_ref[i].astype(F32)
        o_ref[...] = acc

    if slot is None:
        return pl.pallas_call(
            body, name=name, grid=(r // tr, c // tc),
            in_specs=[pl.BlockSpec((p, tr, tc), lambda i, j: (0, i, j))],
            out_specs=pl.BlockSpec((tr, tc), lambda i, j: (i, j)),
            out_shape=jax.ShapeDtypeStruct((r, c), F32), compiler_params=_params("parallel", "parallel"))(x)
    return pl.pallas_call(
        body, name=name,
        grid_spec=pltpu.PrefetchScalarGridSpec(
            num_scalar_prefetch=1, grid=(r // tr, c // tc),
            in_specs=[pl.BlockSpec((p, tr, tc), lambda i, j, s: (0, i, j))],
            out_specs=pl.BlockSpec((None, tr, tc), lambda i, j, s: (s[0], i, j))),
        out_shape=jax.ShapeDtypeStruct((2, r, c), F32), compiler_params=_params("parallel", "parallel"))(slot, x)


_HBM = pl.BlockSpec(memory_space=pltpu.HBM)


def _place():
    return lax.axis_index("x"), lax.axis_index("y"), lax.axis_index("c")


def _allgather(x_shard, *, name):
    m_per, n = x_shard.shape

    def body(x_ref, out_ref, send_sems, recv_sems, local_sem):
        x, y, c = _place()
        me, sibling = (x, y, c), (x, y, 1 - c)
        chips = [(1 - x, y), (x, 1 - y), (1 - x, 1 - y)]

        def rows(px, py, pc):
            return out_ref.at[pl.ds((4 * px + 2 * py + pc) * m_per, m_per), :]

        def copy(k, block, to, src=None):
            return pltpu.make_async_remote_copy(
                src_ref=rows(*block) if src is None else src, dst_ref=rows(*block),
                send_sem=send_sems.at[k], recv_sem=recv_sems.at[k], device_id=to, device_id_type=MESH)

        mine = pltpu.make_async_copy(x_ref, rows(*me), local_sem)
        mine.start()
        first = [copy(0, me, sibling, src=x_ref)]
        first += [copy(1 + j, me, (*chip, c), src=x_ref) for j, chip in enumerate(chips)]
        for cp in first:
            cp.start()
        passed = [copy(4 + j, (*chip, c), sibling) for j, chip in enumerate(chips)]
        for j, chip in enumerate(chips):
            copy(1 + j, (*chip, c), me).wait_recv()
            passed[j].start()
        copy(0, sibling, me).wait_recv()
        for j, chip in enumerate(chips):
            copy(4 + j, (*chip, 1 - c), me).wait_recv()
        for cp in first + passed:
            cp.wait_send()
        mine.wait()

    return pl.pallas_call(
        body, name=name, out_shape=jax.ShapeDtypeStruct((8 * m_per, n), x_shard.dtype),
        in_specs=[_HBM], out_specs=_HBM,
        scratch_shapes=[pltpu.SemaphoreType.DMA((7,)), pltpu.SemaphoreType.DMA((7,)), pltpu.SemaphoreType.DMA])(x_shard)


def _all_to_all(parts, *, name):
    def body(g_ref, out_ref, send_sems, recv_sems, local_sem):
        x, y, c = _place()
        me = 4 * x + 2 * y + c
        mine = pltpu.make_async_copy(g_ref.at[me], out_ref.at[me], local_sem)
        mine.start()
        sends, peers = [], []
        for k in range(1, 8):
            px = 1 - x if k & 4 else x
            py = 1 - y if k & 2 else y
            pc = 1 - c if k & 1 else c
            peer = 4 * px + 2 * py + pc
            cp = pltpu.make_async_remote_copy(
                src_ref=g_ref.at[peer], dst_ref=out_ref.at[me], send_sem=send_sems.at[k - 1],
                recv_sem=recv_sems.at[k - 1], device_id=(px, py, pc), device_id_type=MESH)
            cp.start()
            sends.append(cp)
            peers.append((peer, (px, py, pc)))
        for k, (peer, pid) in enumerate(peers):
            pltpu.make_async_remote_copy(
                src_ref=g_ref.at[peer], dst_ref=out_ref.at[peer], send_sem=send_sems.at[k],
                recv_sem=recv_sems.at[k], device_id=pid, device_id_type=MESH).wait_recv()
        for cp in sends:
            cp.wait_send()
        mine.wait()

    return pl.pallas_call(
        body, name=name, out_shape=jax.ShapeDtypeStruct(parts.shape, parts.dtype), in_specs=[_HBM], out_specs=_HBM,
        scratch_shapes=[pltpu.SemaphoreType.DMA((7,)), pltpu.SemaphoreType.DMA((7,)), pltpu.SemaphoreType.DMA])(parts)


def _share_halves(both, *, name):
    _, r, _ = both.shape
    n_split = 1
    while both.size // 2 * both.dtype.itemsize > n_split * MAX_COPY_BYTES and r % (2 * n_split * 16) == 0:
        n_split *= 2
    rs = r // n_split

    def body(in_ref, out_ref, send_sems, recv_sems):
        x, y, c = _place()

        def copy(k, slot):
            rows = pl.ds(k * rs, rs)
            return pltpu.make_async_remote_copy(
                src_ref=in_ref.at[slot, rows], dst_ref=out_ref.at[slot, rows], send_sem=send_sems.at[k],
                recv_sem=recv_sems.at[k], device_id=(x, y, 1 - c), device_id_type=MESH)

        sends = [copy(k, c) for k in range(n_split)]
        for cp in sends:
            cp.start()
        for k in range(n_split):
            copy(k, 1 - c).wait_recv()
        for cp in sends:
            cp.wait_send()

    return pl.pallas_call(
        body, name=name, out_shape=jax.ShapeDtypeStruct(both.shape, both.dtype), in_specs=[_HBM], out_specs=_HBM,
        input_output_aliases={0: 0},
        scratch_shapes=[pltpu.SemaphoreType.DMA((n_split,)), pltpu.SemaphoreType.DMA((n_split,))])(both)


def _piece_rows(shape):
    n = math.prod(shape)
    if n % 128 == 0:
        return n // 128, 128
    assert shape[-1] <= 128, shape
    return n // shape[-1], shape[-1]


def _pack_small(arrs, row_multiple):
    pieces = []
    for a in arrs:
        rows, lanes = _piece_rows(a.shape)
        t = a.astype(F32).reshape(rows, lanes)
        pieces.append(jnp.pad(t, ((0, -rows % 8), (0, 128 - lanes))))
    buf = jnp.concatenate(pieces, axis=0)
    return jnp.pad(buf, ((0, -buf.shape[0] % row_multiple), (0, 0)))


def _unpack_small(buf, shapes):
    out, off = [], 0
    for shp in shapes:
        rows, lanes = _piece_rows(shp)
        out.append(buf[off:off + rows, :lanes].reshape(shp))
        off += rows + (-rows % 8)
    return out


def _w_in_grad_parts(g_qkv, g_zpg, g_ba, n_heads, *, name):
    d = g_qkv.shape[0]
    cw = (g_qkv.shape[1] + 4 * d + 2 * n_heads) // N_CHIPS
    tr = _tile(d // 2, 128, 16)
    per_half = d // 2 // tr

    def body(a_ref, z_ref, p_ref, ga_ref, gb_ref, ba_ref, o_ref):
        full = jnp.concatenate([a_ref[...], z_ref[...], ba_ref[...][:, :2 * n_heads], p_ref[...], ga_ref[...],
                                gb_ref[...]], axis=1)
        for j in range(N_CHIPS):
            o_ref[j] = full[:, cw * j:cw * (j + 1)]

    row = lambda c: pl.BlockSpec((tr, c), lambda i: (i, 0))
    out = pl.pallas_call(
        body, name=name, grid=(d // tr,),
        in_specs=[row(g_qkv.shape[1]), row(d), row(d), row(d), row(d), row(128)],
        out_specs=pl.BlockSpec((N_CHIPS, None, tr, cw), lambda i: (0, i // per_half, i % per_half, 0)),
        out_shape=jax.ShapeDtypeStruct((N_CHIPS, 2, d // 2, cw), BF16),
        compiler_params=_params("parallel"))(g_qkv, *g_zpg, g_ba)
    return out.reshape(8, d // 2, cw)


def _layer_fwd(x, p, n_heads):
    h = _rmsnorm_fwd(x, p["norm_mix_w"], name="norm_mix_fwd")
    pq = _matmul(h, p["w_qkv"], name="proj_qkv")
    zpg = _matmul(h, p["w_zpg"], name="proj_zpg")
    ba = _matmul(h, p["w_ba"], name="proj_ba")
    qkv3 = _qkv_fwd(pq, p["conv_qkv_w"], n_heads, name="qkv_fwd")
    bg = _gates_fwd(ba, p["a_log"], p["dt_bias"], n_heads, name="gates_fwd")
    o, st, ti = _gdn_fwd(qkv3, bg, name="gdn_fwd")
    mixed = _mix_fwd(o, zpg, p["gdn_nw"], p["pool_w"], p["pool_scale"], name="mix_fwd")
    x1 = _matmul(mixed, p["w_out"], add=x, name="out_proj")
    h2 = _rmsnorm_fwd(x1, p["norm_ffn_w"], name="norm_ffn_fwd")
    gu = _matmul(h2, p["w_up"], name="up_proj")
    act = _ffn_mid_fwd(gu, p["conv_ffn_w"], p["conv_ffn_b"], name="ffn_mid_fwd")
    x2 = _matmul(act, p["w_down"], add=x1, name="down_proj")
    return x2, (x, h, pq, zpg, ba, o, st, ti, mixed, x1, h2, gu, act)


def _layer_bwd(dx2, p, saved, n_heads):
    x, h, pq, zpg, ba, o, st, ti, mixed, x1, h2, gu, act = saved
    d = x.shape[1]
    f = act.shape[1]
    dact = _matmul(dx2, p["w_down"], tb=True, name="d_act")
    g_down = _matmul(act, dx2, ta=True, out_dtype=BF16, name="g_w_down")
    dgate, dup, dcw_p, dcb_p = _ffn_mid_bwd(gu, p["conv_ffn_w"], p["conv_ffn_b"], dact, name="ffn_mid_bwd")
    dh2 = _matmul(dgate, p["w_up"], tb=True, b_k0=0, name="d_h2_gate")
    dh2 = _matmul(dup, p["w_up"], tb=True, b_k0=f, add=dh2, name="d_h2_up")
    g_up = jnp.concatenate([_matmul(h2, dgate, ta=True, out_dtype=BF16, name="g_w_up_gate"),
                            _matmul(h2, dup, ta=True, out_dtype=BF16, name="g_w_up_up")], axis=1)
    dx1, dnf_p = _rmsnorm_bwd(x1, p["norm_ffn_w"], dh2, dx2, name="norm_ffn_bwd")
    dmix = _matmul(dx1, p["w_out"], tb=True, name="d_mixed")
    g_out = _matmul(mixed, dx1, ta=True, out_dtype=BF16, name="g_w_out")
    d_o, d4, g_pool, dnw_p, dps_p = _mix_bwd(o, zpg, p["gdn_nw"], p["pool_w"], p["pool_scale"], dmix, name="mix_bwd")
    qkv3 = _qkv_fwd(pq, p["conv_qkv_w"], n_heads, name="qkv_fwd")
    bg = _gates_fwd(ba, p["a_log"], p["dt_bias"], n_heads, name="gates_fwd")
    dqkv3, dbg = _gdn_bwd(qkv3, bg, st, ti, d_o, name="gdn_bwd")
    dpq, dcq_p = _qkv_bwd(pq, p["conv_qkv_w"], dqkv3, n_heads, name="qkv_bwd")
    dba, dgate_p = _gates_bwd(ba, p["a_log"], p["dt_bias"], dbg, n_heads, name="gates_bwd")
    dh = _matmul(dpq, p["w_qkv"], tb=True, name="d_h_qkv")
    for seg in range(4):
        dh = _matmul(d4, p["w_zpg"], tb=True, a_part=seg, b_k0=seg * d, add=dh, name="d_h_zpg")
    dh = _matmul(dba, p["w_ba"], tb=True, add=dh, name="d_h_ba")
    g_qkv = _matmul(h, dpq, ta=True, out_dtype=BF16, name="g_w_qkv")
    g_zpg = [_matmul(h, d4, ta=True, b_part=seg, out_dtype=BF16, name="g_w_zpg") for seg in range(4)]
    g_ba = _matmul(h, dba, ta=True, out_dtype=BF16, name="g_w_ba")
    dx, dnm_p = _rmsnorm_bwd(x, p["norm_mix_w"], dh, dx1, name="norm_mix_bwd")
    g_in = _w_in_grad_parts(g_qkv, g_zpg, g_ba, n_heads, name="g_w_in_parts")
    rows = lambda t: jnp.sum(t, axis=0)
    ni8, c12 = dcq_p.shape
    nj = dcw_p.shape[1] // (3 * _tile(f, 512))
    small = {
        "norm_mix_w": rows(dnm_p),
        "conv_qkv_w": rows(dcq_p).reshape(c12 // (4 * HEAD_DIM), 4, HEAD_DIM).transpose(1, 0, 2).reshape(4, c12 // 4),
        "a_log": rows(dgate_p)[:n_heads],
        "dt_bias": rows(dgate_p)[n_heads:2 * n_heads],
        "gdn_norm_w": jnp.sum(rows(dnw_p).reshape(d // HEAD_DIM, HEAD_DIM), axis=0),
        "pool_scale": rows(dps_p),
        "norm_ffn_w": rows(dnf_p),
        "conv_ffn_w": rows(dcw_p).reshape(nj, 3, f // nj).transpose(1, 0, 2).reshape(3, f),
        "conv_ffn_b": rows(dcb_p),
    }
    big = {"w_in": g_in, "pool_w": g_pool, "w_out": g_out, "w_up": g_up, "w_down": g_down}
    return dx, big, small


BIG = ("w_in", "pool_w", "w_out", "w_up", "w_down")
SMALL = ("norm_mix_w", "conv_qkv_w", "a_log", "dt_bias", "gdn_norm_w", "pool_scale", "norm_ffn_w", "conv_ffn_w",
         "conv_ffn_b", "norm_final_w")
WEIGHTS = ("norm_mix_w", "w_in", "conv_qkv_w", "a_log", "dt_bias", "gdn_norm_w", "pool_w", "pool_scale", "w_out",
           "norm_ffn_w", "w_up", "conv_ffn_w", "conv_ffn_b", "w_down", "norm_final_w")
N_CHIPS = 4


def _gather_shards(local, cc, *, name):
    m = local.shape[0] // 2
    return _allgather(lax.dynamic_slice_in_dim(local, cc * m, m, axis=0), name=name)


def _full_weights(w, l, cc):
    d = w["w_in"].shape[1]
    wi = _gather_shards(w["w_in"][l].astype(BF16), cc, name="gather_w_in").reshape(N_CHIPS, d, -1)
    pw = w["pool_w"][l].astype(BF16)
    g, r, c = pw.shape
    pool = _gather_shards(pw.reshape(g * r, c), cc, name="gather_pool_w").reshape(N_CHIPS, g, r, c)
    wu = _gather_shards(w["w_up"][l].astype(BF16), cc, name="gather_w_up").reshape(N_CHIPS, d, -1)
    return dict(
        w_in=jnp.concatenate([wi[j] for j in range(N_CHIPS)], axis=1),
        pool_w=pool.transpose(1, 0, 2, 3).reshape(g, N_CHIPS * r, c),
        w_out=_gather_shards(w["w_out"][l].astype(BF16), cc, name="gather_w_out"),
        w_up=wu.transpose(1, 0, 2).reshape(d, -1),
        w_down=_gather_shards(w["w_down"][l].astype(BF16), cc, name="gather_w_down"))


def _grad_parts(name, g):
    if name in ("w_in", "w_up"):
        r, c = g.shape
        return g.reshape(2, r // 2, N_CHIPS, c // N_CHIPS).transpose(2, 0, 1, 3).reshape(8, r // 2, c // N_CHIPS)
    if name == "pool_w":
        ng, r, c = g.shape
        t = g.reshape(2, ng // 2, N_CHIPS, r // N_CHIPS, c).transpose(2, 0, 1, 3, 4)
        return t.reshape(8, (ng // 2) * (r // N_CHIPS), c)
    r, c = g.shape
    return g.reshape(8, r // 8, c)


def _reduce_grad(name, g, shard_shape, slot):
    parts = g if name == "w_in" else _grad_parts(name, g.astype(BF16))
    got = _all_to_all(parts, name="scatter_" + name)
    both = _share_halves(_sum_parts(got, slot=slot, name="sum_" + name), name="share_" + name)
    return both.reshape(shard_shape)


def kernel(x, norm_mix_w, w_in, conv_qkv_w, a_log, dt_bias, gdn_norm_w, pool_w, pool_scale, w_out, norm_ffn_w, w_up, conv_ffn_w, conv_ffn_b, w_down, norm_final_w, loss_target, m_norm_mix_w, m_w_in, m_conv_qkv_w, m_a_log, m_dt_bias, m_gdn_norm_w, m_pool_w, m_pool_scale, m_w_out, m_norm_ffn_w, m_w_up, m_conv_ffn_w, m_conv_ffn_b, m_w_down, m_norm_final_w, v_norm_mix_w, v_w_in, v_conv_qkv_w, v_a_log, v_dt_bias, v_gdn_norm_w, v_pool_w, v_pool_scale, v_w_out, v_norm_ffn_w, v_w_up, v_conv_ffn_w, v_conv_ffn_b, v_w_down, v_norm_final_w):
    w = dict(norm_mix_w=norm_mix_w, w_in=w_in, conv_qkv_w=conv_qkv_w, a_log=a_log, dt_bias=dt_bias,
             gdn_norm_w=gdn_norm_w, pool_w=pool_w, pool_scale=pool_scale, w_out=w_out, norm_ffn_w=norm_ffn_w,
             w_up=w_up, conv_ffn_w=conv_ffn_w, conv_ffn_b=conv_ffn_b, w_down=w_down, norm_final_w=norm_final_w)
    m = dict(norm_mix_w=m_norm_mix_w, w_in=m_w_in, conv_qkv_w=m_conv_qkv_w, a_log=m_a_log, dt_bias=m_dt_bias,
             gdn_norm_w=m_gdn_norm_w, pool_w=m_pool_w, pool_scale=m_pool_scale, w_out=m_w_out,
             norm_ffn_w=m_norm_ffn_w, w_up=m_w_up, conv_ffn_w=m_conv_ffn_w, conv_ffn_b=m_conv_ffn_b,
             w_down=m_w_down, norm_final_w=m_norm_final_w)
    v = dict(norm_mix_w=v_norm_mix_w, w_in=v_w_in, conv_qkv_w=v_conv_qkv_w, a_log=v_a_log, dt_bias=v_dt_bias,
             gdn_norm_w=v_gdn_norm_w, pool_w=v_pool_w, pool_scale=v_pool_scale, w_out=v_w_out,
             norm_ffn_w=v_norm_ffn_w, w_up=v_w_up, conv_ffn_w=v_conv_ffn_w, conv_ffn_b=v_conv_ffn_b,
             w_down=v_w_down, norm_final_w=v_norm_final_w)
    depth, n_heads = a_log.shape
    d = x.shape[-1]
    dl = n_heads * HEAD_DIM
    assert dl == d and gdn_norm_w.shape[1] == HEAD_DIM
    cx, cy, cc = _place()
    chip = 2 * cx + cy

    per_layer = [_full_weights(w, l, cc) for l in range(depth)]
    conv_packed = _pack_small([conv_qkv_w, conv_ffn_w], 16)
    conv_all = _allgather(conv_packed, name="gather_conv").reshape(N_CHIPS, 2, -1, 128)[:, 0]
    conv_j = [_unpack_small(conv_all[j], [conv_qkv_w.shape, conv_ffn_w.shape]) for j in range(N_CHIPS)]
    conv_q = jnp.concatenate([t[0] for t in conv_j], axis=-1)
    conv_f = jnp.concatenate([t[1] for t in conv_j], axis=-1)

    params = []
    for l in range(depth):
        full = per_layer[l]
        wi = full["w_in"]
        params.append(dict(
            norm_mix_w=norm_mix_w[l][None], norm_ffn_w=norm_ffn_w[l][None],
            w_qkv=wi[:, :3 * dl],
            w_zpg=jnp.concatenate([wi[:, 3 * dl:4 * dl], wi[:, 4 * dl + 2 * n_heads:]], axis=1),
            w_ba=jnp.concatenate([wi[:, 4 * dl:4 * dl + 2 * n_heads], jnp.zeros((d, 128 - 2 * n_heads), BF16)], axis=1),
            conv_qkv_w=conv_q[l], a_log=a_log[l], dt_bias=dt_bias[l],
            gdn_nw=jnp.tile(gdn_norm_w[l], d // HEAD_DIM)[None], pool_w=full["pool_w"], pool_scale=pool_scale[l][None],
            w_out=full["w_out"], w_up=full["w_up"], conv_ffn_w=conv_f[l], conv_ffn_b=conv_ffn_b[l][None],
            w_down=full["w_down"]))

    xs = x[0]
    saved = []
    for l in range(depth):
        xs, sv = _layer_fwd(xs, params[l], n_heads)
        saved.append(sv)
    dx, dnf_p, loss_p = _final_loss(xs, norm_final_w[None], loss_target[0], name="final_loss")
    loss = lax.psum(jnp.sum(loss_p) * (0.5 / d), ("x", "y", "c"))
    big_g = [None] * depth
    small_g = [None] * depth
    for l in reversed(range(depth)):
        dx, big_g[l], small_g[l] = _layer_bwd(dx, params[l], saved[l], n_heads)

    slot = jnp.reshape(cc, (1,)).astype(jnp.int32)
    grads = {n: jnp.stack([_reduce_grad(n, big_g[l][n], w[n].shape[1:], slot) for l in range(depth)]) for n in BIG}

    small_shapes = {n: ((depth,) + small_g[0][n].shape if n != "norm_final_w" else (d,)) for n in SMALL}
    small_local = [jnp.stack([small_g[l][n] for l in range(depth)]) for n in SMALL[:-1]] + [jnp.sum(dnf_p, axis=0)]
    sp = _pack_small(small_local, 512)
    sg = _allgather(sp, name="gather_small").reshape(8, sp.shape[0], 128)
    small_sum = _unpack_small(_sum_parts(sg, name="sum_small"), [small_shapes[n] for n in SMALL])
    for n, g in zip(SMALL, small_sum):
        if n in ("conv_qkv_w", "conv_ffn_w"):
            cols = w[n].shape[-1]
            g = lax.dynamic_slice_in_dim(g, chip * cols, cols, axis=2)
        grads[n] = g

    delta, new_m, new_v = {}, {}, {}
    for n in BIG:
        shp = w[n].shape
        r2 = lambda t: t.reshape(-1, shp[-1])
        dd, mm, vv = _adamw(r2(w[n]), r2(grads[n]), r2(m[n]), r2(v[n]), name="adamw_" + n)
        delta[n], new_m[n], new_v[n] = dd.reshape(shp), mm.reshape(shp), vv.reshape(shp)
    pk = lambda src: _pack_small([src[n] for n in SMALL], 8)
    outs = _adamw(pk(w), pk(grads), pk(m), pk(v), name="adamw_small")
    for dst, buf in zip((delta, new_m, new_v), outs):
        for n, t in zip(SMALL, _unpack_small(buf, [w[n].shape for n in SMALL])):
            dst[n] = t

    return (loss, dx[None], *[grads[n] for n in WEIGHTS], *[delta[n] for n in WEIGHTS],
            *[new_m[n] for n in WEIGHTS], *[new_v[n] for n in WEIGHTS])
```

```python
import functools
import math

import jax
import jax.numpy as jnp
from jax import lax
from jax.experimental import pallas as pl
from jax.experimental.pallas import tpu as pltpu

F32 = jnp.float32
BF16 = jnp.bfloat16
EPS = 1e-6
CHUNK = 64
HEAD_DIM = 128
POOL_WINDOWS = (2, 4, 8, 16)
HALO = 16
ADAM_LR, ADAM_B1, ADAM_B2, ADAM_EPS, ADAM_WD, ADAM_STEP = 0.001, 0.9, 0.999, 1e-08, 0.01, 10
V7X_VMEM_LIMIT = 56 * 1024 * 1024
SUM_BLOCK_BYTES = 6 * 1024 * 1024
MAX_COPY_BYTES = 8 * 1024 * 1024
MATMUL_TILE_BYTES = 10 * 1024 * 1024
MESH = pl.DeviceIdType.MESH


def _tile(n, cap, align=128):
    if n <= cap:
        return n
    t = (cap // align) * align
    while t >= align:
        if n % t == 0:
            return t
        t -= align
    return n


def _params(*sem):
    return pltpu.CompilerParams(dimension_semantics=sem, vmem_limit_bytes=V7X_VMEM_LIMIT)


def _sigmoid(x):
    return 1.0 / (1.0 + jnp.exp(-x))


def _down(x, j):
    return pltpu.roll(x, j, 0)


def _up(x, j):
    return pltpu.roll(x, x.shape[0] - j, 0)


def _fold8(x):
    n, c = x.shape
    return jnp.sum(x.reshape(n // 8, 8, c), axis=0)


def _matmul(a, b, *, ta=False, tb=False, add=None, out_dtype=F32, b_k0=0, b_n0=0, n=None, a_part=None, b_part=None, name):
    a2, b2 = a.shape[-2:], b.shape[-2:]
    m, k = (a2[1], a2[0]) if ta else a2
    if n is None:
        n = b2[0] if tb else b2[1]
    tm, tn = _tile(m, 1024), _tile(n, 1024)
    per_k = tm * a.dtype.itemsize + tn * b.dtype.itemsize
    tk = _tile(k, max(128, MATMUL_TILE_BYTES // per_k // 128 * 128))
    nk = k // tk
    assert b_k0 % tk == 0 and b_n0 % tn == 0, (b_k0, b_n0, tk, tn)
    ko, no = b_k0 // tk, b_n0 // tn

    def spec(shape, index, part):
        if part is None:
            return pl.BlockSpec(shape, index)
        return pl.BlockSpec((None,) + shape, lambda i, j, kk: (part,) + index(i, j, kk))

    a_spec = spec((tk, tm), lambda i, j, kk: (kk, i), a_part) if ta else spec((tm, tk), lambda i, j, kk: (i, kk), a_part)
    b_spec = (spec((tn, tk), lambda i, j, kk: (j + no, kk + ko), b_part) if tb
              else spec((tk, tn), lambda i, j, kk: (kk + ko, j + no), b_part))
    o_spec = pl.BlockSpec((tm, tn), lambda i, j, kk: (i, j))
    dims = (((0 if ta else 1,), (1 if tb else 0,)), ((), ()))
    has_add = add is not None

    def body(*refs):
        a_ref, b_ref = refs[:2]
        add_ref = refs[2] if has_add else None
        o_ref = refs[3 if has_add else 2]
        part = lax.dot_general(a_ref[...].astype(BF16), b_ref[...].astype(BF16), dims, preferred_element_type=F32)

        def finish(r):
            if has_add:
                r = r + add_ref[...]
            o_ref[...] = r.astype(out_dtype)

        if nk == 1:
            finish(part)
            return
        acc = refs[-1]
        kk = pl.program_id(2)

        @pl.when(kk == 0)
        def _():
            acc[...] = part

        @pl.when(kk > 0)
        def _():
            acc[...] += part

        @pl.when(kk == nk - 1)
        def _():
            finish(acc[...])

    ins = [a, b] + ([add] if has_add else [])
    specs = [a_spec, b_spec] + ([o_spec] if has_add else [])
    return pl.pallas_call(
        body, name=name, grid=(m // tm, n // tn, nk), in_specs=specs, out_specs=o_spec,
        out_shape=jax.ShapeDtypeStruct((m, n), out_dtype),
        scratch_shapes=[pltpu.VMEM((tm, tn), F32)] if nk > 1 else [],
        compiler_params=_params("parallel", "parallel", "arbitrary"))(*ins)


def _rmsnorm_fwd(x, w, *, name):
    s, d = x.shape
    ts = _tile(s, 512, 8)

    def body(x_ref, w_ref, o_ref):
        xv = x_ref[...]
        r = lax.rsqrt(jnp.mean(xv * xv, axis=-1, keepdims=True) + EPS)
        o_ref[...] = (xv * r * w_ref[...]).astype(BF16)

    return pl.pallas_call(
        body, name=name, grid=(s // ts,),
        in_specs=[pl.BlockSpec((ts, d), lambda i: (i, 0)), pl.BlockSpec((1, d), lambda i: (0, 0))],
        out_specs=pl.BlockSpec((ts, d), lambda i: (i, 0)),
        out_shape=jax.ShapeDtypeStruct((s, d), BF16), compiler_params=_params("parallel"))(x, w)


def _rmsnorm_bwd(x, w, dh, dres, *, name):
    s, d = x.shape
    ts = _tile(s, 512, 8)

    def body(x_ref, w_ref, dh_ref, dres_ref, dx_ref, dw_ref):
        xv = x_ref[...]
        r = lax.rsqrt(jnp.mean(xv * xv, axis=-1, keepdims=True) + EPS)
        xh = xv * r
        dhv = dh_ref[...]
        dxh = dhv * w_ref[...]
        dx_ref[...] = dres_ref[...] + r * (dxh - xh * jnp.mean(dxh * xh, axis=-1, keepdims=True))

        @pl.when(pl.program_id(0) == 0)
        def _():
            dw_ref[...] = jnp.zeros_like(dw_ref)

        dw_ref[...] += _fold8(dhv * xh)

    row = pl.BlockSpec((ts, d), lambda i: (i, 0))
    return pl.pallas_call(
        body, name=name, grid=(s // ts,),
        in_specs=[row, pl.BlockSpec((1, d), lambda i: (0, 0)), row, row],
        out_specs=[row, pl.BlockSpec((8, d), lambda i: (0, 0))],
        out_shape=[jax.ShapeDtypeStruct((s, d), F32), jax.ShapeDtypeStruct((8, d), F32)],
        compiler_params=_params("arbitrary"))(x, w, dh, dres)


def _final_loss(x, w, target, *, name):
    s, d = x.shape
    ts = _tile(s, 512, 8)

    def body(x_ref, w_ref, t_ref, dx_ref, dw_ref, loss_ref):
        xv = x_ref[...]
        r = lax.rsqrt(jnp.mean(xv * xv, axis=-1, keepdims=True) + EPS)
        xh = xv * r
        wv = w_ref[...]
        err = xh * wv - t_ref[...]
        dy = err * (1.0 / d)
        dxh = dy * wv
        dx_ref[...] = r * (dxh - xh * jnp.mean(dxh * xh, axis=-1, keepdims=True))

        @pl.when(pl.program_id(0) == 0)
        def _():
            dw_ref[...] = jnp.zeros_like(dw_ref)
            loss_ref[...] = jnp.zeros_like(loss_ref)

        dw_ref[...] += _fold8(dy * xh)
        e2 = _fold8(err * err)
        part = e2[:, 0:128]
        for j in range(1, d // 128):
            part = part + e2[:, j * 128:(j + 1) * 128]
        loss_ref[...] += part

    row = pl.BlockSpec((ts, d), lambda i: (i, 0))
    return pl.pallas_call(
        body, name=name, grid=(s // ts,),
        in_specs=[row, pl.BlockSpec((1, d), lambda i: (0, 0)), row],
        out_specs=[row, pl.BlockSpec((8, d), lambda i: (0, 0)), pl.BlockSpec((8, 128), lambda i: (0, 0))],
        out_shape=[jax.ShapeDtypeStruct((s, d), F32), jax.ShapeDtypeStruct((8, d), F32),
                   jax.ShapeDtypeStruct((8, 128), F32)],
        compiler_params=_params("arbitrary"))(x, w, target)


def _prev_spec(ts, tc, col):
    return pl.BlockSpec((HALO, tc), lambda i, j: (jnp.maximum(i * (ts // HALO) - 1, 0), col(j)))


def _next_spec(ts, tc, col, n_tiles):
    return pl.BlockSpec((HALO, tc), lambda i, j: (jnp.minimum((i + 1) * (ts // HALO), n_tiles * (ts // HALO) - 1), col(j)))


def _with_prev(prev, cur, first):
    return jnp.concatenate([jnp.where(first, 0.0, prev), cur], axis=0)


def _with_next(cur, nxt, last):
    return jnp.concatenate([cur, jnp.where(last, 0.0, nxt)], axis=0)


def _gelu(x):
    return 0.5 * x * (1.0 + lax.erf(x * (1.0 / math.sqrt(2.0))))


def _gelu_grad(x):
    return 0.5 * (1.0 + lax.erf(x * (1.0 / math.sqrt(2.0)))) + x * jnp.exp(-0.5 * x * x) * (1.0 / math.sqrt(2.0 * math.pi))


def _ffn_conv(prev, cur, w, first):
    xx = _with_prev(prev, cur, first)
    return w[2:3] * cur + w[1:2] * _down(xx, 1)[HALO:] + w[0:1] * _down(xx, 2)[HALO:]


def _ffn_mid_fwd(gu, cw, cb, *, name):
    s, f2 = gu.shape
    f = f2 // 2
    ts, tc = _tile(s, 512, HALO), _tile(f, 512)
    nj = f // tc

    def body(g_ref, gp_ref, u_ref, w_ref, b_ref, o_ref):
        first = pl.program_id(0) == 0
        gc = _ffn_conv(gp_ref[...], g_ref[...], w_ref[...], first) + b_ref[...]
        o_ref[...] = (_gelu(gc) * u_ref[...]).astype(BF16)

    return pl.pallas_call(
        body, name=name, grid=(s // ts, nj),
        in_specs=[pl.BlockSpec((ts, tc), lambda i, j: (i, j)), _prev_spec(ts, tc, lambda j: j),
                  pl.BlockSpec((ts, tc), lambda i, j: (i, j + nj)),
                  pl.BlockSpec((3, tc), lambda i, j: (0, j)), pl.BlockSpec((1, tc), lambda i, j: (0, j))],
        out_specs=pl.BlockSpec((ts, tc), lambda i, j: (i, j)),
        out_shape=jax.ShapeDtypeStruct((s, f), BF16), compiler_params=_params("parallel", "parallel"))(gu, gu, gu, cw, cb)


def _ffn_mid_bwd(gu, cw, cb, dact, *, name):
    s, f2 = gu.shape
    f = f2 // 2
    ts, tc = _tile(s, 512, HALO), _tile(f, 512)
    nj, ni = f // tc, s // ts

    def body(g_ref, gp_ref, gn_ref, u_ref, un_ref, d_ref, dn_ref, w_ref, b_ref, dg_ref, du_ref, dw_ref, db_ref):
        i = pl.program_id(0)
        first, last = i == 0, i == ni - 1
        w, b = w_ref[...], b_ref[...]
        g = g_ref[...]
        gx = jnp.concatenate([jnp.where(first, 0.0, gp_ref[...]), g, jnp.where(last, 0.0, gn_ref[...])], axis=0)
        g1, g2 = _down(gx, 1), _down(gx, 2)
        gc = (w[2:3] * gx + w[1:2] * g1 + w[0:1] * g2)[HALO:] + b
        ux = _with_next(u_ref[...], un_ref[...], last)
        dx = _with_next(d_ref[...], dn_ref[...], last)
        dgc = dx * ux * _gelu_grad(gc)
        du_ref[...] = (dx[:ts] * _gelu(gc[:ts])).astype(BF16)
        dg = w[2:3] * dgc + w[1:2] * _up(dgc, 1) + w[0:1] * _up(dgc, 2)
        dg_ref[...] = dg[:ts].astype(BF16)
        dgt = dgc[:ts]
        db_ref[...] = _fold8(dgt)
        dw_ref[:, 0:tc] = _fold8(dgt * g2[HALO:HALO + ts])
        dw_ref[:, tc:2 * tc] = _fold8(dgt * g1[HALO:HALO + ts])
        dw_ref[:, 2 * tc:3 * tc] = _fold8(dgt * g)

    cur = lambda off: pl.BlockSpec((ts, tc), lambda i, j: (i, j + off))
    return pl.pallas_call(
        body, name=name, grid=(ni, nj),
        in_specs=[cur(0), _prev_spec(ts, tc, lambda j: j), _next_spec(ts, tc, lambda j: j, ni),
                  cur(nj), _next_spec(ts, tc, lambda j: j + nj, ni),
                  cur(0), _next_spec(ts, tc, lambda j: j, ni),
                  pl.BlockSpec((3, tc), lambda i, j: (0, j)), pl.BlockSpec((1, tc), lambda i, j: (0, j))],
        out_specs=[cur(0), cur(0),
                   pl.BlockSpec((8, 3 * tc), lambda i, j: (i, j)), pl.BlockSpec((8, tc), lambda i, j: (i, j))],
        out_shape=[jax.ShapeDtypeStruct((s, f), BF16), jax.ShapeDtypeStruct((s, f), BF16),
                   jax.ShapeDtypeStruct((ni * 8, 3 * f), F32), jax.ShapeDtypeStruct((ni * 8, f), F32)],
        compiler_params=_params("parallel", "parallel"))(gu, gu, gu, gu, gu, dact, dact, cw, cb)


def _qkv_fwd(pq, cw, n_heads, *, name):
    s, c3 = pq.shape
    ts = _tile(s, 512, HALO)

    def body(x_ref, xp_ref, w_ref, o_ref):
        j = pl.program_id(1)
        w = w_ref[...]
        x = x_ref[...]
        xx = _with_prev(xp_ref[...], x, pl.program_id(0) == 0)
        y = w[3:4] * x + w[2:3] * _down(xx, 1)[HALO:] + w[1:2] * _down(xx, 2)[HALO:] + w[0:1] * _down(xx, 3)[HALO:]
        c = y * _sigmoid(y)
        r = lax.rsqrt(jnp.sum(c * c, axis=-1, keepdims=True) + EPS)
        scale = jnp.where(j < n_heads, HEAD_DIM ** -0.5, 1.0)
        o_ref[...] = jnp.where(j < 2 * n_heads, c * (r * scale), c)

    return pl.pallas_call(
        body, name=name, grid=(s // ts, 3 * n_heads),
        in_specs=[pl.BlockSpec((ts, HEAD_DIM), lambda i, j: (i, j)), _prev_spec(ts, HEAD_DIM, lambda j: j),
                  pl.BlockSpec((4, HEAD_DIM), lambda i, j: (0, j))],
        out_specs=pl.BlockSpec((None, ts, HEAD_DIM), lambda i, j: (j // n_heads, i, j % n_heads)),
        out_shape=jax.ShapeDtypeStruct((3, s, c3 // 3), F32),
        compiler_params=_params("parallel", "parallel"))(pq, pq, cw)


def _qkv_bwd(pq, cw, dqkv3, n_heads, *, name):
    s, c3 = pq.shape
    ts = _tile(s, 512, HALO)
    ni = s // ts
    hd = HEAD_DIM

    def body(x_ref, xp_ref, xn_ref, w_ref, d_ref, dn_ref, dx_ref, dw_ref):
        i, j = pl.program_id(0), pl.program_id(1)
        first, last = i == 0, i == ni - 1
        w = w_ref[...]
        x = x_ref[...]
        xx = jnp.concatenate([jnp.where(first, 0.0, xp_ref[...]), x, jnp.where(last, 0.0, xn_ref[...])], axis=0)
        x1, x2, x3 = _down(xx, 1), _down(xx, 2), _down(xx, 3)
        y = (w[3:4] * xx + w[2:3] * x1 + w[1:2] * x2 + w[0:1] * x3)[HALO:]
        sg = _sigmoid(y)
        c = y * sg
        dn = _with_next(d_ref[...], dn_ref[...], last)
        r = lax.rsqrt(jnp.sum(c * c, axis=-1, keepdims=True) + EPS)
        nrm = c * r
        dnn = dn * jnp.where(j < n_heads, hd ** -0.5, 1.0)
        dc = jnp.where(j < 2 * n_heads, r * (dnn - nrm * jnp.sum(dnn * nrm, axis=-1, keepdims=True)), dn)
        dy = dc * (sg * (1.0 + y * (1.0 - sg)))
        dx = w[3:4] * dy + w[2:3] * _up(dy, 1) + w[1:2] * _up(dy, 2) + w[0:1] * _up(dy, 3)
        dx_ref[...] = dx[:ts].astype(BF16)
        dyt = dy[:ts]
        dw_ref[:, 0:hd] = _fold8(dyt * x3[HALO:HALO + ts])
        dw_ref[:, hd:2 * hd] = _fold8(dyt * x2[HALO:HALO + ts])
        dw_ref[:, 2 * hd:3 * hd] = _fold8(dyt * x1[HALO:HALO + ts])
        dw_ref[:, 3 * hd:4 * hd] = _fold8(dyt * x)

    dspec = lambda rows, row_index: pl.BlockSpec(
        (None, rows, hd), lambda i, j: (j // n_heads, row_index(i), j % n_heads))
    return pl.pallas_call(
        body, name=name, grid=(ni, 3 * n_heads),
        in_specs=[pl.BlockSpec((ts, hd), lambda i, j: (i, j)), _prev_spec(ts, hd, lambda j: j),
                  _next_spec(ts, hd, lambda j: j, ni), pl.BlockSpec((4, hd), lambda i, j: (0, j)),
                  dspec(ts, lambda i: i),
                  dspec(HALO, lambda i: jnp.minimum((i + 1) * (ts // HALO), ni * (ts // HALO) - 1))],
        out_specs=[pl.BlockSpec((ts, hd), lambda i, j: (i, j)), pl.BlockSpec((8, 4 * hd), lambda i, j: (i, j))],
        out_shape=[jax.ShapeDtypeStruct((s, c3), BF16), jax.ShapeDtypeStruct((ni * 8, 4 * c3), F32)],
        compiler_params=_params("parallel", "parallel"))(pq, pq, pq, cw, dqkv3, dqkv3)


def _gate_terms(ba, al, dt, h, n_heads):
    lane = lax.broadcasted_iota(jnp.int32, ba.shape, 1)
    braw = jnp.sum(jnp.where(lane == h, ba, 0.0), axis=1, keepdims=True)
    araw = jnp.sum(jnp.where(lane == h + n_heads, ba, 0.0), axis=1, keepdims=True)
    beta = _sigmoid(braw)
    z = araw + dt
    sp = jnp.maximum(z, 0.0) + jnp.log(1.0 + jnp.exp(-jnp.abs(z)))
    ea = jnp.exp(jnp.zeros((1, 1), F32) + al)
    return beta, z, sp, ea


def _gates_fwd(ba, a_log, dt_bias, n_heads, *, name):
    s = ba.shape[0]
    ts = _tile(s, 512, CHUNK)

    def body(ba_ref, al_ref, dt_ref, o_ref):
        h = pl.program_id(1)
        beta, _, sp, ea = _gate_terms(ba_ref[...], al_ref[h], dt_ref[h], h, n_heads)
        gx = jnp.broadcast_to(-ea * sp, (ts, HEAD_DIM))
        rc = lax.broadcasted_iota(jnp.int32, (ts, HEAD_DIM), 0) & (CHUNK - 1)
        for sh in (1, 2, 4, 8, 16, 32):
            gx = gx + jnp.where(rc >= sh, _down(gx, sh), 0.0)
        o_ref[0] = jnp.broadcast_to(beta, (ts, HEAD_DIM))
        o_ref[1] = gx

    smem = pl.BlockSpec(memory_space=pltpu.SMEM)
    return pl.pallas_call(
        body, name=name, grid=(s // ts, n_heads),
        in_specs=[pl.BlockSpec((ts, 128), lambda i, h: (i, 0)), smem, smem],
        out_specs=pl.BlockSpec((2, ts, HEAD_DIM), lambda i, h: (0, i, h)),
        out_shape=jax.ShapeDtypeStruct((2, s, n_heads * HEAD_DIM), F32),
        compiler_params=_params("parallel", "parallel"))(ba, a_log, dt_bias)


def _gates_bwd(ba, a_log, dt_bias, dbg, n_heads, *, name):
    s = ba.shape[0]
    ts = _tile(s, 512, CHUNK)
    ni = s // ts

    def body(ba_ref, al_ref, dt_ref, d_ref, o_ref, p_ref):
        h = pl.program_id(1)
        beta, z, sp, ea = _gate_terms(ba_ref[...], al_ref[h], dt_ref[h], h, n_heads)
        dg = d_ref[1]
        rc = lax.broadcasted_iota(jnp.int32, (ts, HEAD_DIM), 0) & (CHUNK - 1)
        for sh in (1, 2, 4, 8, 16, 32):
            dg = dg + jnp.where(rc < CHUNK - sh, _up(dg, sh), 0.0)
        daraw = dg * (-ea * _sigmoid(z))
        dbraw = d_ref[0] * (beta * (1.0 - beta))

        @pl.when(h == 0)
        def _():
            o_ref[...] = jnp.zeros_like(o_ref)
            p_ref[...] = jnp.zeros_like(p_ref)

        lane = lax.broadcasted_iota(jnp.int32, (1, 128), 1)
        is_b, is_a = lane == h, lane == h + n_heads
        o_ref[...] += jnp.where(is_b, dbraw, 0.0) + jnp.where(is_a, daraw, 0.0)
        p_ref[...] += jnp.where(is_b, _fold8(dg * (-ea * sp)), 0.0) + jnp.where(is_a, _fold8(daraw), 0.0)

    smem = pl.BlockSpec(memory_space=pltpu.SMEM)
    return pl.pallas_call(
        body, name=name, grid=(ni, n_heads),
        in_specs=[pl.BlockSpec((ts, 128), lambda i, h: (i, 0)), smem, smem,
                  pl.BlockSpec((2, ts, HEAD_DIM), lambda i, h: (0, i, h))],
        out_specs=[pl.BlockSpec((ts, 128), lambda i, h: (i, 0)), pl.BlockSpec((8, 128), lambda i, h: (i, 0))],
        out_shape=[jax.ShapeDtypeStruct((s, 128), F32), jax.ShapeDtypeStruct((ni * 8, 128), F32)],
        compiler_params=_params("parallel", "arbitrary"))(ba, a_log, dt_bias, dbg)


BLK = 2 * CHUNK
HEADS_PER_STEP = 2
NN = (((1,), (0,)), ((), ()))
NT = (((1,), (1,)), ((), ()))
TN = (((0,), (0,)), ((), ()))


def _dot(a, b, dims):
    return lax.dot_general(a.astype(BF16), b.astype(BF16), dims, preferred_element_type=F32)


def _dot3(a, b, dims):
    ah, bh = a.astype(BF16), b.astype(BF16)
    al, bl = (a - ah.astype(F32)).astype(BF16), (b - bh.astype(F32)).astype(BF16)
    d = lambda u, v: lax.dot_general(u, v, dims, preferred_element_type=F32)
    return d(ah, bh) + (d(ah, bl) + d(al, bh))


def _pair_masks():
    row = lax.broadcasted_iota(jnp.int32, (BLK, BLK), 0)
    col = lax.broadcasted_iota(jnp.int32, (BLK, BLK), 1)
    same = (row < CHUNK) == (col < CHUNK)
    return same & (row >= col), same & (row > col), row == col


def _pair_terms(q, k, v, b, gam, masks):
    tril, strict, eye = masks
    g_cols = jnp.sum(jnp.where(eye, gam, 0.0), axis=0, keepdims=True)
    dmat = jnp.exp(jnp.where(tril, gam - g_cols, -jnp.inf))
    eg = jnp.exp(gam)
    rowi = lax.broadcasted_iota(jnp.int32, (BLK, HEAD_DIM), 0)
    elast = jnp.exp(jnp.where(rowi < CHUNK, gam[CHUNK - 1:CHUNK], gam[BLK - 1:BLK]) - gam)
    kb, vb = k * b, v * b
    kq = _dot(jnp.concatenate([kb, q], axis=0), k, NT)
    lmat = jnp.where(strict, kq[:BLK] * dmat, 0.0)
    attn = kq[BLK:] * dmat
    rhs = jnp.concatenate([vb, kb * eg], axis=1)
    return dmat, eg, elast, kb, lmat, attn, rhs


def _unit_lower_inverse(lmat, eye):
    p = -lmat
    t = jnp.where(eye, 1.0, 0.0) + p
    p = _dot3(p, p, NN)
    for _ in range(4):
        r = _dot3(jnp.concatenate([p, t], axis=0), p, NN)
        p, t = r[:BLK], t + r[BLK:]
    return t + _dot3(t, p, NN)


def _ride(step, in_refs, out_refs, sems, n_steps, *, start):
    if not in_refs:
        return
    i, j = pl.program_id(0), pl.program_id(1)
    when = (i == 0) & (j == 0) if start else (i == n_steps[0] - 1) & (j == n_steps[1] - 1)

    @pl.when(when)
    def _():
        for t, (src, dst) in enumerate(zip(in_refs, out_refs)):
            step(src, dst, *sems[3 * t:3 * t + 3], start=start, finish=not start)


def _gdn_fwd(qkv3, bg, gather=(), *, name):
    _, s, dl = qkv3.shape
    n_heads = dl // HEAD_DIM
    sb = _tile(s, 1024, BLK)
    npair = sb // BLK
    c = CHUNK

    hpg = HEADS_PER_STEP if n_heads % HEADS_PER_STEP == 0 else 1
    wd = hpg * HEAD_DIM

    n_x = len(gather)
    n_steps = (n_heads // hpg, s // sb)

    def body(*refs):
        qkv_ref, bg_ref = refs[:2]
        o_ref, st_ref, ti_ref = refs[2 + n_x:5 + n_x]
        s_scr = refs[5 + 2 * n_x]
        _ride(_allgather_step, refs[2:2 + n_x], refs[5 + n_x:5 + 2 * n_x], refs[6 + 2 * n_x:], n_steps, start=True)

        @pl.when(pl.program_id(1) == 0)
        def _():
            s_scr[...] = jnp.zeros_like(s_scr)

        masks = _pair_masks()

        def head_pair(hh, p, rows, state):
            ls = slice(hh * HEAD_DIM, (hh + 1) * HEAD_DIM)
            q, k, v = qkv_ref[0, rows, ls], qkv_ref[1, rows, ls], qkv_ref[2, rows, ls]
            b, gam = bg_ref[0, rows, ls], bg_ref[1, rows, ls]
            _, eg, elast, _, lmat, attn, rhs = _pair_terms(q, k, v, b, gam, masks)
            tinv = _unit_lower_inverse(lmat, masks[2])
            ti_ref[hh, rows, :] = tinv
            sol = _dot3(tinv, rhs, NN)
            u, w = sol[:, :HEAD_DIM], sol[:, HEAD_DIM:]
            qd, ke = q * eg, k * elast
            st_ref[hh, 2 * p] = state
            wq = _dot(jnp.concatenate([w[:c], qd[:c]], axis=0), state, NN)
            vn_a, o_a = u[:c] - wq[:c], wq[c:]
            state = state * jnp.exp(gam[c - 1:c]) + _dot(ke[:c], vn_a, TN)
            st_ref[hh, 2 * p + 1] = state
            wq = _dot(jnp.concatenate([w[c:], qd[c:]], axis=0), state, NN)
            vn_b, o_b = u[c:] - wq[:c], wq[c:]
            state = state * jnp.exp(gam[BLK - 1:BLK]) + _dot(ke[c:], vn_b, TN)
            o_ref[rows, ls] = jnp.concatenate([o_a, o_b], axis=0) + _dot(attn, jnp.concatenate([vn_a, vn_b], axis=0), NN)
            return state

        def pair(p, states):
            rows = pl.ds(pl.multiple_of(p * BLK, BLK), BLK)
            return tuple(head_pair(hh, p, rows, states[hh]) for hh in range(hpg))

        states = lax.fori_loop(0, npair, pair, tuple(s_scr[hh] for hh in range(hpg)))
        for hh in range(hpg):
            s_scr[hh] = states[hh]
        _ride(_allgather_step, refs[2:2 + n_x], refs[5 + n_x:5 + 2 * n_x], refs[6 + 2 * n_x:], n_steps, start=False)

    outs = pl.pallas_call(
        body, name=name, grid=n_steps,
        in_specs=[pl.BlockSpec((3, sb, wd), lambda h, j: (0, j, h)),
                  pl.BlockSpec((2, sb, wd), lambda h, j: (0, j, h))] + [_HBM] * n_x,
        out_specs=[pl.BlockSpec((sb, wd), lambda h, j: (j, h)),
                   pl.BlockSpec((hpg, 2 * npair, HEAD_DIM, HEAD_DIM), lambda h, j: (h, j, 0, 0)),
                   pl.BlockSpec((hpg, sb, BLK), lambda h, j: (h, j, 0))] + [_HBM] * n_x,
        out_shape=[jax.ShapeDtypeStruct((s, dl), F32),
                   jax.ShapeDtypeStruct((n_heads, s // c, HEAD_DIM, HEAD_DIM), F32),
                   jax.ShapeDtypeStruct((n_heads, s, BLK), F32)] + [_allgather_shape(t) for t in gather],
        scratch_shapes=[pltpu.VMEM((hpg, HEAD_DIM, HEAD_DIM), F32)] + _EXCHANGE_SEMS * n_x,
        compiler_params=_params("arbitrary", "arbitrary"))(qkv3, bg, *gather)
    return outs[0], outs[1], outs[2], list(outs[3:])


def _gdn_bwd(qkv3, bg, st, ti, d_o, scatter=(), *, name):
    _, s, dl = qkv3.shape
    n_heads = dl // HEAD_DIM
    sb = _tile(s, 1024, BLK)
    npair, nsb = sb // BLK, s // sb
    c = CHUNK
    hpg = HEADS_PER_STEP if n_heads % HEADS_PER_STEP == 0 else 1
    wd = hpg * HEAD_DIM

    n_x = len(scatter)
    n_steps = (n_heads // hpg, nsb)

    def body(*refs):
        qkv_ref, bg_ref, st_ref, ti_ref, do_ref = refs[:5]
        dqkv_ref, dbg_ref = refs[5 + n_x:7 + n_x]
        ds_scr = refs[7 + 2 * n_x]
        _ride(_all_to_all_step, refs[5:5 + n_x], refs[7 + n_x:7 + 2 * n_x], refs[8 + 2 * n_x:], n_steps, start=True)

        @pl.when(pl.program_id(1) == 0)
        def _():
            ds_scr[...] = jnp.zeros_like(ds_scr)

        masks = _pair_masks()
        tril, strict, eye = masks
        rowc = lax.broadcasted_iota(jnp.int32, (BLK, 1), 0)

        def total(x):
            return jnp.sum(jnp.sum(x, axis=1, keepdims=True), axis=0, keepdims=True)

        def head_pair(hh, p, rows, ds2):
            ls = slice(hh * HEAD_DIM, (hh + 1) * HEAD_DIM)
            q, k, v = qkv_ref[0, rows, ls], qkv_ref[1, rows, ls], qkv_ref[2, rows, ls]
            b, gam = bg_ref[0, rows, ls], bg_ref[1, rows, ls]
            tinv, dout = ti_ref[hh, rows, :], do_ref[rows, ls]
            s0, s1 = st_ref[hh, 2 * p], st_ref[hh, 2 * p + 1]
            dmat, eg, elast, kb, lmat, attn, rhs = _pair_terms(q, k, v, b, gam, masks)
            sol = _dot3(tinv, rhs, NN)
            u, w = sol[:, :HEAD_DIM], sol[:, HEAD_DIM:]
            qd, ke = q * eg, k * elast
            dec_a, dec_b = jnp.exp(gam[c - 1:c]), jnp.exp(gam[BLK - 1:BLK])
            vn = u - jnp.concatenate([_dot(w[:c], s0, NN), _dot(w[c:], s1, NN)], axis=0)
            dvn_o = _dot(attn, dout, TN)
            dvn_b = dvn_o[c:] + _dot(ke[c:], ds2, NN)
            ds1 = _dot(qd[c:], dout[c:], TN) + ds2 * dec_b - _dot(w[c:], dvn_b, TN)
            dvn_a = dvn_o[:c] + _dot(ke[:c], ds1, NN)
            ds0 = _dot(qd[:c], dout[:c], TN) + ds1 * dec_a - _dot(w[:c], dvn_a, TN)
            dvn = jnp.concatenate([dvn_a, dvn_b], axis=0)
            dke = jnp.concatenate([_dot(vn[:c], ds1, NT), _dot(vn[c:], ds2, NT)], axis=0)
            dw = -jnp.concatenate([_dot(dvn_a, s0, NT), _dot(dvn_b, s1, NT)], axis=0)
            dqd = jnp.concatenate([_dot(dout[:c], s0, NT), _dot(dout[c:], s1, NT)], axis=0)
            dattn = jnp.where(tril, _dot(dout, vn, NT), 0.0)
            drhs = _dot3(tinv, jnp.concatenate([dvn, dw], axis=1), TN)
            dl_ = jnp.where(strict, -_dot3(drhs, sol, NT), 0.0)
            dm, dqk = dl_ * dmat, dattn * dmat
            dvb, drw = drhs[:, :HEAD_DIM], drhs[:, HEAD_DIM:]
            dkb = _dot(dm, k, NN) + drw * eg
            dq = _dot(dqk, k, NN) + dqd * eg
            dk = _dot(dm, kb, TN) + _dot(dqk, q, TN) + dke * elast + dkb * b
            dbeta = jnp.sum(dvb * v + dkb * k, axis=1, keepdims=True)
            e = dl_ * lmat + dattn * attn
            e_cols = jnp.sum(jnp.where(eye, jnp.sum(e, axis=0, keepdims=True), 0.0), axis=1, keepdims=True)
            dke_ke = dke * ke
            dgam = (jnp.sum(e, axis=1, keepdims=True) - e_cols
                    + jnp.sum(drw * (kb * eg) + dqd * qd - dke_ke, axis=1, keepdims=True))
            tot_a = total(dke_ke[:c]) + total(s0 * ds1) * dec_a[:, :1]
            tot_b = total(dke_ke[c:]) + total(s1 * ds2) * dec_b[:, :1]
            dgam = dgam + jnp.where(rowc == c - 1, tot_a, 0.0) + jnp.where(rowc == BLK - 1, tot_b, 0.0)
            dqkv_ref[0, rows, ls] = dq
            dqkv_ref[1, rows, ls] = dk
            dqkv_ref[2, rows, ls] = dvb * b
            dbg_ref[0, rows, ls] = jnp.broadcast_to(dbeta, (BLK, HEAD_DIM))
            dbg_ref[1, rows, ls] = jnp.broadcast_to(dgam, (BLK, HEAD_DIM))
            return ds0

        def pair(pp, dstates):
            p = npair - 1 - pp
            rows = pl.ds(pl.multiple_of(p * BLK, BLK), BLK)
            return tuple(head_pair(hh, p, rows, dstates[hh]) for hh in range(hpg))

        dstates = lax.fori_loop(0, npair, pair, tuple(ds_scr[hh] for hh in range(hpg)))
        for hh in range(hpg):
            ds_scr[hh] = dstates[hh]
        _ride(_all_to_all_step, refs[5:5 + n_x], refs[7 + n_x:7 + 2 * n_x], refs[8 + 2 * n_x:], n_steps, start=False)

    rev = lambda j: nsb - 1 - j
    outs = pl.pallas_call(
        body, name=name, grid=n_steps,
        in_specs=[pl.BlockSpec((3, sb, wd), lambda h, j: (0, rev(j), h)),
                  pl.BlockSpec((2, sb, wd), lambda h, j: (0, rev(j), h)),
                  pl.BlockSpec((hpg, 2 * npair, HEAD_DIM, HEAD_DIM), lambda h, j: (h, rev(j), 0, 0)),
                  pl.BlockSpec((hpg, sb, BLK), lambda h, j: (h, rev(j), 0)),
                  pl.BlockSpec((sb, wd), lambda h, j: (rev(j), h))] + [_HBM] * n_x,
        out_specs=[pl.BlockSpec((3, sb, wd), lambda h, j: (0, rev(j), h)),
                   pl.BlockSpec((2, sb, wd), lambda h, j: (0, rev(j), h))] + [_HBM] * n_x,
        out_shape=[jax.ShapeDtypeStruct((3, s, dl), F32), jax.ShapeDtypeStruct((2, s, dl), F32)]
        + [jax.ShapeDtypeStruct(t.shape, t.dtype) for t in scatter],
        scratch_shapes=[pltpu.VMEM((hpg, HEAD_DIM, HEAD_DIM), F32)] + _EXCHANGE_SEMS * n_x,
        compiler_params=_params("arbitrary", "arbitrary"))(qkv3, bg, st, ti, d_o, *scatter)
    return outs[0], outs[1], list(outs[2:])


N_GROUPS = len(POOL_WINDOWS)


def _pick(g, vals):
    out = vals[-1]
    for i in range(len(vals) - 2, -1, -1):
        out = jnp.where(g == i, vals[i], out)
    return out


def _head_norm(o, nw):
    hats, outs = [], []
    for h in range(o.shape[1] // HEAD_DIM):
        sl = slice(h * HEAD_DIM, (h + 1) * HEAD_DIM)
        oh = o[:, sl]
        r = lax.rsqrt(jnp.mean(oh * oh, axis=-1, keepdims=True) + EPS)
        hats.append((oh * r, r))
        outs.append(oh * r * nw[:, sl])
    return hats, jnp.concatenate(outs, axis=1) if len(outs) > 1 else outs[0]


def _pool_counts(g, t0, n):
    t = (lax.broadcasted_iota(jnp.int32, (n, 1), 0) + t0 + 1).astype(F32)
    return jnp.minimum(t, _pick(g, [float(w) for w in POOL_WINDOWS]))


def _pool(prev, cur, first, g, t0):
    s = _with_prev(prev, cur, first)
    sums = []
    for sh in (1, 2, 4, 8):
        s = s + _down(s, sh)
        sums.append(s)
    return _pick(g, sums)[HALO:] / _pool_counts(g, t0, cur.shape[0]) - cur


def _mix_specs(ts, gw, ni):
    seg = lambda k: pl.BlockSpec((ts, gw), lambda g, i: (i, k * N_GROUPS + g))
    per = ts // HALO
    prev = lambda k: pl.BlockSpec((HALO, gw), lambda g, i: (jnp.maximum(i * per - 1, 0), k * N_GROUPS + g))
    nxt = lambda k: pl.BlockSpec((HALO, gw), lambda g, i: (jnp.minimum((i + 1) * per, ni * per - 1), k * N_GROUPS + g))
    vec = pl.BlockSpec((1, gw), lambda g, i: (0, g))
    pw = pl.BlockSpec((None, gw, gw), lambda g, i: (g, 0, 0))
    return seg, prev, nxt, vec, pw


def _mix_fwd(o, zpg, nw, pw, ps, *, name):
    s, d = o.shape
    gw = d // N_GROUPS
    ts = _tile(s, 512, HALO)
    ni = s // ts
    seg, prev, _, vec, pwspec = _mix_specs(ts, gw, ni)

    def body(o_ref, z_ref, p_ref, pp_ref, ga_ref, gb_ref, nw_ref, pw_ref, ps_ref, out_ref):
        g, i = pl.program_id(0), pl.program_id(1)
        _, on = _head_norm(o_ref[...], nw_ref[...])
        z = z_ref[...]
        ya = on * (z * _sigmoid(z))
        pooled = _pool(pp_ref[...], p_ref[...], i == 0, g, i * ts)
        yb = _dot(pooled, pw_ref[...], NN) * ps_ref[...]
        out_ref[...] = (_sigmoid(ga_ref[...]) * ya + _sigmoid(gb_ref[...]) * yb).astype(BF16)

    return pl.pallas_call(
        body, name=name, grid=(N_GROUPS, ni),
        in_specs=[seg(0), seg(0), seg(1), prev(1), seg(2), seg(3), vec, pwspec, vec],
        out_specs=seg(0), out_shape=jax.ShapeDtypeStruct((s, d), BF16),
        compiler_params=_params("parallel", "parallel"))(o, zpg, zpg, zpg, zpg, zpg, nw, pw, ps)


def _mix_bwd(o, zpg, nw, pw, ps, dmix, *, name):
    s, d = o.shape
    gw = d // N_GROUPS
    ts = _tile(s, 512, HALO)
    ni = s // ts
    seg, prev, nxt, vec, pwspec = _mix_specs(ts, gw, ni)

    def body(o_ref, z_ref, p_ref, pp_ref, ga_ref, gb_ref, gbn_ref, nw_ref, pw_ref, ps_ref, dm_ref, dmn_ref,
             do_ref, d4_ref, dpw_ref, dnw_ref, dps_ref):
        g, i = pl.program_id(0), pl.program_id(1)
        last = i == ni - 1
        nw, ps, pwv = nw_ref[...], ps_ref[...], pw_ref[...]
        ov, z = o_ref[...], z_ref[...]
        hats, on = _head_norm(ov, nw)
        sz = _sigmoid(z)
        silu = z * sz
        ya = on * silu
        pooled = _pool(pp_ref[...], p_ref[...], i == 0, g, i * ts)
        yp = _dot(pooled, pwv, NN)
        sga, sgb = _sigmoid(ga_ref[...]), _sigmoid(gb_ref[...])
        dm = dm_ref[...]
        dya, dyb = dm * sga, dm * sgb
        d4_ref[2] = (dm * ya * (sga * (1.0 - sga))).astype(BF16)
        d4_ref[3] = (dm * (yp * ps) * (sgb * (1.0 - sgb))).astype(BF16)
        dps_ref[...] = _fold8(dyb * yp)
        dyp = dyb * ps

        @pl.when(i == 0)
        def _():
            dpw_ref[...] = jnp.zeros_like(dpw_ref)

        dpw_ref[...] += _dot(pooled, dyp, TN)
        dyp_next = jnp.where(last, 0.0, dmn_ref[...] * _sigmoid(gbn_ref[...]) * ps)
        dpool = _dot(jnp.concatenate([dyp, dyp_next], axis=0), pwv, NT)
        a = dpool / _pool_counts(g, i * ts, ts + HALO)
        sums = []
        for sh in (1, 2, 4, 8):
            a = a + _up(a, sh)
            sums.append(a)
        d4_ref[1] = (_pick(g, sums)[:ts] - dpool[:ts]).astype(BF16)
        d4_ref[0] = (dya * on * (sz * (1.0 + z * (1.0 - sz)))).astype(BF16)
        don = dya * silu
        dos, dnws = [], []
        for h, (ohat, r) in enumerate(hats):
            sl = slice(h * HEAD_DIM, (h + 1) * HEAD_DIM)
            dxh = don[:, sl] * nw[:, sl]
            dos.append(r * (dxh - ohat * jnp.mean(dxh * ohat, axis=-1, keepdims=True)))
            dnws.append(_fold8(don[:, sl] * ohat))
        do_ref[...] = jnp.concatenate(dos, axis=1) if len(dos) > 1 else dos[0]
        dnw_ref[...] = jnp.concatenate(dnws, axis=1) if len(dnws) > 1 else dnws[0]

    part = pl.BlockSpec((8, gw), lambda g, i: (i, g))
    return pl.pallas_call(
        body, name=name, grid=(N_GROUPS, ni),
        in_specs=[seg(0), seg(0), seg(1), prev(1), seg(2), seg(3), nxt(3), vec, pwspec, vec, seg(0), nxt(0)],
        out_specs=[seg(0), pl.BlockSpec((4, ts, gw), lambda g, i: (0, i, g)),
                   pl.BlockSpec((None, gw, gw), lambda g, i: (g, 0, 0)), part, part],
        out_shape=[jax.ShapeDtypeStruct((s, d), F32), jax.ShapeDtypeStruct((4, s, d), BF16),
                   jax.ShapeDtypeStruct((N_GROUPS, gw, gw), F32),
                   jax.ShapeDtypeStruct((ni * 8, d), F32), jax.ShapeDtypeStruct((ni * 8, d), F32)],
        compiler_params=_params("parallel", "arbitrary"))(o, zpg, zpg, zpg, zpg, zpg, zpg, nw, pw, ps, dmix, dmix)


def _adamw(w, g, m, v, *, name):
    r, c = w.shape
    tr = _tile(r, max(8, (1 << 19) // c // 8 * 8), 8)

    def body(w_ref, g_ref, m_ref, v_ref, d_ref, mo_ref, vo_ref):
        gv = g_ref[...]
        mn = ADAM_B1 * m_ref[...] + (1.0 - ADAM_B1) * gv
        vn = ADAM_B2 * v_ref[...] + (1.0 - ADAM_B2) * (gv * gv)
        m_hat = mn / (1.0 - ADAM_B1 ** ADAM_STEP)
        v_hat = vn / (1.0 - ADAM_B2 ** ADAM_STEP)
        d_ref[...] = -ADAM_LR * (m_hat / (jnp.sqrt(v_hat) + ADAM_EPS) + ADAM_WD * w_ref[...])
        mo_ref[...] = mn
        vo_ref[...] = vn

    blk = pl.BlockSpec((tr, c), lambda i: (i, 0))
    return pl.pallas_call(
        body, name=name, grid=(r // tr,), in_specs=[blk] * 4, out_specs=[blk] * 3,
        out_shape=[jax.ShapeDtypeStruct((r, c), F32)] * 3, compiler_params=_params("parallel"))(w, g, m, v)


def _sum_parts(x, *, slot=None, name):
    p, r, c = x.shape
    tc = 128 if c % 128 == 0 else c
    tr = _tile(r, max(16, SUM_BLOCK_BYTES // (p * tc * x.dtype.itemsize)), 16)

    def body(*refs):
        x_ref, o_ref = refs[-2:]
        acc = x_ref[0].astype(F32)
        for i in range(1, p):
            acc = acc + x_ref[i].astype(F32)
        o_ref[...] = acc

    if slot is None:
        return pl.pallas_call(
            body, name=name, grid=(r // tr, c // tc),
            in_specs=[pl.BlockSpec((p, tr, tc), lambda i, j: (0, i, j))],
            out_specs=pl.BlockSpec((tr, tc), lambda i, j: (i, j)),
            out_shape=jax.ShapeDtypeStruct((r, c), F32), compiler_params=_params("parallel", "parallel"))(x)
    return pl.pallas_call(
        body, name=name,
        grid_spec=pltpu.PrefetchScalarGridSpec(
            num_scalar_prefetch=1, grid=(r // tr, c // tc),
            in_specs=[pl.BlockSpec((p, tr, tc), lambda i, j, s: (0, i, j))],
            out_specs=pl.BlockSpec((None, tr, tc), lambda i, j, s: (s[0], i, j))),
        out_shape=jax.ShapeDtypeStruct((2, r, c), F32), compiler_params=_params("parallel", "parallel"))(slot, x)


_HBM = pl.BlockSpec(memory_space=pltpu.HBM)


def _place():
    return lax.axis_index("x"), lax.axis_index("y"), lax.axis_index("c")


def _allgather(x_shard, *, name):
    def body(x_ref, out_ref, send_sems, recv_sems, local_sem):
        _allgather_step(x_ref, out_ref, send_sems, recv_sems, local_sem, start=True, finish=True)

    return pl.pallas_call(
        body, name=name, out_shape=_allgather_shape(x_shard), in_specs=[_HBM], out_specs=_HBM,
        scratch_shapes=_EXCHANGE_SEMS)(x_shard)


_EXCHANGE_SEMS = [pltpu.SemaphoreType.DMA((7,)), pltpu.SemaphoreType.DMA((7,)), pltpu.SemaphoreType.DMA]


def _allgather_shape(x_shard):
    return jax.ShapeDtypeStruct((8 * x_shard.shape[0], x_shard.shape[1]), x_shard.dtype)


def _allgather_step(x_ref, out_ref, send_sems, recv_sems, local_sem, *, start, finish):
    m_per = x_ref.shape[0]
    x, y, c = _place()
    me, sibling = (x, y, c), (x, y, 1 - c)
    chips = [(1 - x, y), (x, 1 - y), (1 - x, 1 - y)]

    def rows(px, py, pc):
        return out_ref.at[pl.ds((4 * px + 2 * py + pc) * m_per, m_per), :]

    def copy(k, block, to, src=None):
        return pltpu.make_async_remote_copy(
            src_ref=rows(*block) if src is None else src, dst_ref=rows(*block),
            send_sem=send_sems.at[k], recv_sem=recv_sems.at[k], device_id=to, device_id_type=MESH)

    mine = pltpu.make_async_copy(x_ref, rows(*me), local_sem)
    first = [copy(0, me, sibling, src=x_ref)]
    first += [copy(1 + j, me, (*chip, c), src=x_ref) for j, chip in enumerate(chips)]
    if start:
        mine.start()
        for cp in first:
            cp.start()
    if finish:
        passed = [copy(4 + j, (*chip, c), sibling) for j, chip in enumerate(chips)]
        for j, chip in enumerate(chips):
            copy(1 + j, (*chip, c), me).wait_recv()
            passed[j].start()
        copy(0, sibling, me).wait_recv()
        for j, chip in enumerate(chips):
            copy(4 + j, (*chip, 1 - c), me).wait_recv()
        for cp in first + passed:
            cp.wait_send()
        mine.wait()


def _all_to_all(parts, *, name):
    def body(g_ref, out_ref, send_sems, recv_sems, local_sem):
        _all_to_all_step(g_ref, out_ref, send_sems, recv_sems, local_sem, start=True, finish=True)

    return pl.pallas_call(
        body, name=name, out_shape=jax.ShapeDtypeStruct(parts.shape, parts.dtype), in_specs=[_HBM], out_specs=_HBM,
        scratch_shapes=_EXCHANGE_SEMS)(parts)


def _all_to_all_step(g_ref, out_ref, send_sems, recv_sems, local_sem, *, start, finish):
    x, y, c = _place()
    me = 4 * x + 2 * y + c
    mine = pltpu.make_async_copy(g_ref.at[me], out_ref.at[me], local_sem)
    if start:
        mine.start()
    sends, peers = [], []
    for k in range(1, 8):
        px = 1 - x if k & 4 else x
        py = 1 - y if k & 2 else y
        pc = 1 - c if k & 1 else c
        peer = 4 * px + 2 * py + pc
        cp = pltpu.make_async_remote_copy(
            src_ref=g_ref.at[peer], dst_ref=out_ref.at[me], send_sem=send_sems.at[k - 1],
            recv_sem=recv_sems.at[k - 1], device_id=(px, py, pc), device_id_type=MESH)
        if start:
            cp.start()
        sends.append(cp)
        peers.append((peer, (px, py, pc)))
    if finish:
        for k, (peer, pid) in enumerate(peers):
            pltpu.make_async_remote_copy(
                src_ref=g_ref.at[peer], dst_ref=out_ref.at[peer], send_sem=send_sems.at[k],
                recv_sem=recv_sems.at[k], device_id=pid, device_id_type=MESH).wait_recv()
        for cp in sends:
            cp.wait_send()
        mine.wait()


def _share_halves(both, *, name):
    _, r, _ = both.shape
    n_split = 1
    while both.size // 2 * both.dtype.itemsize > n_split * MAX_COPY_BYTES and r % (2 * n_split * 16) == 0:
        n_split *= 2
    rs = r // n_split

    def body(in_ref, out_ref, send_sems, recv_sems):
        x, y, c = _place()

        def copy(k, slot):
            rows = pl.ds(k * rs, rs)
            return pltpu.make_async_remote_copy(
                src_ref=in_ref.at[slot, rows], dst_ref=out_ref.at[slot, rows], send_sem=send_sems.at[k],
                recv_sem=recv_sems.at[k], device_id=(x, y, 1 - c), device_id_type=MESH)

        sends = [copy(k, c) for k in range(n_split)]
        for cp in sends:
            cp.start()
        for k in range(n_split):
            copy(k, 1 - c).wait_recv()
        for cp in sends:
            cp.wait_send()

    return pl.pallas_call(
        body, name=name, out_shape=jax.ShapeDtypeStruct(both.shape, both.dtype), in_specs=[_HBM], out_specs=_HBM,
        input_output_aliases={0: 0},
        scratch_shapes=[pltpu.SemaphoreType.DMA((n_split,)), pltpu.SemaphoreType.DMA((n_split,))])(both)


def _piece_rows(shape):
    n = math.prod(shape)
    if n % 128 == 0:
        return n // 128, 128
    assert shape[-1] <= 128, shape
    return n // shape[-1], shape[-1]


def _pack_small(arrs, row_multiple):
    pieces = []
    for a in arrs:
        rows, lanes = _piece_rows(a.shape)
        t = a.astype(F32).reshape(rows, lanes)
        pieces.append(jnp.pad(t, ((0, -rows % 8), (0, 128 - lanes))))
    buf = jnp.concatenate(pieces, axis=0)
    return jnp.pad(buf, ((0, -buf.shape[0] % row_multiple), (0, 0)))


def _unpack_small(buf, shapes):
    out, off = [], 0
    for shp in shapes:
        rows, lanes = _piece_rows(shp)
        out.append(buf[off:off + rows, :lanes].reshape(shp))
        off += rows + (-rows % 8)
    return out


def _w_in_grad_parts(g_qkv, g_zpg, g_ba, n_heads, *, name):
    d = g_qkv.shape[0]
    cw = (g_qkv.shape[1] + 4 * d + 2 * n_heads) // N_CHIPS
    tr = _tile(d // 2, 128, 16)
    per_half = d // 2 // tr

    def body(a_ref, z_ref, p_ref, ga_ref, gb_ref, ba_ref, o_ref):
        full = jnp.concatenate([a_ref[...], z_ref[...], ba_ref[...][:, :2 * n_heads], p_ref[...], ga_ref[...],
                                gb_ref[...]], axis=1)
        for j in range(N_CHIPS):
            o_ref[j] = full[:, cw * j:cw * (j + 1)]

    row = lambda c: pl.BlockSpec((tr, c), lambda i: (i, 0))
    out = pl.pallas_call(
        body, name=name, grid=(d // tr,),
        in_specs=[row(g_qkv.shape[1]), row(d), row(d), row(d), row(d), row(128)],
        out_specs=pl.BlockSpec((N_CHIPS, None, tr, cw), lambda i: (0, i // per_half, i % per_half, 0)),
        out_shape=jax.ShapeDtypeStruct((N_CHIPS, 2, d // 2, cw), BF16),
        compiler_params=_params("parallel"))(g_qkv, *g_zpg, g_ba)
    return out.reshape(8, d // 2, cw)


def _layer_fwd(x, p, n_heads, gather=()):
    h = _rmsnorm_fwd(x, p["norm_mix_w"], name="norm_mix_fwd")
    pq = _matmul(h, p["w_qkv"], name="proj_qkv")
    zpg = _matmul(h, p["w_zpg"], name="proj_zpg")
    ba = _matmul(h, p["w_ba"], name="proj_ba")
    qkv3 = _qkv_fwd(pq, p["conv_qkv_w"], n_heads, name="qkv_fwd")
    bg = _gates_fwd(ba, p["a_log"], p["dt_bias"], n_heads, name="gates_fwd")
    o, st, ti, gathered = _gdn_fwd(qkv3, bg, gather, name="gdn_fwd")
    mixed = _mix_fwd(o, zpg, p["gdn_nw"], p["pool_w"], p["pool_scale"], name="mix_fwd")
    x1 = _matmul(mixed, p["w_out"], add=x, name="out_proj")
    h2 = _rmsnorm_fwd(x1, p["norm_ffn_w"], name="norm_ffn_fwd")
    gu = _matmul(h2, p["w_up"], name="up_proj")
    act = _ffn_mid_fwd(gu, p["conv_ffn_w"], p["conv_ffn_b"], name="ffn_mid_fwd")
    x2 = _matmul(act, p["w_down"], add=x1, name="down_proj")
    return x2, (x, h, pq, zpg, ba, o, st, ti, mixed, x1, h2, gu, act), gathered


def _layer_bwd(dx2, p, saved, n_heads, scatter=()):
    x, h, pq, zpg, ba, o, st, ti, mixed, x1, h2, gu, act = saved
    d = x.shape[1]
    f = act.shape[1]
    dact = _matmul(dx2, p["w_down"], tb=True, name="d_act")
    g_down = _matmul(act, dx2, ta=True, out_dtype=BF16, name="g_w_down")
    dgate, dup, dcw_p, dcb_p = _ffn_mid_bwd(gu, p["conv_ffn_w"], p["conv_ffn_b"], dact, name="ffn_mid_bwd")
    dh2 = _matmul(dgate, p["w_up"], tb=True, b_k0=0, name="d_h2_gate")
    dh2 = _matmul(dup, p["w_up"], tb=True, b_k0=f, add=dh2, name="d_h2_up")
    g_up = jnp.concatenate([_matmul(h2, dgate, ta=True, out_dtype=BF16, name="g_w_up_gate"),
                            _matmul(h2, dup, ta=True, out_dtype=BF16, name="g_w_up_up")], axis=1)
    dx1, dnf_p = _rmsnorm_bwd(x1, p["norm_ffn_w"], dh2, dx2, name="norm_ffn_bwd")
    dmix = _matmul(dx1, p["w_out"], tb=True, name="d_mixed")
    g_out = _matmul(mixed, dx1, ta=True, out_dtype=BF16, name="g_w_out")
    d_o, d4, g_pool, dnw_p, dps_p = _mix_bwd(o, zpg, p["gdn_nw"], p["pool_w"], p["pool_scale"], dmix, name="mix_bwd")
    qkv3 = _qkv_fwd(pq, p["conv_qkv_w"], n_heads, name="qkv_fwd")
    bg = _gates_fwd(ba, p["a_log"], p["dt_bias"], n_heads, name="gates_fwd")
    dqkv3, dbg, scattered = _gdn_bwd(qkv3, bg, st, ti, d_o, scatter, name="gdn_bwd")
    dpq, dcq_p = _qkv_bwd(pq, p["conv_qkv_w"], dqkv3, n_heads, name="qkv_bwd")
    dba, dgate_p = _gates_bwd(ba, p["a_log"], p["dt_bias"], dbg, n_heads, name="gates_bwd")
    dh = _matmul(dpq, p["w_qkv"], tb=True, name="d_h_qkv")
    for seg in range(4):
        dh = _matmul(d4, p["w_zpg"], tb=True, a_part=seg, b_k0=seg * d, add=dh, name="d_h_zpg")
    dh = _matmul(dba, p["w_ba"], tb=True, add=dh, name="d_h_ba")
    g_qkv = _matmul(h, dpq, ta=True, out_dtype=BF16, name="g_w_qkv")
    g_zpg = [_matmul(h, d4, ta=True, b_part=seg, out_dtype=BF16, name="g_w_zpg") for seg in range(4)]
    g_ba = _matmul(h, dba, ta=True, out_dtype=BF16, name="g_w_ba")
    dx, dnm_p = _rmsnorm_bwd(x, p["norm_mix_w"], dh, dx1, name="norm_mix_bwd")
    g_in = _w_in_grad_parts(g_qkv, g_zpg, g_ba, n_heads, name="g_w_in_parts")
    rows = lambda t: jnp.sum(t, axis=0)
    ni8, c12 = dcq_p.shape
    nj = dcw_p.shape[1] // (3 * _tile(f, 512))
    small = {
        "norm_mix_w": rows(dnm_p),
        "conv_qkv_w": rows(dcq_p).reshape(c12 // (4 * HEAD_DIM), 4, HEAD_DIM).transpose(1, 0, 2).reshape(4, c12 // 4),
        "a_log": rows(dgate_p)[:n_heads],
        "dt_bias": rows(dgate_p)[n_heads:2 * n_heads],
        "gdn_norm_w": jnp.sum(rows(dnw_p).reshape(d // HEAD_DIM, HEAD_DIM), axis=0),
        "pool_scale": rows(dps_p),
        "norm_ffn_w": rows(dnf_p),
        "conv_ffn_w": rows(dcw_p).reshape(nj, 3, f // nj).transpose(1, 0, 2).reshape(3, f),
        "conv_ffn_b": rows(dcb_p),
    }
    big = {"w_in": g_in, "pool_w": g_pool, "w_out": g_out, "w_up": g_up, "w_down": g_down}
    return dx, big, small, scattered


BIG = ("w_in", "pool_w", "w_out", "w_up", "w_down")
SMALL = ("norm_mix_w", "conv_qkv_w", "a_log", "dt_bias", "gdn_norm_w", "pool_scale", "norm_ffn_w", "conv_ffn_w",
         "conv_ffn_b", "norm_final_w")
WEIGHTS = ("norm_mix_w", "w_in", "conv_qkv_w", "a_log", "dt_bias", "gdn_norm_w", "pool_w", "pool_scale", "w_out",
           "norm_ffn_w", "w_up", "conv_ffn_w", "conv_ffn_b", "w_down", "norm_final_w")
N_CHIPS = 4
RIDERS = ("w_in", "pool_w", "w_out")


def _my_half(local, cc):
    m = local.shape[0] // 2
    return lax.dynamic_slice_in_dim(local, cc * m, m, axis=0)


def _weight_halves(w, l, cc):
    pw = w["pool_w"][l].astype(BF16)
    local = dict(w_in=w["w_in"][l].astype(BF16), pool_w=pw.reshape(-1, pw.shape[-1]),
                 w_out=w["w_out"][l].astype(BF16), w_up=w["w_up"][l].astype(BF16), w_down=w["w_down"][l].astype(BF16))
    return [_my_half(local[n], cc) for n in BIG]


def _full_weights(w, gathered):
    g_in, g_pool, g_out, g_up, g_down = gathered
    d = w["w_in"].shape[1]
    g, r, c = w["pool_w"].shape[1:]
    wi = g_in.reshape(N_CHIPS, d, -1)
    return dict(
        w_in=jnp.concatenate([wi[j] for j in range(N_CHIPS)], axis=1),
        pool_w=g_pool.reshape(N_CHIPS, g, r, c).transpose(1, 0, 2, 3).reshape(g, N_CHIPS * r, c),
        w_out=g_out, w_up=g_up.reshape(N_CHIPS, d, -1).transpose(1, 0, 2).reshape(d, -1), w_down=g_down)


def _grad_parts(name, g):
    if name in ("w_in", "w_up"):
        r, c = g.shape
        return g.reshape(2, r // 2, N_CHIPS, c // N_CHIPS).transpose(2, 0, 1, 3).reshape(8, r // 2, c // N_CHIPS)
    if name == "pool_w":
        ng, r, c = g.shape
        t = g.reshape(2, ng // 2, N_CHIPS, r // N_CHIPS, c).transpose(2, 0, 1, 3, 4)
        return t.reshape(8, (ng // 2) * (r // N_CHIPS), c)
    r, c = g.shape
    return g.reshape(8, r // 8, c)


def _layer_parts(big):
    return {n: big[n] if n == "w_in" else _grad_parts(n, big[n].astype(BF16)) for n in BIG}


def _finish_reduce(name, got, shard_shape, slot):
    both = _share_halves(_sum_parts(got, slot=slot, name="sum_" + name), name="share_" + name)
    return both.reshape(shard_shape)


def kernel(x, norm_mix_w, w_in, conv_qkv_w, a_log, dt_bias, gdn_norm_w, pool_w, pool_scale, w_out, norm_ffn_w, w_up, conv_ffn_w, conv_ffn_b, w_down, norm_final_w, loss_target, m_norm_mix_w, m_w_in, m_conv_qkv_w, m_a_log, m_dt_bias, m_gdn_norm_w, m_pool_w, m_pool_scale, m_w_out, m_norm_ffn_w, m_w_up, m_conv_ffn_w, m_conv_ffn_b, m_w_down, m_norm_final_w, v_norm_mix_w, v_w_in, v_conv_qkv_w, v_a_log, v_dt_bias, v_gdn_norm_w, v_pool_w, v_pool_scale, v_w_out, v_norm_ffn_w, v_w_up, v_conv_ffn_w, v_conv_ffn_b, v_w_down, v_norm_final_w):
    w = dict(norm_mix_w=norm_mix_w, w_in=w_in, conv_qkv_w=conv_qkv_w, a_log=a_log, dt_bias=dt_bias,
             gdn_norm_w=gdn_norm_w, pool_w=pool_w, pool_scale=pool_scale, w_out=w_out, norm_ffn_w=norm_ffn_w,
             w_up=w_up, conv_ffn_w=conv_ffn_w, conv_ffn_b=conv_ffn_b, w_down=w_down, norm_final_w=norm_final_w)
    m = dict(norm_mix_w=m_norm_mix_w, w_in=m_w_in, conv_qkv_w=m_conv_qkv_w, a_log=m_a_log, dt_bias=m_dt_bias,
             gdn_norm_w=m_gdn_norm_w, pool_w=m_pool_w, pool_scale=m_pool_scale, w_out=m_w_out,
             norm_ffn_w=m_norm_ffn_w, w_up=m_w_up, conv_ffn_w=m_conv_ffn_w, conv_ffn_b=m_conv_ffn_b,
             w_down=m_w_down, norm_final_w=m_norm_final_w)
    v = dict(norm_mix_w=v_norm_mix_w, w_in=v_w_in, conv_qkv_w=v_conv_qkv_w, a_log=v_a_log, dt_bias=v_dt_bias,
             gdn_norm_w=v_gdn_norm_w, pool_w=v_pool_w, pool_scale=v_pool_scale, w_out=v_w_out,
             norm_ffn_w=v_norm_ffn_w, w_up=v_w_up, conv_ffn_w=v_conv_ffn_w, conv_ffn_b=v_conv_ffn_b,
             w_down=v_w_down, norm_final_w=v_norm_final_w)
    depth, n_heads = a_log.shape
    d = x.shape[-1]
    dl = n_heads * HEAD_DIM
    assert dl == d and gdn_norm_w.shape[1] == HEAD_DIM
    cx, cy, cc = _place()
    chip = 2 * cx + cy

    conv_packed = _pack_small([conv_qkv_w, conv_ffn_w], 16)
    conv_all = _allgather(conv_packed, name="gather_conv").reshape(N_CHIPS, 2, -1, 128)[:, 0]
    conv_j = [_unpack_small(conv_all[j], [conv_qkv_w.shape, conv_ffn_w.shape]) for j in range(N_CHIPS)]
    conv_q = jnp.concatenate([t[0] for t in conv_j], axis=-1)
    conv_f = jnp.concatenate([t[1] for t in conv_j], axis=-1)

    def layer_params(l, full):
        wi = full["w_in"]
        return dict(
            norm_mix_w=norm_mix_w[l][None], norm_ffn_w=norm_ffn_w[l][None],
            w_qkv=wi[:, :3 * dl],
            w_zpg=jnp.concatenate([wi[:, 3 * dl:4 * dl], wi[:, 4 * dl + 2 * n_heads:]], axis=1),
            w_ba=jnp.concatenate([wi[:, 4 * dl:4 * dl + 2 * n_heads], jnp.zeros((d, 128 - 2 * n_heads), BF16)], axis=1),
            conv_qkv_w=conv_q[l], a_log=a_log[l], dt_bias=dt_bias[l],
            gdn_nw=jnp.tile(gdn_norm_w[l], d // HEAD_DIM)[None], pool_w=full["pool_w"], pool_scale=pool_scale[l][None],
            w_out=full["w_out"], w_up=full["w_up"], conv_ffn_w=conv_f[l], conv_ffn_b=conv_ffn_b[l][None],
            w_down=full["w_down"])

    xs = x[0]
    saved, params = [], []
    gathered = [_allgather(t, name="gather_" + n) for n, t in zip(BIG, _weight_halves(w, 0, cc))]
    for l in range(depth):
        params.append(layer_params(l, _full_weights(w, gathered)))
        nxt = _weight_halves(w, l + 1, cc) if l + 1 < depth else ()
        xs, sv, gathered = _layer_fwd(xs, params[l], n_heads, nxt)
        saved.append(sv)
    dx, dnf_p, loss_p = _final_loss(xs, norm_final_w[None], loss_target[0], name="final_loss")
    loss = lax.psum(jnp.sum(loss_p) * (0.5 / d), ("x", "y", "c"))

    slot = jnp.reshape(cc, (1,)).astype(jnp.int32)
    small_g = [None] * depth
    reduced = [dict() for _ in range(depth)]
    riding = {}
    for l in reversed(range(depth)):
        dx, big, small_g[l], got = _layer_bwd(dx, params[l], saved[l], n_heads, [riding[n] for n in RIDERS if riding])
        for n, t in zip(RIDERS if riding else (), got):
            reduced[l + 1][n] = _finish_reduce(n, t, w[n].shape[1:], slot)
        parts = _layer_parts(big)
        riding = {n: parts[n] for n in RIDERS} if l > 0 else {}
        for n in BIG:
            if n not in riding:
                got_n = _all_to_all(parts[n], name="scatter_" + n)
                reduced[l][n] = _finish_reduce(n, got_n, w[n].shape[1:], slot)
    grads = {n: jnp.stack([reduced[l][n] for l in range(depth)]) for n in BIG}

    small_shapes = {n: ((depth,) + small_g[0][n].shape if n != "norm_final_w" else (d,)) for n in SMALL}
    small_local = [jnp.stack([small_g[l][n] for l in range(depth)]) for n in SMALL[:-1]] + [jnp.sum(dnf_p, axis=0)]
    sp = _pack_small(small_local, 512)
    sg = _allgather(sp, name="gather_small").reshape(8, sp.shape[0], 128)
    small_sum = _unpack_small(_sum_parts(sg, name="sum_small"), [small_shapes[n] for n in SMALL])
    for n, g in zip(SMALL, small_sum):
        if n in ("conv_qkv_w", "conv_ffn_w"):
            cols = w[n].shape[-1]
            g = lax.dynamic_slice_in_dim(g, chip * cols, cols, axis=2)
        grads[n] = g

    delta, new_m, new_v = {}, {}, {}
    for n in BIG:
        shp = w[n].shape
        r2 = lambda t: t.reshape(-1, shp[-1])
        dd, mm, vv = _adamw(r2(w[n]), r2(grads[n]), r2(m[n]), r2(v[n]), name="adamw_" + n)
        delta[n], new_m[n], new_v[n] = dd.reshape(shp), mm.reshape(shp), vv.reshape(shp)
    pk = lambda src: _pack_small([src[n] for n in SMALL], 8)
    outs = _adamw(pk(w), pk(grads), pk(m), pk(v), name="adamw_small")
    for dst, buf in zip((delta, new_m, new_v), outs):
        for n, t in zip(SMALL, _unpack_small(buf, [w[n].shape for n in SMALL])):
            dst[n] = t

    return (loss, dx[None], *[grads[n] for n in WEIGHTS], *[delta[n] for n in WEIGHTS],
            *[new_m[n] for n in WEIGHTS], *[new_v[n] for n in WEIGHTS])
```

```python
import functools
import math

import jax
import jax.numpy as jnp
from jax import lax
from jax.experimental import pallas as pl
from jax.experimental.pallas import tpu as pltpu

F32 = jnp.float32
BF16 = jnp.bfloat16
EPS = 1e-6
CHUNK = 64
HEAD_DIM = 128
POOL_WINDOWS = (2, 4, 8, 16)
HALO = 16
ADAM_LR, ADAM_B1, ADAM_B2, ADAM_EPS, ADAM_WD, ADAM_STEP = 0.001, 0.9, 0.999, 1e-08, 0.01, 10
V7X_VMEM_LIMIT = 56 * 1024 * 1024
SUM_BLOCK_BYTES = 6 * 1024 * 1024
MAX_COPY_BYTES = 8 * 1024 * 1024
MATMUL_TILE_BYTES = 10 * 1024 * 1024
MESH = pl.DeviceIdType.MESH


def _tile(n, cap, align=128):
    if n <= cap:
        return n
    t = (cap // align) * align
    while t >= align:
        if n % t == 0:
            return t
        t -= align
    return n


def _params(*sem):
    return pltpu.CompilerParams(dimension_semantics=sem, vmem_limit_bytes=V7X_VMEM_LIMIT)


def _sigmoid(x):
    return 0.5 * (1.0 + jnp.tanh(0.5 * x))


def _down(x, j):
    return pltpu.roll(x, j, 0)


def _up(x, j):
    return pltpu.roll(x, x.shape[0] - j, 0)


def _fold8(x):
    n, c = x.shape
    return jnp.sum(x.reshape(n // 8, 8, c), axis=0)


def _matmul(a, b, *, ta=False, tb=False, add=None, out_dtype=F32, b_k0=0, b_n0=0, n=None, a_part=None, b_part=None, name):
    a2, b2 = a.shape[-2:], b.shape[-2:]
    m, k = (a2[1], a2[0]) if ta else a2
    if n is None:
        n = b2[0] if tb else b2[1]
    tm, tn = _tile(m, 1024), _tile(n, 1024)
    per_k = tm * a.dtype.itemsize + tn * b.dtype.itemsize
    tk = _tile(k, max(128, MATMUL_TILE_BYTES // per_k // 128 * 128))
    nk = k // tk
    assert b_k0 % tk == 0 and b_n0 % tn == 0, (b_k0, b_n0, tk, tn)
    ko, no = b_k0 // tk, b_n0 // tn

    def spec(shape, index, part):
        if part is None:
            return pl.BlockSpec(shape, index)
        return pl.BlockSpec((None,) + shape, lambda i, j, kk: (part,) + index(i, j, kk))

    a_spec = spec((tk, tm), lambda i, j, kk: (kk, i), a_part) if ta else spec((tm, tk), lambda i, j, kk: (i, kk), a_part)
    b_spec = (spec((tn, tk), lambda i, j, kk: (j + no, kk + ko), b_part) if tb
              else spec((tk, tn), lambda i, j, kk: (kk + ko, j + no), b_part))
    o_spec = pl.BlockSpec((tm, tn), lambda i, j, kk: (i, j))
    dims = (((0 if ta else 1,), (1 if tb else 0,)), ((), ()))
    has_add = add is not None

    def body(*refs):
        a_ref, b_ref = refs[:2]
        add_ref = refs[2] if has_add else None
        o_ref = refs[3 if has_add else 2]
        part = lax.dot_general(a_ref[...].astype(BF16), b_ref[...].astype(BF16), dims, preferred_element_type=F32)

        def finish(r):
            if has_add:
                r = r + add_ref[...]
            o_ref[...] = r.astype(out_dtype)

        if nk == 1:
            finish(part)
            return
        acc = refs[-1]
        kk = pl.program_id(2)

        @pl.when(kk == 0)
        def _():
            acc[...] = part

        @pl.when(kk > 0)
        def _():
            acc[...] += part

        @pl.when(kk == nk - 1)
        def _():
            finish(acc[...])

    ins = [a, b] + ([add] if has_add else [])
    specs = [a_spec, b_spec] + ([o_spec] if has_add else [])
    return pl.pallas_call(
        body, name=name, grid=(m // tm, n // tn, nk), in_specs=specs, out_specs=o_spec,
        out_shape=jax.ShapeDtypeStruct((m, n), out_dtype),
        scratch_shapes=[pltpu.VMEM((tm, tn), F32)] if nk > 1 else [],
        compiler_params=_params("parallel", "parallel", "arbitrary"))(*ins)


def _rmsnorm_fwd(x, w, *, name):
    s, d = x.shape
    ts = _tile(s, 512, 8)

    def body(x_ref, w_ref, o_ref):
        xv = x_ref[...]
        r = lax.rsqrt(jnp.mean(xv * xv, axis=-1, keepdims=True) + EPS)
        o_ref[...] = (xv * r * w_ref[...]).astype(BF16)

    return pl.pallas_call(
        body, name=name, grid=(s // ts,),
        in_specs=[pl.BlockSpec((ts, d), lambda i: (i, 0)), pl.BlockSpec((1, d), lambda i: (0, 0))],
        out_specs=pl.BlockSpec((ts, d), lambda i: (i, 0)),
        out_shape=jax.ShapeDtypeStruct((s, d), BF16), compiler_params=_params("parallel"))(x, w)


def _rmsnorm_bwd(x, w, dh, dres, *, name):
    s, d = x.shape
    ts = _tile(s, 512, 8)

    def body(x_ref, w_ref, dh_ref, dres_ref, dx_ref, dw_ref):
        xv = x_ref[...]
        r = lax.rsqrt(jnp.mean(xv * xv, axis=-1, keepdims=True) + EPS)
        xh = xv * r
        dhv = dh_ref[...]
        dxh = dhv * w_ref[...]
        dx_ref[...] = dres_ref[...] + r * (dxh - xh * jnp.mean(dxh * xh, axis=-1, keepdims=True))

        @pl.when(pl.program_id(0) == 0)
        def _():
            dw_ref[...] = jnp.zeros_like(dw_ref)

        dw_ref[...] += _fold8(dhv * xh)

    row = pl.BlockSpec((ts, d), lambda i: (i, 0))
    return pl.pallas_call(
        body, name=name, grid=(s // ts,),
        in_specs=[row, pl.BlockSpec((1, d), lambda i: (0, 0)), row, row],
        out_specs=[row, pl.BlockSpec((8, d), lambda i: (0, 0))],
        out_shape=[jax.ShapeDtypeStruct((s, d), F32), jax.ShapeDtypeStruct((8, d), F32)],
        compiler_params=_params("arbitrary"))(x, w, dh, dres)


def _final_loss(x, w, target, *, name):
    s, d = x.shape
    ts = _tile(s, 512, 8)

    def body(x_ref, w_ref, t_ref, dx_ref, dw_ref, loss_ref):
        xv = x_ref[...]
        r = lax.rsqrt(jnp.mean(xv * xv, axis=-1, keepdims=True) + EPS)
        xh = xv * r
        wv = w_ref[...]
        err = xh * wv - t_ref[...]
        dy = err * (1.0 / d)
        dxh = dy * wv
        dx_ref[...] = r * (dxh - xh * jnp.mean(dxh * xh, axis=-1, keepdims=True))

        @pl.when(pl.program_id(0) == 0)
        def _():
            dw_ref[...] = jnp.zeros_like(dw_ref)
            loss_ref[...] = jnp.zeros_like(loss_ref)

        dw_ref[...] += _fold8(dy * xh)
        e2 = _fold8(err * err)
        part = e2[:, 0:128]
        for j in range(1, d // 128):
            part = part + e2[:, j * 128:(j + 1) * 128]
        loss_ref[...] += part

    row = pl.BlockSpec((ts, d), lambda i: (i, 0))
    return pl.pallas_call(
        body, name=name, grid=(s // ts,),
        in_specs=[row, pl.BlockSpec((1, d), lambda i: (0, 0)), row],
        out_specs=[row, pl.BlockSpec((8, d), lambda i: (0, 0)), pl.BlockSpec((8, 128), lambda i: (0, 0))],
        out_shape=[jax.ShapeDtypeStruct((s, d), F32), jax.ShapeDtypeStruct((8, d), F32),
                   jax.ShapeDtypeStruct((8, 128), F32)],
        compiler_params=_params("arbitrary"))(x, w, target)


def _prev_spec(ts, tc, col):
    return pl.BlockSpec((HALO, tc), lambda i, j: (jnp.maximum(i * (ts // HALO) - 1, 0), col(j)))


def _next_spec(ts, tc, col, n_tiles):
    return pl.BlockSpec((HALO, tc), lambda i, j: (jnp.minimum((i + 1) * (ts // HALO), n_tiles * (ts // HALO) - 1), col(j)))


def _with_prev(prev, cur, first):
    return jnp.concatenate([jnp.where(first, 0.0, prev), cur], axis=0)


def _with_next(cur, nxt, last):
    return jnp.concatenate([cur, jnp.where(last, 0.0, nxt)], axis=0)


def _gelu(x):
    return 0.5 * x * (1.0 + lax.erf(x * (1.0 / math.sqrt(2.0))))


def _gelu_grad(x):
    return 0.5 * (1.0 + lax.erf(x * (1.0 / math.sqrt(2.0)))) + x * jnp.exp(-0.5 * x * x) * (1.0 / math.sqrt(2.0 * math.pi))


def _ffn_conv(prev, cur, w, first):
    xx = _with_prev(prev, cur, first)
    return w[2:3] * cur + w[1:2] * _down(xx, 1)[HALO:] + w[0:1] * _down(xx, 2)[HALO:]


def _ffn_mid_fwd(gu, cw, cb, *, name):
    s, f2 = gu.shape
    f = f2 // 2
    ts, tc = _tile(s, 512, HALO), _tile(f, 512)
    nj = f // tc

    def body(g_ref, gp_ref, u_ref, w_ref, b_ref, o_ref):
        first = pl.program_id(0) == 0
        gc = _ffn_conv(gp_ref[...], g_ref[...], w_ref[...], first) + b_ref[...]
        o_ref[...] = (_gelu(gc) * u_ref[...]).astype(BF16)

    return pl.pallas_call(
        body, name=name, grid=(s // ts, nj),
        in_specs=[pl.BlockSpec((ts, tc), lambda i, j: (i, j)), _prev_spec(ts, tc, lambda j: j),
                  pl.BlockSpec((ts, tc), lambda i, j: (i, j + nj)),
                  pl.BlockSpec((3, tc), lambda i, j: (0, j)), pl.BlockSpec((1, tc), lambda i, j: (0, j))],
        out_specs=pl.BlockSpec((ts, tc), lambda i, j: (i, j)),
        out_shape=jax.ShapeDtypeStruct((s, f), BF16), compiler_params=_params("parallel", "parallel"))(gu, gu, gu, cw, cb)


def _ffn_mid_bwd(gu, cw, cb, dact, scatter=(), *, name):
    s, f2 = gu.shape
    f = f2 // 2
    ts, tc = _tile(s, 512, HALO), _tile(f, 512)
    nj, ni = f // tc, s // ts
    n_x = len(scatter)

    def body(*refs):
        g_ref, gp_ref, gn_ref, u_ref, un_ref, d_ref, dn_ref, w_ref, b_ref = refs[:9]
        dg_ref, du_ref, dw_ref, db_ref = refs[9 + n_x:13 + n_x]
        riders = (_all_to_all_step, refs[9:9 + n_x], refs[13 + n_x:13 + 2 * n_x], refs[13 + 2 * n_x:], (ni, nj))
        _ride(*riders, start=True)
        i = pl.program_id(0)
        first, last = i == 0, i == ni - 1
        w, b = w_ref[...], b_ref[...]
        g = g_ref[...]
        gx = jnp.concatenate([jnp.where(first, 0.0, gp_ref[...]), g, jnp.where(last, 0.0, gn_ref[...])], axis=0)
        g1, g2 = _down(gx, 1), _down(gx, 2)
        gc = (w[2:3] * gx + w[1:2] * g1 + w[0:1] * g2)[HALO:] + b
        ux = _with_next(u_ref[...], un_ref[...], last)
        dx = _with_next(d_ref[...], dn_ref[...], last)
        dgc = dx * ux * _gelu_grad(gc)
        du_ref[...] = (dx[:ts] * _gelu(gc[:ts])).astype(BF16)
        dg = w[2:3] * dgc + w[1:2] * _up(dgc, 1) + w[0:1] * _up(dgc, 2)
        dg_ref[...] = dg[:ts].astype(BF16)
        dgt = dgc[:ts]
        db_ref[...] = _fold8(dgt)
        dw_ref[:, 0:tc] = _fold8(dgt * g2[HALO:HALO + ts])
        dw_ref[:, tc:2 * tc] = _fold8(dgt * g1[HALO:HALO + ts])
        dw_ref[:, 2 * tc:3 * tc] = _fold8(dgt * g)
        _ride(*riders, start=False)

    cur = lambda off: pl.BlockSpec((ts, tc), lambda i, j: (i, j + off))
    outs = pl.pallas_call(
        body, name=name, grid=(ni, nj),
        in_specs=[cur(0), _prev_spec(ts, tc, lambda j: j), _next_spec(ts, tc, lambda j: j, ni),
                  cur(nj), _next_spec(ts, tc, lambda j: j + nj, ni),
                  cur(0), _next_spec(ts, tc, lambda j: j, ni),
                  pl.BlockSpec((3, tc), lambda i, j: (0, j)), pl.BlockSpec((1, tc), lambda i, j: (0, j))] + [_HBM] * n_x,
        out_specs=[cur(0), cur(0),
                   pl.BlockSpec((8, 3 * tc), lambda i, j: (i, j)), pl.BlockSpec((8, tc), lambda i, j: (i, j))] + [_HBM] * n_x,
        out_shape=[jax.ShapeDtypeStruct((s, f), BF16), jax.ShapeDtypeStruct((s, f), BF16),
                   jax.ShapeDtypeStruct((ni * 8, 3 * f), F32), jax.ShapeDtypeStruct((ni * 8, f), F32)]
        + [jax.ShapeDtypeStruct(t.shape, t.dtype) for t in scatter],
        scratch_shapes=_EXCHANGE_SEMS * n_x,
        compiler_params=_params(*(("arbitrary", "arbitrary") if n_x else ("parallel", "parallel"))))(
            gu, gu, gu, gu, gu, dact, dact, cw, cb, *scatter)
    return outs[0], outs[1], outs[2], outs[3], list(outs[4:])


def _qkv_fwd(pq, cw, n_heads, *, name):
    s, c3 = pq.shape
    ts = _tile(s, 512, HALO)

    def body(x_ref, xp_ref, w_ref, o_ref):
        j = pl.program_id(1)
        w = w_ref[...]
        x = x_ref[...]
        xx = _with_prev(xp_ref[...], x, pl.program_id(0) == 0)
        y = w[3:4] * x + w[2:3] * _down(xx, 1)[HALO:] + w[1:2] * _down(xx, 2)[HALO:] + w[0:1] * _down(xx, 3)[HALO:]
        c = y * _sigmoid(y)
        r = lax.rsqrt(jnp.sum(c * c, axis=-1, keepdims=True) + EPS)
        scale = jnp.where(j < n_heads, HEAD_DIM ** -0.5, 1.0)
        o_ref[...] = jnp.where(j < 2 * n_heads, c * (r * scale), c)

    return pl.pallas_call(
        body, name=name, grid=(s // ts, 3 * n_heads),
        in_specs=[pl.BlockSpec((ts, HEAD_DIM), lambda i, j: (i, j)), _prev_spec(ts, HEAD_DIM, lambda j: j),
                  pl.BlockSpec((4, HEAD_DIM), lambda i, j: (0, j))],
        out_specs=pl.BlockSpec((None, ts, HEAD_DIM), lambda i, j: (j // n_heads, i, j % n_heads)),
        out_shape=jax.ShapeDtypeStruct((3, s, c3 // 3), F32),
        compiler_params=_params("parallel", "parallel"))(pq, pq, cw)


def _qkv_bwd(pq, cw, dqkv3, n_heads, scatter=(), *, name):
    s, c3 = pq.shape
    ts = _tile(s, 512, HALO)
    ni = s // ts
    hd = HEAD_DIM

    n_x = len(scatter)

    def body(*refs):
        x_ref, xp_ref, xn_ref, w_ref, d_ref, dn_ref = refs[:6]
        dx_ref, dw_ref = refs[6 + n_x:8 + n_x]
        riders = (_all_to_all_step, refs[6:6 + n_x], refs[8 + n_x:8 + 2 * n_x], refs[8 + 2 * n_x:], (ni, 3 * n_heads))
        _ride(*riders, start=True)
        i, j = pl.program_id(0), pl.program_id(1)
        first, last = i == 0, i == ni - 1
        w = w_ref[...]
        x = x_ref[...]
        xx = jnp.concatenate([jnp.where(first, 0.0, xp_ref[...]), x, jnp.where(last, 0.0, xn_ref[...])], axis=0)
        x1, x2, x3 = _down(xx, 1), _down(xx, 2), _down(xx, 3)
        y = (w[3:4] * xx + w[2:3] * x1 + w[1:2] * x2 + w[0:1] * x3)[HALO:]
        sg = _sigmoid(y)
        c = y * sg
        dn = _with_next(d_ref[...], dn_ref[...], last)
        r = lax.rsqrt(jnp.sum(c * c, axis=-1, keepdims=True) + EPS)
        nrm = c * r
        dnn = dn * jnp.where(j < n_heads, hd ** -0.5, 1.0)
        dc = jnp.where(j < 2 * n_heads, r * (dnn - nrm * jnp.sum(dnn * nrm, axis=-1, keepdims=True)), dn)
        dy = dc * (sg * (1.0 + y * (1.0 - sg)))
        dx = w[3:4] * dy + w[2:3] * _up(dy, 1) + w[1:2] * _up(dy, 2) + w[0:1] * _up(dy, 3)
        dx_ref[...] = dx[:ts].astype(BF16)
        dyt = dy[:ts]
        dw_ref[:, 0:hd] = _fold8(dyt * x3[HALO:HALO + ts])
        dw_ref[:, hd:2 * hd] = _fold8(dyt * x2[HALO:HALO + ts])
        dw_ref[:, 2 * hd:3 * hd] = _fold8(dyt * x1[HALO:HALO + ts])
        dw_ref[:, 3 * hd:4 * hd] = _fold8(dyt * x)
        _ride(*riders, start=False)

    dspec = lambda rows, row_index: pl.BlockSpec(
        (None, rows, hd), lambda i, j: (j // n_heads, row_index(i), j % n_heads))
    outs = pl.pallas_call(
        body, name=name, grid=(ni, 3 * n_heads),
        in_specs=[pl.BlockSpec((ts, hd), lambda i, j: (i, j)), _prev_spec(ts, hd, lambda j: j),
                  _next_spec(ts, hd, lambda j: j, ni), pl.BlockSpec((4, hd), lambda i, j: (0, j)),
                  dspec(ts, lambda i: i),
                  dspec(HALO, lambda i: jnp.minimum((i + 1) * (ts // HALO), ni * (ts // HALO) - 1))] + [_HBM] * n_x,
        out_specs=[pl.BlockSpec((ts, hd), lambda i, j: (i, j)),
                   pl.BlockSpec((8, 4 * hd), lambda i, j: (i, j))] + [_HBM] * n_x,
        out_shape=[jax.ShapeDtypeStruct((s, c3), BF16), jax.ShapeDtypeStruct((ni * 8, 4 * c3), F32)]
        + [jax.ShapeDtypeStruct(t.shape, t.dtype) for t in scatter],
        scratch_shapes=_EXCHANGE_SEMS * n_x,
        compiler_params=_params(*(("arbitrary", "arbitrary") if n_x else ("parallel", "parallel"))))(
            pq, pq, pq, cw, dqkv3, dqkv3, *scatter)
    return outs[0], outs[1], list(outs[2:])


def _gate_terms(ba, al, dt, h, n_heads):
    lane = lax.broadcasted_iota(jnp.int32, ba.shape, 1)
    braw = jnp.sum(jnp.where(lane == h, ba, 0.0), axis=1, keepdims=True)
    araw = jnp.sum(jnp.where(lane == h + n_heads, ba, 0.0), axis=1, keepdims=True)
    beta = _sigmoid(braw)
    z = araw + dt
    sp = jnp.maximum(z, 0.0) + jnp.log(1.0 + jnp.exp(-jnp.abs(z)))
    ea = jnp.exp(jnp.zeros((1, 1), F32) + al)
    return beta, z, sp, ea


def _gates_fwd(ba, a_log, dt_bias, n_heads, *, name):
    s = ba.shape[0]
    ts = _tile(s, 512, CHUNK)

    def body(ba_ref, al_ref, dt_ref, o_ref):
        h = pl.program_id(1)
        beta, _, sp, ea = _gate_terms(ba_ref[...], al_ref[h], dt_ref[h], h, n_heads)
        gx = jnp.broadcast_to(-ea * sp, (ts, HEAD_DIM))
        rc = lax.broadcasted_iota(jnp.int32, (ts, HEAD_DIM), 0) & (CHUNK - 1)
        for sh in (1, 2, 4, 8, 16, 32):
            gx = gx + jnp.where(rc >= sh, _down(gx, sh), 0.0)
        o_ref[0] = jnp.broadcast_to(beta, (ts, HEAD_DIM))
        o_ref[1] = gx

    smem = pl.BlockSpec(memory_space=pltpu.SMEM)
    return pl.pallas_call(
        body, name=name, grid=(s // ts, n_heads),
        in_specs=[pl.BlockSpec((ts, 128), lambda i, h: (i, 0)), smem, smem],
        out_specs=pl.BlockSpec((2, ts, HEAD_DIM), lambda i, h: (0, i, h)),
        out_shape=jax.ShapeDtypeStruct((2, s, n_heads * HEAD_DIM), F32),
        compiler_params=_params("parallel", "parallel"))(ba, a_log, dt_bias)


def _gates_bwd(ba, a_log, dt_bias, dbg, n_heads, *, name):
    s = ba.shape[0]
    ts = _tile(s, 512, CHUNK)
    ni = s // ts

    def body(ba_ref, al_ref, dt_ref, d_ref, o_ref, p_ref):
        h = pl.program_id(1)
        beta, z, sp, ea = _gate_terms(ba_ref[...], al_ref[h], dt_ref[h], h, n_heads)
        dg = d_ref[1]
        rc = lax.broadcasted_iota(jnp.int32, (ts, HEAD_DIM), 0) & (CHUNK - 1)
        for sh in (1, 2, 4, 8, 16, 32):
            dg = dg + jnp.where(rc < CHUNK - sh, _up(dg, sh), 0.0)
        daraw = dg * (-ea * _sigmoid(z))
        dbraw = d_ref[0] * (beta * (1.0 - beta))

        @pl.when(h == 0)
        def _():
            o_ref[...] = jnp.zeros_like(o_ref)
            p_ref[...] = jnp.zeros_like(p_ref)

        lane = lax.broadcasted_iota(jnp.int32, (1, 128), 1)
        is_b, is_a = lane == h, lane == h + n_heads
        o_ref[...] += jnp.where(is_b, dbraw, 0.0) + jnp.where(is_a, daraw, 0.0)
        p_ref[...] += jnp.where(is_b, _fold8(dg * (-ea * sp)), 0.0) + jnp.where(is_a, _fold8(daraw), 0.0)

    smem = pl.BlockSpec(memory_space=pltpu.SMEM)
    return pl.pallas_call(
        body, name=name, grid=(ni, n_heads),
        in_specs=[pl.BlockSpec((ts, 128), lambda i, h: (i, 0)), smem, smem,
                  pl.BlockSpec((2, ts, HEAD_DIM), lambda i, h: (0, i, h))],
        out_specs=[pl.BlockSpec((ts, 128), lambda i, h: (i, 0)), pl.BlockSpec((8, 128), lambda i, h: (i, 0))],
        out_shape=[jax.ShapeDtypeStruct((s, 128), F32), jax.ShapeDtypeStruct((ni * 8, 128), F32)],
        compiler_params=_params("parallel", "arbitrary"))(ba, a_log, dt_bias, dbg)


BLK = 2 * CHUNK
HEADS_PER_STEP = 2
NN = (((1,), (0,)), ((), ()))
NT = (((1,), (1,)), ((), ()))
TN = (((0,), (0,)), ((), ()))


def _dot(a, b, dims):
    return lax.dot_general(a.astype(BF16), b.astype(BF16), dims, preferred_element_type=F32)


def _dot3(a, b, dims):
    ah, bh = a.astype(BF16), b.astype(BF16)
    al, bl = (a - ah.astype(F32)).astype(BF16), (b - bh.astype(F32)).astype(BF16)
    d = lambda u, v: lax.dot_general(u, v, dims, preferred_element_type=F32)
    return d(ah, bh) + (d(ah, bl) + d(al, bh))


def _pair_masks():
    row = lax.broadcasted_iota(jnp.int32, (BLK, BLK), 0)
    col = lax.broadcasted_iota(jnp.int32, (BLK, BLK), 1)
    same = (row < CHUNK) == (col < CHUNK)
    return same & (row >= col), same & (row > col), row == col


def _pair_terms(q, k, v, b, gam, masks):
    tril, strict, eye = masks
    g_cols = jnp.sum(jnp.where(eye, gam, 0.0), axis=0, keepdims=True)
    dmat = jnp.exp(jnp.where(tril, gam - g_cols, -jnp.inf))
    eg = jnp.exp(gam)
    rowi = lax.broadcasted_iota(jnp.int32, (BLK, HEAD_DIM), 0)
    elast = jnp.exp(jnp.where(rowi < CHUNK, gam[CHUNK - 1:CHUNK], gam[BLK - 1:BLK]) - gam)
    kb, vb = k * b, v * b
    kq = _dot(jnp.concatenate([kb, q], axis=0), k, NT)
    lmat = jnp.where(strict, kq[:BLK] * dmat, 0.0)
    attn = kq[BLK:] * dmat
    rhs = jnp.concatenate([vb, kb * eg], axis=1)
    return dmat, eg, elast, kb, lmat, attn, rhs


def _unit_lower_inverse(lmat, eye):
    p = -lmat
    t = jnp.where(eye, 1.0, 0.0) + p
    p = _dot(p, p, NN)
    for _ in range(4):
        r = _dot(jnp.concatenate([p, t], axis=0), p, NN)
        p, t = r[:BLK], t + r[BLK:]
    return t + _dot(t, p, NN)


def _ride(step, in_refs, out_refs, sems, n_steps, *, start):
    if not in_refs:
        return
    i, j = pl.program_id(0), pl.program_id(1)
    when = (i == 0) & (j == 0) if start else (i == n_steps[0] - 1) & (j == n_steps[1] - 1)

    @pl.when(when)
    def _():
        for t, (src, dst) in enumerate(zip(in_refs, out_refs)):
            step(src, dst, *sems[3 * t:3 * t + 3], start=start, finish=not start)


def _gdn_fwd(qkv3, bg, gather=(), *, name):
    _, s, dl = qkv3.shape
    n_heads = dl // HEAD_DIM
    sb = _tile(s, 1024, BLK)
    npair = sb // BLK
    c = CHUNK

    hpg = HEADS_PER_STEP if n_heads % HEADS_PER_STEP == 0 else 1
    wd = hpg * HEAD_DIM

    n_x = len(gather)
    n_steps = (n_heads // hpg, s // sb)

    def body(*refs):
        qkv_ref, bg_ref = refs[:2]
        o_ref, st_ref, ti_ref = refs[2 + n_x:5 + n_x]
        s_scr = refs[5 + 2 * n_x]
        _ride(_allgather_step, refs[2:2 + n_x], refs[5 + n_x:5 + 2 * n_x], refs[6 + 2 * n_x:], n_steps, start=True)

        @pl.when(pl.program_id(1) == 0)
        def _():
            s_scr[...] = jnp.zeros_like(s_scr)

        masks = _pair_masks()

        def head_pair(hh, p, rows, state):
            ls = slice(hh * HEAD_DIM, (hh + 1) * HEAD_DIM)
            q, k, v = qkv_ref[0, rows, ls], qkv_ref[1, rows, ls], qkv_ref[2, rows, ls]
            b, gam = bg_ref[0, rows, ls], bg_ref[1, rows, ls]
            _, eg, elast, _, lmat, attn, rhs = _pair_terms(q, k, v, b, gam, masks)
            tinv = _unit_lower_inverse(lmat, masks[2])
            ti_ref[hh, rows, :] = tinv
            sol = _dot3(tinv, rhs, NN)
            u, w = sol[:, :HEAD_DIM], sol[:, HEAD_DIM:]
            qd, ke = q * eg, k * elast
            st_ref[hh, 2 * p] = state
            wq = _dot(jnp.concatenate([w[:c], qd[:c]], axis=0), state, NN)
            vn_a, o_a = u[:c] - wq[:c], wq[c:]
            state = state * jnp.exp(gam[c - 1:c]) + _dot(ke[:c], vn_a, TN)
            st_ref[hh, 2 * p + 1] = state
            wq = _dot(jnp.concatenate([w[c:], qd[c:]], axis=0), state, NN)
            vn_b, o_b = u[c:] - wq[:c], wq[c:]
            state = state * jnp.exp(gam[BLK - 1:BLK]) + _dot(ke[c:], vn_b, TN)
            o_ref[rows, ls] = jnp.concatenate([o_a, o_b], axis=0) + _dot(attn, jnp.concatenate([vn_a, vn_b], axis=0), NN)
            return state

        def pair(p, states):
            rows = pl.ds(pl.multiple_of(p * BLK, BLK), BLK)
            return tuple(head_pair(hh, p, rows, states[hh]) for hh in range(hpg))

        states = lax.fori_loop(0, npair, pair, tuple(s_scr[hh] for hh in range(hpg)))
        for hh in range(hpg):
            s_scr[hh] = states[hh]
        _ride(_allgather_step, refs[2:2 + n_x], refs[5 + n_x:5 + 2 * n_x], refs[6 + 2 * n_x:], n_steps, start=False)

    outs = pl.pallas_call(
        body, name=name, grid=n_steps,
        in_specs=[pl.BlockSpec((3, sb, wd), lambda h, j: (0, j, h)),
                  pl.BlockSpec((2, sb, wd), lambda h, j: (0, j, h))] + [_HBM] * n_x,
        out_specs=[pl.BlockSpec((sb, wd), lambda h, j: (j, h)),
                   pl.BlockSpec((hpg, 2 * npair, HEAD_DIM, HEAD_DIM), lambda h, j: (h, j, 0, 0)),
                   pl.BlockSpec((hpg, sb, BLK), lambda h, j: (h, j, 0))] + [_HBM] * n_x,
        out_shape=[jax.ShapeDtypeStruct((s, dl), F32),
                   jax.ShapeDtypeStruct((n_heads, s // c, HEAD_DIM, HEAD_DIM), F32),
                   jax.ShapeDtypeStruct((n_heads, s, BLK), F32)] + [_allgather_shape(t) for t in gather],
        scratch_shapes=[pltpu.VMEM((hpg, HEAD_DIM, HEAD_DIM), F32)] + _EXCHANGE_SEMS * n_x,
        compiler_params=_params("arbitrary", "arbitrary"))(qkv3, bg, *gather)
    return outs[0], outs[1], outs[2], list(outs[3:])


def _gdn_bwd(qkv3, bg, st, ti, d_o, scatter=(), *, name):
    _, s, dl = qkv3.shape
    n_heads = dl // HEAD_DIM
    sb = _tile(s, 1024, BLK)
    npair, nsb = sb // BLK, s // sb
    c = CHUNK
    hpg = HEADS_PER_STEP if n_heads % HEADS_PER_STEP == 0 else 1
    wd = hpg * HEAD_DIM

    n_x = len(scatter)
    n_steps = (n_heads // hpg, nsb)

    def body(*refs):
        qkv_ref, bg_ref, st_ref, ti_ref, do_ref = refs[:5]
        dqkv_ref, dbg_ref = refs[5 + n_x:7 + n_x]
        ds_scr = refs[7 + 2 * n_x]
        _ride(_all_to_all_step, refs[5:5 + n_x], refs[7 + n_x:7 + 2 * n_x], refs[8 + 2 * n_x:], n_steps, start=True)

        @pl.when(pl.program_id(1) == 0)
        def _():
            ds_scr[...] = jnp.zeros_like(ds_scr)

        masks = _pair_masks()
        tril, strict, eye = masks
        rowc = lax.broadcasted_iota(jnp.int32, (BLK, 1), 0)

        def total(x):
            return jnp.sum(jnp.sum(x, axis=1, keepdims=True), axis=0, keepdims=True)

        def head_pair(hh, p, rows, ds2):
            ls = slice(hh * HEAD_DIM, (hh + 1) * HEAD_DIM)
            q, k, v = qkv_ref[0, rows, ls], qkv_ref[1, rows, ls], qkv_ref[2, rows, ls]
            b, gam = bg_ref[0, rows, ls], bg_ref[1, rows, ls]
            tinv, dout = ti_ref[hh, rows, :], do_ref[rows, ls]
            s0, s1 = st_ref[hh, 2 * p], st_ref[hh, 2 * p + 1]
            dmat, eg, elast, kb, lmat, attn, rhs = _pair_terms(q, k, v, b, gam, masks)
            sol = _dot3(tinv, rhs, NN)
            u, w = sol[:, :HEAD_DIM], sol[:, HEAD_DIM:]
            qd, ke = q * eg, k * elast
            dec_a, dec_b = jnp.exp(gam[c - 1:c]), jnp.exp(gam[BLK - 1:BLK])
            vn = u - jnp.concatenate([_dot(w[:c], s0, NN), _dot(w[c:], s1, NN)], axis=0)
            dvn_o = _dot(attn, dout, TN)
            dvn_b = dvn_o[c:] + _dot(ke[c:], ds2, NN)
            ds1 = _dot(qd[c:], dout[c:], TN) + ds2 * dec_b - _dot(w[c:], dvn_b, TN)
            dvn_a = dvn_o[:c] + _dot(ke[:c], ds1, NN)
            ds0 = _dot(qd[:c], dout[:c], TN) + ds1 * dec_a - _dot(w[:c], dvn_a, TN)
            dvn = jnp.concatenate([dvn_a, dvn_b], axis=0)
            dke = jnp.concatenate([_dot(vn[:c], ds1, NT), _dot(vn[c:], ds2, NT)], axis=0)
            dw = -jnp.concatenate([_dot(dvn_a, s0, NT), _dot(dvn_b, s1, NT)], axis=0)
            dqd = jnp.concatenate([_dot(dout[:c], s0, NT), _dot(dout[c:], s1, NT)], axis=0)
            dattn = jnp.where(tril, _dot(dout, vn, NT), 0.0)
            drhs = _dot3(tinv, jnp.concatenate([dvn, dw], axis=1), TN)
            dl_ = jnp.where(strict, -_dot3(drhs, sol, NT), 0.0)
            dm, dqk = dl_ * dmat, dattn * dmat
            dvb, drw = drhs[:, :HEAD_DIM], drhs[:, HEAD_DIM:]
            dkb = _dot(dm, k, NN) + drw * eg
            dq = _dot(dqk, k, NN) + dqd * eg
            dk = _dot(dm, kb, TN) + _dot(dqk, q, TN) + dke * elast + dkb * b
            dbeta = jnp.sum(dvb * v + dkb * k, axis=1, keepdims=True)
            e = dl_ * lmat + dattn * attn
            e_cols = jnp.sum(jnp.where(eye, jnp.sum(e, axis=0, keepdims=True), 0.0), axis=1, keepdims=True)
            dke_ke = dke * ke
            dgam = (jnp.sum(e, axis=1, keepdims=True) - e_cols
                    + jnp.sum(drw * (kb * eg) + dqd * qd - dke_ke, axis=1, keepdims=True))
            tot_a = total(dke_ke[:c]) + total(s0 * ds1) * dec_a[:, :1]
            tot_b = total(dke_ke[c:]) + total(s1 * ds2) * dec_b[:, :1]
            dgam = dgam + jnp.where(rowc == c - 1, tot_a, 0.0) + jnp.where(rowc == BLK - 1, tot_b, 0.0)
            dqkv_ref[0, rows, ls] = dq
            dqkv_ref[1, rows, ls] = dk
            dqkv_ref[2, rows, ls] = dvb * b
            dbg_ref[0, rows, ls] = jnp.broadcast_to(dbeta, (BLK, HEAD_DIM))
            dbg_ref[1, rows, ls] = jnp.broadcast_to(dgam, (BLK, HEAD_DIM))
            return ds0

        def pair(pp, dstates):
            p = npair - 1 - pp
            rows = pl.ds(pl.multiple_of(p * BLK, BLK), BLK)
            return tuple(head_pair(hh, p, rows, dstates[hh]) for hh in range(hpg))

        dstates = lax.fori_loop(0, npair, pair, tuple(ds_scr[hh] for hh in range(hpg)))
        for hh in range(hpg):
            ds_scr[hh] = dstates[hh]
        _ride(_all_to_all_step, refs[5:5 + n_x], refs[7 + n_x:7 + 2 * n_x], refs[8 + 2 * n_x:], n_steps, start=False)

    rev = lambda j: nsb - 1 - j
    outs = pl.pallas_call(
        body, name=name, grid=n_steps,
        in_specs=[pl.BlockSpec((3, sb, wd), lambda h, j: (0, rev(j), h)),
                  pl.BlockSpec((2, sb, wd), lambda h, j: (0, rev(j), h)),
                  pl.BlockSpec((hpg, 2 * npair, HEAD_DIM, HEAD_DIM), lambda h, j: (h, rev(j), 0, 0)),
                  pl.BlockSpec((hpg, sb, BLK), lambda h, j: (h, rev(j), 0)),
                  pl.BlockSpec((sb, wd), lambda h, j: (rev(j), h))] + [_HBM] * n_x,
        out_specs=[pl.BlockSpec((3, sb, wd), lambda h, j: (0, rev(j), h)),
                   pl.BlockSpec((2, sb, wd), lambda h, j: (0, rev(j), h))] + [_HBM] * n_x,
        out_shape=[jax.ShapeDtypeStruct((3, s, dl), F32), jax.ShapeDtypeStruct((2, s, dl), F32)]
        + [jax.ShapeDtypeStruct(t.shape, t.dtype) for t in scatter],
        scratch_shapes=[pltpu.VMEM((hpg, HEAD_DIM, HEAD_DIM), F32)] + _EXCHANGE_SEMS * n_x,
        compiler_params=_params("arbitrary", "arbitrary"))(qkv3, bg, st, ti, d_o, *scatter)
    return outs[0], outs[1], list(outs[2:])


N_GROUPS = len(POOL_WINDOWS)


def _pick(g, vals):
    out = vals[-1]
    for i in range(len(vals) - 2, -1, -1):
        out = jnp.where(g == i, vals[i], out)
    return out


def _head_norm(o, nw):
    hats, outs = [], []
    for h in range(o.shape[1] // HEAD_DIM):
        sl = slice(h * HEAD_DIM, (h + 1) * HEAD_DIM)
        oh = o[:, sl]
        r = lax.rsqrt(jnp.mean(oh * oh, axis=-1, keepdims=True) + EPS)
        hats.append((oh * r, r))
        outs.append(oh * r * nw[:, sl])
    return hats, jnp.concatenate(outs, axis=1) if len(outs) > 1 else outs[0]


def _pool_counts(g, t0, n):
    t = (lax.broadcasted_iota(jnp.int32, (n, 1), 0) + t0 + 1).astype(F32)
    return jnp.minimum(t, _pick(g, [float(w) for w in POOL_WINDOWS]))


def _pool(prev, cur, first, g, t0):
    s = _with_prev(prev, cur, first)
    sums = []
    for sh in (1, 2, 4, 8):
        s = s + _down(s, sh)
        sums.append(s)
    return _pick(g, sums)[HALO:] / _pool_counts(g, t0, cur.shape[0]) - cur


def _mix_specs(ts, gw, ni):
    seg = lambda k: pl.BlockSpec((ts, gw), lambda g, i: (i, k * N_GROUPS + g))
    per = ts // HALO
    prev = lambda k: pl.BlockSpec((HALO, gw), lambda g, i: (jnp.maximum(i * per - 1, 0), k * N_GROUPS + g))
    nxt = lambda k: pl.BlockSpec((HALO, gw), lambda g, i: (jnp.minimum((i + 1) * per, ni * per - 1), k * N_GROUPS + g))
    vec = pl.BlockSpec((1, gw), lambda g, i: (0, g))
    pw = pl.BlockSpec((None, gw, gw), lambda g, i: (g, 0, 0))
    return seg, prev, nxt, vec, pw


def _mix_fwd(o, zpg, nw, pw, ps, *, name):
    s, d = o.shape
    gw = d // N_GROUPS
    ts = _tile(s, 512, HALO)
    ni = s // ts
    seg, prev, _, vec, pwspec = _mix_specs(ts, gw, ni)

    def body(o_ref, z_ref, p_ref, pp_ref, ga_ref, gb_ref, nw_ref, pw_ref, ps_ref, out_ref):
        g, i = pl.program_id(0), pl.program_id(1)
        _, on = _head_norm(o_ref[...], nw_ref[...])
        z = z_ref[...]
        ya = on * (z * _sigmoid(z))
        pooled = _pool(pp_ref[...], p_ref[...], i == 0, g, i * ts)
        yb = _dot(pooled, pw_ref[...], NN) * ps_ref[...]
        out_ref[...] = (_sigmoid(ga_ref[...]) * ya + _sigmoid(gb_ref[...]) * yb).astype(BF16)

    return pl.pallas_call(
        body, name=name, grid=(N_GROUPS, ni),
        in_specs=[seg(0), seg(0), seg(1), prev(1), seg(2), seg(3), vec, pwspec, vec],
        out_specs=seg(0), out_shape=jax.ShapeDtypeStruct((s, d), BF16),
        compiler_params=_params("parallel", "parallel"))(o, zpg, zpg, zpg, zpg, zpg, nw, pw, ps)


def _mix_bwd(o, zpg, nw, pw, ps, dmix, *, name):
    s, d = o.shape
    gw = d // N_GROUPS
    ts = _tile(s, 512, HALO)
    ni = s // ts
    seg, prev, nxt, vec, pwspec = _mix_specs(ts, gw, ni)

    def body(o_ref, z_ref, p_ref, pp_ref, ga_ref, gb_ref, gbn_ref, nw_ref, pw_ref, ps_ref, dm_ref, dmn_ref,
             do_ref, d4_ref, dpw_ref, dnw_ref, dps_ref):
        g, i = pl.program_id(0), pl.program_id(1)
        last = i == ni - 1
        nw, ps, pwv = nw_ref[...], ps_ref[...], pw_ref[...]
        ov, z = o_ref[...], z_ref[...]
        hats, on = _head_norm(ov, nw)
        sz = _sigmoid(z)
        silu = z * sz
        ya = on * silu
        pooled = _pool(pp_ref[...], p_ref[...], i == 0, g, i * ts)
        yp = _dot(pooled, pwv, NN)
        sga, sgb = _sigmoid(ga_ref[...]), _sigmoid(gb_ref[...])
        dm = dm_ref[...]
        dya, dyb = dm * sga, dm * sgb
        d4_ref[2] = (dm * ya * (sga * (1.0 - sga))).astype(BF16)
        d4_ref[3] = (dm * (yp * ps) * (sgb * (1.0 - sgb))).astype(BF16)
        dps_ref[...] = _fold8(dyb * yp)
        dyp = dyb * ps

        @pl.when(i == 0)
        def _():
            dpw_ref[...] = jnp.zeros_like(dpw_ref)

        dpw_ref[...] += _dot(pooled, dyp, TN)
        dyp_next = jnp.where(last, 0.0, dmn_ref[...] * _sigmoid(gbn_ref[...]) * ps)
        dpool = _dot(jnp.concatenate([dyp, dyp_next], axis=0), pwv, NT)
        a = dpool / _pool_counts(g, i * ts, ts + HALO)
        sums = []
        for sh in (1, 2, 4, 8):
            a = a + _up(a, sh)
            sums.append(a)
        d4_ref[1] = (_pick(g, sums)[:ts] - dpool[:ts]).astype(BF16)
        d4_ref[0] = (dya * on * (sz * (1.0 + z * (1.0 - sz)))).astype(BF16)
        don = dya * silu
        dos, dnws = [], []
        for h, (ohat, r) in enumerate(hats):
            sl = slice(h * HEAD_DIM, (h + 1) * HEAD_DIM)
            dxh = don[:, sl] * nw[:, sl]
            dos.append(r * (dxh - ohat * jnp.mean(dxh * ohat, axis=-1, keepdims=True)))
            dnws.append(_fold8(don[:, sl] * ohat))
        do_ref[...] = jnp.concatenate(dos, axis=1) if len(dos) > 1 else dos[0]
        dnw_ref[...] = jnp.concatenate(dnws, axis=1) if len(dnws) > 1 else dnws[0]

    part = pl.BlockSpec((8, gw), lambda g, i: (i, g))
    return pl.pallas_call(
        body, name=name, grid=(N_GROUPS, ni),
        in_specs=[seg(0), seg(0), seg(1), prev(1), seg(2), seg(3), nxt(3), vec, pwspec, vec, seg(0), nxt(0)],
        out_specs=[seg(0), pl.BlockSpec((4, ts, gw), lambda g, i: (0, i, g)),
                   pl.BlockSpec((None, gw, gw), lambda g, i: (g, 0, 0)), part, part],
        out_shape=[jax.ShapeDtypeStruct((s, d), F32), jax.ShapeDtypeStruct((4, s, d), BF16),
                   jax.ShapeDtypeStruct((N_GROUPS, gw, gw), F32),
                   jax.ShapeDtypeStruct((ni * 8, d), F32), jax.ShapeDtypeStruct((ni * 8, d), F32)],
        compiler_params=_params("parallel", "arbitrary"))(o, zpg, zpg, zpg, zpg, zpg, zpg, nw, pw, ps, dmix, dmix)


def _adamw(w, g, m, v, *, name):
    r, c = w.shape
    tr = _tile(r, max(8, (1 << 19) // c // 8 * 8), 8)

    def body(w_ref, g_ref, m_ref, v_ref, d_ref, mo_ref, vo_ref):
        gv = g_ref[...]
        mn = ADAM_B1 * m_ref[...] + (1.0 - ADAM_B1) * gv
        vn = ADAM_B2 * v_ref[...] + (1.0 - ADAM_B2) * (gv * gv)
        m_hat = mn / (1.0 - ADAM_B1 ** ADAM_STEP)
        v_hat = vn / (1.0 - ADAM_B2 ** ADAM_STEP)
        d_ref[...] = -ADAM_LR * (m_hat / (jnp.sqrt(v_hat) + ADAM_EPS) + ADAM_WD * w_ref[...])
        mo_ref[...] = mn
        vo_ref[...] = vn

    blk = pl.BlockSpec((tr, c), lambda i: (i, 0))
    return pl.pallas_call(
        body, name=name, grid=(r // tr,), in_specs=[blk] * 4, out_specs=[blk] * 3,
        out_shape=[jax.ShapeDtypeStruct((r, c), F32)] * 3, compiler_params=_params("parallel"))(w, g, m, v)


def _sum_parts(x, *, slot=None, name):
    p, r, c = x.shape
    tc = 128 if c % 128 == 0 else c
    tr = _tile(r, max(16, SUM_BLOCK_BYTES // (p * tc * x.dtype.itemsize)), 16)

    def body(*refs):
        x_ref, o_ref = refs[-2:]
        acc = x_ref[0].astype(F32)
        for i in range(1, p):
            acc = acc + x_ref[i].astype(F32)
        o_ref[...] = acc

    if slot is None:
        return pl.pallas_call(
            body, name=name, grid=(r // tr, c // tc),
            in_specs=[pl.BlockSpec((p, tr, tc), lambda i, j: (0, i, j))],
            out_specs=pl.BlockSpec((tr, tc), lambda i, j: (i, j)),
            out_shape=jax.ShapeDtypeStruct((r, c), F32), compiler_params=_params("parallel", "parallel"))(x)
    return pl.pallas_call(
        body, name=name,
        grid_spec=pltpu.PrefetchScalarGridSpec(
            num_scalar_prefetch=1, grid=(r // tr, c // tc),
            in_specs=[pl.BlockSpec((p, tr, tc), lambda i, j, s: (0, i, j))],
            out_specs=pl.BlockSpec((None, tr, tc), lambda i, j, s: (s[0], i, j))),
        out_shape=jax.ShapeDtypeStruct((2, r, c), F32), compiler_params=_params("parallel", "parallel"))(slot, x)


_HBM = pl.BlockSpec(memory_space=pltpu.HBM)


def _place():
    return lax.axis_index("x"), lax.axis_index("y"), lax.axis_index("c")


def _allgather(x_shard, *, name):
    def body(x_ref, out_ref, send_sems, recv_sems, local_sem):
        _allgather_step(x_ref, out_ref, send_sems, recv_sems, local_sem, start=True, finish=True)

    return pl.pallas_call(
        body, name=name, out_shape=_allgather_shape(x_shard), in_specs=[_HBM], out_specs=_HBM,
        scratch_shapes=_EXCHANGE_SEMS)(x_shard)


_EXCHANGE_SEMS = [pltpu.SemaphoreType.DMA((7,)), pltpu.SemaphoreType.DMA((7,)), pltpu.SemaphoreType.DMA]


def _allgather_shape(x_shard):
    return jax.ShapeDtypeStruct((8 * x_shard.shape[0], x_shard.shape[1]), x_shard.dtype)


def _allgather_step(x_ref, out_ref, send_sems, recv_sems, local_sem, *, start, finish):
    m_per = x_ref.shape[0]
    x, y, c = _place()
    me, sibling = (x, y, c), (x, y, 1 - c)
    chips = [(1 - x, y), (x, 1 - y), (1 - x, 1 - y)]

    def rows(px, py, pc):
        return out_ref.at[pl.ds((4 * px + 2 * py + pc) * m_per, m_per), :]

    def copy(k, block, to, src=None):
        return pltpu.make_async_remote_copy(
            src_ref=rows(*block) if src is None else src, dst_ref=rows(*block),
            send_sem=send_sems.at[k], recv_sem=recv_sems.at[k], device_id=to, device_id_type=MESH)

    mine = pltpu.make_async_copy(x_ref, rows(*me), local_sem)
    first = [copy(0, me, sibling, src=x_ref)]
    first += [copy(1 + j, me, (*chip, c), src=x_ref) for j, chip in enumerate(chips)]
    if start:
        mine.start()
        for cp in first:
            cp.start()
    if finish:
        passed = [copy(4 + j, (*chip, c), sibling) for j, chip in enumerate(chips)]
        for j, chip in enumerate(chips):
            copy(1 + j, (*chip, c), me).wait_recv()
            passed[j].start()
        copy(0, sibling, me).wait_recv()
        for j, chip in enumerate(chips):
            copy(4 + j, (*chip, 1 - c), me).wait_recv()
        for cp in first + passed:
            cp.wait_send()
        mine.wait()


def _all_to_all(parts, *, name):
    def body(g_ref, out_ref, send_sems, recv_sems, local_sem):
        _all_to_all_step(g_ref, out_ref, send_sems, recv_sems, local_sem, start=True, finish=True)

    return pl.pallas_call(
        body, name=name, out_shape=jax.ShapeDtypeStruct(parts.shape, parts.dtype), in_specs=[_HBM], out_specs=_HBM,
        scratch_shapes=_EXCHANGE_SEMS)(parts)


def _all_to_all_step(g_ref, out_ref, send_sems, recv_sems, local_sem, *, start, finish):
    x, y, c = _place()
    me = 4 * x + 2 * y + c
    mine = pltpu.make_async_copy(g_ref.at[me], out_ref.at[me], local_sem)
    if start:
        mine.start()
    sends, peers = [], []
    for k in range(1, 8):
        px = 1 - x if k & 4 else x
        py = 1 - y if k & 2 else y
        pc = 1 - c if k & 1 else c
        peer = 4 * px + 2 * py + pc
        cp = pltpu.make_async_remote_copy(
            src_ref=g_ref.at[peer], dst_ref=out_ref.at[me], send_sem=send_sems.at[k - 1],
            recv_sem=recv_sems.at[k - 1], device_id=(px, py, pc), device_id_type=MESH)
        if start:
            cp.start()
        sends.append(cp)
        peers.append((peer, (px, py, pc)))
    if finish:
        for k, (peer, pid) in enumerate(peers):
            pltpu.make_async_remote_copy(
                src_ref=g_ref.at[peer], dst_ref=out_ref.at[peer], send_sem=send_sems.at[k],
                recv_sem=recv_sems.at[k], device_id=pid, device_id_type=MESH).wait_recv()
        for cp in sends:
            cp.wait_send()
        mine.wait()


def _share_halves(both, *, name):
    _, r, _ = both.shape
    n_split = 1
    while both.size // 2 * both.dtype.itemsize > n_split * MAX_COPY_BYTES and r % (2 * n_split * 16) == 0:
        n_split *= 2
    rs = r // n_split

    def body(in_ref, out_ref, send_sems, recv_sems):
        x, y, c = _place()

        def copy(k, slot):
            rows = pl.ds(k * rs, rs)
            return pltpu.make_async_remote_copy(
                src_ref=in_ref.at[slot, rows], dst_ref=out_ref.at[slot, rows], send_sem=send_sems.at[k],
                recv_sem=recv_sems.at[k], device_id=(x, y, 1 - c), device_id_type=MESH)

        sends = [copy(k, c) for k in range(n_split)]
        for cp in sends:
            cp.start()
        for k in range(n_split):
            copy(k, 1 - c).wait_recv()
        for cp in sends:
            cp.wait_send()

    return pl.pallas_call(
        body, name=name, out_shape=jax.ShapeDtypeStruct(both.shape, both.dtype), in_specs=[_HBM], out_specs=_HBM,
        input_output_aliases={0: 0},
        scratch_shapes=[pltpu.SemaphoreType.DMA((n_split,)), pltpu.SemaphoreType.DMA((n_split,))])(both)


def _piece_rows(shape):
    n = math.prod(shape)
    if n % 128 == 0:
        return n // 128, 128
    assert shape[-1] <= 128, shape
    return n // shape[-1], shape[-1]


def _pack_small(arrs, row_multiple):
    pieces = []
    for a in arrs:
        rows, lanes = _piece_rows(a.shape)
        t = a.astype(F32).reshape(rows, lanes)
        pieces.append(jnp.pad(t, ((0, -rows % 8), (0, 128 - lanes))))
    buf = jnp.concatenate(pieces, axis=0)
    return jnp.pad(buf, ((0, -buf.shape[0] % row_multiple), (0, 0)))


def _unpack_small(buf, shapes):
    out, off = [], 0
    for shp in shapes:
        rows, lanes = _piece_rows(shp)
        out.append(buf[off:off + rows, :lanes].reshape(shp))
        off += rows + (-rows % 8)
    return out


def _w_in_grad_parts(g_qkv, g_zpg, g_ba, n_heads, *, name):
    d = g_qkv.shape[0]
    cw = (g_qkv.shape[1] + 4 * d + 2 * n_heads) // N_CHIPS
    tr = _tile(d // 2, 128, 16)
    per_half = d // 2 // tr

    def body(a_ref, z_ref, p_ref, ga_ref, gb_ref, ba_ref, o_ref):
        full = jnp.concatenate([a_ref[...], z_ref[...], ba_ref[...][:, :2 * n_heads], p_ref[...], ga_ref[...],
                                gb_ref[...]], axis=1)
        for j in range(N_CHIPS):
            o_ref[j] = full[:, cw * j:cw * (j + 1)]

    row = lambda c: pl.BlockSpec((tr, c), lambda i: (i, 0))
    out = pl.pallas_call(
        body, name=name, grid=(d // tr,),
        in_specs=[row(g_qkv.shape[1]), row(d), row(d), row(d), row(d), row(128)],
        out_specs=pl.BlockSpec((N_CHIPS, None, tr, cw), lambda i: (0, i // per_half, i % per_half, 0)),
        out_shape=jax.ShapeDtypeStruct((N_CHIPS, 2, d // 2, cw), BF16),
        compiler_params=_params("parallel"))(g_qkv, *g_zpg, g_ba)
    return out.reshape(8, d // 2, cw)


def _layer_fwd(x, p, n_heads, gather=()):
    h = _rmsnorm_fwd(x, p["norm_mix_w"], name="norm_mix_fwd")
    pq = _matmul(h, p["w_qkv"], name="proj_qkv")
    zpg = _matmul(h, p["w_zpg"], name="proj_zpg")
    ba = _matmul(h, p["w_ba"], name="proj_ba")
    qkv3 = _qkv_fwd(pq, p["conv_qkv_w"], n_heads, name="qkv_fwd")
    bg = _gates_fwd(ba, p["a_log"], p["dt_bias"], n_heads, name="gates_fwd")
    o, st, ti, gathered = _gdn_fwd(qkv3, bg, gather, name="gdn_fwd")
    mixed = _mix_fwd(o, zpg, p["gdn_nw"], p["pool_w"], p["pool_scale"], name="mix_fwd")
    x1 = _matmul(mixed, p["w_out"], add=x, name="out_proj")
    h2 = _rmsnorm_fwd(x1, p["norm_ffn_w"], name="norm_ffn_fwd")
    gu = _matmul(h2, p["w_up"], name="up_proj")
    act = _ffn_mid_fwd(gu, p["conv_ffn_w"], p["conv_ffn_b"], name="ffn_mid_fwd")
    x2 = _matmul(act, p["w_down"], add=x1, name="down_proj")
    return x2, (x, h, pq, zpg, ba, o, st, ti, mixed, x1, h2, gu, act), gathered


def _layer_bwd(dx2, p, saved, n_heads, scatter=()):
    x, h, pq, zpg, ba, o, st, ti, mixed, x1, h2, gu, act = saved
    d = x.shape[1]
    f = act.shape[1]
    dact = _matmul(dx2, p["w_down"], tb=True, name="d_act")
    g_down = _matmul(act, dx2, ta=True, out_dtype=BF16, name="g_w_down")
    riders = lambda carrier: [scatter[n] for n in RIDERS[carrier] if scatter]
    scattered = {}
    dgate, dup, dcw_p, dcb_p, got = _ffn_mid_bwd(gu, p["conv_ffn_w"], p["conv_ffn_b"], dact, riders("ffn_mid_bwd"),
                                                 name="ffn_mid_bwd")
    scattered.update(zip(RIDERS["ffn_mid_bwd"], got))
    dh2 = _matmul(dgate, p["w_up"], tb=True, b_k0=0, name="d_h2_gate")
    dh2 = _matmul(dup, p["w_up"], tb=True, b_k0=f, add=dh2, name="d_h2_up")
    g_up = jnp.concatenate([_matmul(h2, dgate, ta=True, out_dtype=BF16, name="g_w_up_gate"),
                            _matmul(h2, dup, ta=True, out_dtype=BF16, name="g_w_up_up")], axis=1)
    dx1, dnf_p = _rmsnorm_bwd(x1, p["norm_ffn_w"], dh2, dx2, name="norm_ffn_bwd")
    dmix = _matmul(dx1, p["w_out"], tb=True, name="d_mixed")
    g_out = _matmul(mixed, dx1, ta=True, out_dtype=BF16, name="g_w_out")
    d_o, d4, g_pool, dnw_p, dps_p = _mix_bwd(o, zpg, p["gdn_nw"], p["pool_w"], p["pool_scale"], dmix, name="mix_bwd")
    qkv3 = _qkv_fwd(pq, p["conv_qkv_w"], n_heads, name="qkv_fwd")
    bg = _gates_fwd(ba, p["a_log"], p["dt_bias"], n_heads, name="gates_fwd")
    dqkv3, dbg, got = _gdn_bwd(qkv3, bg, st, ti, d_o, riders("gdn_bwd"), name="gdn_bwd")
    scattered.update(zip(RIDERS["gdn_bwd"], got))
    dpq, dcq_p, got = _qkv_bwd(pq, p["conv_qkv_w"], dqkv3, n_heads, riders("qkv_bwd"), name="qkv_bwd")
    scattered.update(zip(RIDERS["qkv_bwd"], got))
    dba, dgate_p = _gates_bwd(ba, p["a_log"], p["dt_bias"], dbg, n_heads, name="gates_bwd")
    dh = _matmul(dpq, p["w_qkv"], tb=True, name="d_h_qkv")
    for seg in range(4):
        dh = _matmul(d4, p["w_zpg"], tb=True, a_part=seg, b_k0=seg * d, add=dh, name="d_h_zpg")
    dh = _matmul(dba, p["w_ba"], tb=True, add=dh, name="d_h_ba")
    g_qkv = _matmul(h, dpq, ta=True, out_dtype=BF16, name="g_w_qkv")
    g_zpg = [_matmul(h, d4, ta=True, b_part=seg, out_dtype=BF16, name="g_w_zpg") for seg in range(4)]
    g_ba = _matmul(h, dba, ta=True, out_dtype=BF16, name="g_w_ba")
    dx, dnm_p = _rmsnorm_bwd(x, p["norm_mix_w"], dh, dx1, name="norm_mix_bwd")
    g_in = _w_in_grad_parts(g_qkv, g_zpg, g_ba, n_heads, name="g_w_in_parts")
    rows = lambda t: jnp.sum(t, axis=0)
    ni8, c12 = dcq_p.shape
    nj = dcw_p.shape[1] // (3 * _tile(f, 512))
    small = {
        "norm_mix_w": rows(dnm_p),
        "conv_qkv_w": rows(dcq_p).reshape(c12 // (4 * HEAD_DIM), 4, HEAD_DIM).transpose(1, 0, 2).reshape(4, c12 // 4),
        "a_log": rows(dgate_p)[:n_heads],
        "dt_bias": rows(dgate_p)[n_heads:2 * n_heads],
        "gdn_norm_w": jnp.sum(rows(dnw_p).reshape(d // HEAD_DIM, HEAD_DIM), axis=0),
        "pool_scale": rows(dps_p),
        "norm_ffn_w": rows(dnf_p),
        "conv_ffn_w": rows(dcw_p).reshape(nj, 3, f // nj).transpose(1, 0, 2).reshape(3, f),
        "conv_ffn_b": rows(dcb_p),
    }
    big = {"w_in": g_in, "pool_w": g_pool, "w_out": g_out, "w_up": g_up, "w_down": g_down}
    return dx, big, small, scattered


BIG = ("w_in", "pool_w", "w_out", "w_up", "w_down")
SMALL = ("norm_mix_w", "conv_qkv_w", "a_log", "dt_bias", "gdn_norm_w", "pool_scale", "norm_ffn_w", "conv_ffn_w",
         "conv_ffn_b", "norm_final_w")
WEIGHTS = ("norm_mix_w", "w_in", "conv_qkv_w", "a_log", "dt_bias", "gdn_norm_w", "pool_w", "pool_scale", "w_out",
           "norm_ffn_w", "w_up", "conv_ffn_w", "conv_ffn_b", "w_down", "norm_final_w")
N_CHIPS = 4
RIDERS = {"ffn_mid_bwd": ("w_down",), "gdn_bwd": ("w_in", "pool_w", "w_out"), "qkv_bwd": ("w_up",)}


def _my_half(local, cc):
    m = local.shape[0] // 2
    return lax.dynamic_slice_in_dim(local, cc * m, m, axis=0)


def _weight_halves(w, l, cc):
    pw = w["pool_w"][l].astype(BF16)
    local = dict(w_in=w["w_in"][l].astype(BF16), pool_w=pw.reshape(-1, pw.shape[-1]),
                 w_out=w["w_out"][l].astype(BF16), w_up=w["w_up"][l].astype(BF16), w_down=w["w_down"][l].astype(BF16))
    return [_my_half(local[n], cc) for n in BIG]


def _full_weights(w, gathered):
    g_in, g_pool, g_out, g_up, g_down = gathered
    d = w["w_in"].shape[1]
    g, r, c = w["pool_w"].shape[1:]
    wi = g_in.reshape(N_CHIPS, d, -1)
    return dict(
        w_in=jnp.concatenate([wi[j] for j in range(N_CHIPS)], axis=1),
        pool_w=g_pool.reshape(N_CHIPS, g, r, c).transpose(1, 0, 2, 3).reshape(g, N_CHIPS * r, c),
        w_out=g_out, w_up=g_up.reshape(N_CHIPS, d, -1).transpose(1, 0, 2).reshape(d, -1), w_down=g_down)


def _grad_parts(name, g):
    if name in ("w_in", "w_up"):
        r, c = g.shape
        return g.reshape(2, r // 2, N_CHIPS, c // N_CHIPS).transpose(2, 0, 1, 3).reshape(8, r // 2, c // N_CHIPS)
    if name == "pool_w":
        ng, r, c = g.shape
        t = g.reshape(2, ng // 2, N_CHIPS, r // N_CHIPS, c).transpose(2, 0, 1, 3, 4)
        return t.reshape(8, (ng // 2) * (r // N_CHIPS), c)
    r, c = g.shape
    return g.reshape(8, r // 8, c)


def _layer_parts(big):
    return {n: big[n] if n == "w_in" else _grad_parts(n, big[n].astype(BF16)) for n in BIG}


def _finish_reduce(name, got, shard_shape, slot):
    both = _share_halves(_sum_parts(got, slot=slot, name="sum_" + name), name="share_" + name)
    return both.reshape(shard_shape)


def kernel(x, norm_mix_w, w_in, conv_qkv_w, a_log, dt_bias, gdn_norm_w, pool_w, pool_scale, w_out, norm_ffn_w, w_up, conv_ffn_w, conv_ffn_b, w_down, norm_final_w, loss_target, m_norm_mix_w, m_w_in, m_conv_qkv_w, m_a_log, m_dt_bias, m_gdn_norm_w, m_pool_w, m_pool_scale, m_w_out, m_norm_ffn_w, m_w_up, m_conv_ffn_w, m_conv_ffn_b, m_w_down, m_norm_final_w, v_norm_mix_w, v_w_in, v_conv_qkv_w, v_a_log, v_dt_bias, v_gdn_norm_w, v_pool_w, v_pool_scale, v_w_out, v_norm_ffn_w, v_w_up, v_conv_ffn_w, v_conv_ffn_b, v_w_down, v_norm_final_w):
    w = dict(norm_mix_w=norm_mix_w, w_in=w_in, conv_qkv_w=conv_qkv_w, a_log=a_log, dt_bias=dt_bias,
             gdn_norm_w=gdn_norm_w, pool_w=pool_w, pool_scale=pool_scale, w_out=w_out, norm_ffn_w=norm_ffn_w,
             w_up=w_up, conv_ffn_w=conv_ffn_w, conv_ffn_b=conv_ffn_b, w_down=w_down, norm_final_w=norm_final_w)
    m = dict(norm_mix_w=m_norm_mix_w, w_in=m_w_in, conv_qkv_w=m_conv_qkv_w, a_log=m_a_log, dt_bias=m_dt_bias,
             gdn_norm_w=m_gdn_norm_w, pool_w=m_pool_w, pool_scale=m_pool_scale, w_out=m_w_out,
             norm_ffn_w=m_norm_ffn_w, w_up=m_w_up, conv_ffn_w=m_conv_ffn_w, conv_ffn_b=m_conv_ffn_b,
             w_down=m_w_down, norm_final_w=m_norm_final_w)
    v = dict(norm_mix_w=v_norm_mix_w, w_in=v_w_in, conv_qkv_w=v_conv_qkv_w, a_log=v_a_log, dt_bias=v_dt_bias,
             gdn_norm_w=v_gdn_norm_w, pool_w=v_pool_w, pool_scale=v_pool_scale, w_out=v_w_out,
             norm_ffn_w=v_norm_ffn_w, w_up=v_w_up, conv_ffn_w=v_conv_ffn_w, conv_ffn_b=v_conv_ffn_b,
             w_down=v_w_down, norm_final_w=v_norm_final_w)
    depth, n_heads = a_log.shape
    d = x.shape[-1]
    dl = n_heads * HEAD_DIM
    assert dl == d and gdn_norm_w.shape[1] == HEAD_DIM
    cx, cy, cc = _place()
    chip = 2 * cx + cy

    conv_packed = _pack_small([conv_qkv_w, conv_ffn_w], 16)
    conv_all = _allgather(conv_packed, name="gather_conv").reshape(N_CHIPS, 2, -1, 128)[:, 0]
    conv_j = [_unpack_small(conv_all[j], [conv_qkv_w.shape, conv_ffn_w.shape]) for j in range(N_CHIPS)]
    conv_q = jnp.concatenate([t[0] for t in conv_j], axis=-1)
    conv_f = jnp.concatenate([t[1] for t in conv_j], axis=-1)

    def layer_params(l, full):
        wi = full["w_in"]
        return dict(
            norm_mix_w=norm_mix_w[l][None], norm_ffn_w=norm_ffn_w[l][None],
            w_qkv=wi[:, :3 * dl],
            w_zpg=jnp.concatenate([wi[:, 3 * dl:4 * dl], wi[:, 4 * dl + 2 * n_heads:]], axis=1),
            w_ba=jnp.concatenate([wi[:, 4 * dl:4 * dl + 2 * n_heads], jnp.zeros((d, 128 - 2 * n_heads), BF16)], axis=1),
            conv_qkv_w=conv_q[l], a_log=a_log[l], dt_bias=dt_bias[l],
            gdn_nw=jnp.tile(gdn_norm_w[l], d // HEAD_DIM)[None], pool_w=full["pool_w"], pool_scale=pool_scale[l][None],
            w_out=full["w_out"], w_up=full["w_up"], conv_ffn_w=conv_f[l], conv_ffn_b=conv_ffn_b[l][None],
            w_down=full["w_down"])

    xs = x[0]
    saved, params = [], []
    gathered = [_allgather(t, name="gather_" + n) for n, t in zip(BIG, _weight_halves(w, 0, cc))]
    for l in range(depth):
        params.append(layer_params(l, _full_weights(w, gathered)))
        nxt = _weight_halves(w, l + 1, cc) if l + 1 < depth else ()
        xs, sv, gathered = _layer_fwd(xs, params[l], n_heads, nxt)
        saved.append(sv)
    dx, dnf_p, loss_p = _final_loss(xs, norm_final_w[None], loss_target[0], name="final_loss")
    loss = lax.psum(jnp.sum(loss_p) * (0.5 / d), ("x", "y", "c"))

    slot = jnp.reshape(cc, (1,)).astype(jnp.int32)
    small_g = [None] * depth
    reduced = [dict() for _ in range(depth)]
    riding = {}
    for l in reversed(range(depth)):
        dx, big, small_g[l], got = _layer_bwd(dx, params[l], saved[l], n_heads, riding)
        for n, t in got.items():
            reduced[l + 1][n] = _finish_reduce(n, t, w[n].shape[1:], slot)
        riding = _layer_parts(big) if l > 0 else {}
        if l == 0:
            for n, part in _layer_parts(big).items():
                reduced[0][n] = _finish_reduce(n, _all_to_all(part, name="scatter_" + n), w[n].shape[1:], slot)
    grads = {n: jnp.stack([reduced[l][n] for l in range(depth)]) for n in BIG}

    small_shapes = {n: ((depth,) + small_g[0][n].shape if n != "norm_final_w" else (d,)) for n in SMALL}
    small_local = [jnp.stack([small_g[l][n] for l in range(depth)]) for n in SMALL[:-1]] + [jnp.sum(dnf_p, axis=0)]
    sp = _pack_small(small_local, 512)
    sg = _allgather(sp, name="gather_small").reshape(8, sp.shape[0], 128)
    small_sum = _unpack_small(_sum_parts(sg, name="sum_small"), [small_shapes[n] for n in SMALL])
    for n, g in zip(SMALL, small_sum):
        if n in ("conv_qkv_w", "conv_ffn_w"):
            cols = w[n].shape[-1]
            g = lax.dynamic_slice_in_dim(g, chip * cols, cols, axis=2)
        grads[n] = g

    delta, new_m, new_v = {}, {}, {}
    for n in BIG:
        shp = w[n].shape
        r2 = lambda t: t.reshape(-1, shp[-1])
        dd, mm, vv = _adamw(r2(w[n]), r2(grads[n]), r2(m[n]), r2(v[n]), name="adamw_" + n)
        delta[n], new_m[n], new_v[n] = dd.reshape(shp), mm.reshape(shp), vv.reshape(shp)
    pk = lambda src: _pack_small([src[n] for n in SMALL], 8)
    outs = _adamw(pk(w), pk(grads), pk(m), pk(v), name="adamw_small")
    for dst, buf in zip((delta, new_m, new_v), outs):
        for n, t in zip(SMALL, _unpack_small(buf, [w[n].shape for n in SMALL])):
            dst[n] = t

    return (loss, dx[None], *[grads[n] for n in WEIGHTS], *[delta[n] for n in WEIGHTS],
            *[new_m[n] for n in WEIGHTS], *[new_v[n] for n in WEIGHTS])
```

```python
import functools
import math

import jax
import jax.numpy as jnp
from jax import lax
from jax.experimental import pallas as pl
from jax.experimental.pallas import tpu as pltpu

F32 = jnp.float32
BF16 = jnp.bfloat16
EPS = 1e-6
CHUNK = 64
HEAD_DIM = 128
POOL_WINDOWS = (2, 4, 8, 16)
HALO = 16
ADAM_LR, ADAM_B1, ADAM_B2, ADAM_EPS, ADAM_WD, ADAM_STEP = 0.001, 0.9, 0.999, 1e-08, 0.01, 10
V7X_VMEM_LIMIT = 56 * 1024 * 1024
SUM_BLOCK_BYTES = 6 * 1024 * 1024
MAX_COPY_BYTES = 8 * 1024 * 1024
MATMUL_TILE_BYTES = 10 * 1024 * 1024
MESH = pl.DeviceIdType.MESH


def _tile(n, cap, align=128):
    if n <= cap:
        return n
    t = (cap // align) * align
    while t >= align:
        if n % t == 0:
            return t
        t -= align
    return n


def _params(*sem):
    return pltpu.CompilerParams(dimension_semantics=sem, vmem_limit_bytes=V7X_VMEM_LIMIT)


def _sigmoid(x):
    return 0.5 * (1.0 + jnp.tanh(0.5 * x))


def _down(x, j):
    return pltpu.roll(x, j, 0)


def _up(x, j):
    return pltpu.roll(x, x.shape[0] - j, 0)


def _fold8(x):
    n, c = x.shape
    return jnp.sum(x.reshape(n // 8, 8, c), axis=0)


def _matmul(a, b, *, ta=False, tb=False, add=None, out_dtype=F32, b_k0=0, b_n0=0, n=None, a_part=None, b_part=None, name):
    a2, b2 = a.shape[-2:], b.shape[-2:]
    m, k = (a2[1], a2[0]) if ta else a2
    if n is None:
        n = b2[0] if tb else b2[1]
    tm, tn = _tile(m, 1024), _tile(n, 1024)
    per_k = tm * a.dtype.itemsize + tn * b.dtype.itemsize
    tk = _tile(k, max(128, MATMUL_TILE_BYTES // per_k // 128 * 128))
    nk = k // tk
    assert b_k0 % tk == 0 and b_n0 % tn == 0, (b_k0, b_n0, tk, tn)
    ko, no = b_k0 // tk, b_n0 // tn

    def spec(shape, index, part):
        if part is None:
            return pl.BlockSpec(shape, index)
        return pl.BlockSpec((None,) + shape, lambda i, j, kk: (part,) + index(i, j, kk))

    a_spec = spec((tk, tm), lambda i, j, kk: (kk, i), a_part) if ta else spec((tm, tk), lambda i, j, kk: (i, kk), a_part)
    b_spec = (spec((tn, tk), lambda i, j, kk: (j + no, kk + ko), b_part) if tb
              else spec((tk, tn), lambda i, j, kk: (kk + ko, j + no), b_part))
    o_spec = pl.BlockSpec((tm, tn), lambda i, j, kk: (i, j))
    dims = (((0 if ta else 1,), (1 if tb else 0,)), ((), ()))
    has_add = add is not None

    def body(*refs):
        a_ref, b_ref = refs[:2]
        add_ref = refs[2] if has_add else None
        o_ref = refs[3 if has_add else 2]
        part = lax.dot_general(a_ref[...].astype(BF16), b_ref[...].astype(BF16), dims, preferred_element_type=F32)

        def finish(r):
            if has_add:
                r = r + add_ref[...]
            o_ref[...] = r.astype(out_dtype)

        if nk == 1:
            finish(part)
            return
        acc = refs[-1]
        kk = pl.program_id(2)

        @pl.when(kk == 0)
        def _():
            acc[...] = part

        @pl.when(kk > 0)
        def _():
            acc[...] += part

        @pl.when(kk == nk - 1)
        def _():
            finish(acc[...])

    ins = [a, b] + ([add] if has_add else [])
    specs = [a_spec, b_spec] + ([o_spec] if has_add else [])
    return pl.pallas_call(
        body, name=name, grid=(m // tm, n // tn, nk), in_specs=specs, out_specs=o_spec,
        out_shape=jax.ShapeDtypeStruct((m, n), out_dtype),
        scratch_shapes=[pltpu.VMEM((tm, tn), F32)] if nk > 1 else [],
        compiler_params=_params("parallel", "parallel", "arbitrary"))(*ins)


def _rmsnorm_fwd(x, w, *, name):
    s, d = x.shape
    ts = _tile(s, 512, 8)

    def body(x_ref, w_ref, o_ref):
        xv = x_ref[...]
        r = lax.rsqrt(jnp.mean(xv * xv, axis=-1, keepdims=True) + EPS)
        o_ref[...] = (xv * r * w_ref[...]).astype(BF16)

    return pl.pallas_call(
        body, name=name, grid=(s // ts,),
        in_specs=[pl.BlockSpec((ts, d), lambda i: (i, 0)), pl.BlockSpec((1, d), lambda i: (0, 0))],
        out_specs=pl.BlockSpec((ts, d), lambda i: (i, 0)),
        out_shape=jax.ShapeDtypeStruct((s, d), BF16), compiler_params=_params("parallel"))(x, w)


def _rmsnorm_bwd(x, w, dh, dres, *, name):
    s, d = x.shape
    ts = _tile(s, 512, 8)

    def body(x_ref, w_ref, dh_ref, dres_ref, dx_ref, dw_ref):
        xv = x_ref[...]
        r = lax.rsqrt(jnp.mean(xv * xv, axis=-1, keepdims=True) + EPS)
        xh = xv * r
        dhv = dh_ref[...]
        dxh = dhv * w_ref[...]
        dx_ref[...] = dres_ref[...] + r * (dxh - xh * jnp.mean(dxh * xh, axis=-1, keepdims=True))

        @pl.when(pl.program_id(0) == 0)
        def _():
            dw_ref[...] = jnp.zeros_like(dw_ref)

        dw_ref[...] += _fold8(dhv * xh)

    row = pl.BlockSpec((ts, d), lambda i: (i, 0))
    return pl.pallas_call(
        body, name=name, grid=(s // ts,),
        in_specs=[row, pl.BlockSpec((1, d), lambda i: (0, 0)), row, row],
        out_specs=[row, pl.BlockSpec((8, d), lambda i: (0, 0))],
        out_shape=[jax.ShapeDtypeStruct((s, d), F32), jax.ShapeDtypeStruct((8, d), F32)],
        compiler_params=_params("arbitrary"))(x, w, dh, dres)


def _final_loss(x, w, target, *, name):
    s, d = x.shape
    ts = _tile(s, 512, 8)

    def body(x_ref, w_ref, t_ref, dx_ref, dw_ref, loss_ref):
        xv = x_ref[...]
        r = lax.rsqrt(jnp.mean(xv * xv, axis=-1, keepdims=True) + EPS)
        xh = xv * r
        wv = w_ref[...]
        err = xh * wv - t_ref[...]
        dy = err * (1.0 / d)
        dxh = dy * wv
        dx_ref[...] = r * (dxh - xh * jnp.mean(dxh * xh, axis=-1, keepdims=True))

        @pl.when(pl.program_id(0) == 0)
        def _():
            dw_ref[...] = jnp.zeros_like(dw_ref)
            loss_ref[...] = jnp.zeros_like(loss_ref)

        dw_ref[...] += _fold8(dy * xh)
        e2 = _fold8(err * err)
        part = e2[:, 0:128]
        for j in range(1, d // 128):
            part = part + e2[:, j * 128:(j + 1) * 128]
        loss_ref[...] += part

    row = pl.BlockSpec((ts, d), lambda i: (i, 0))
    return pl.pallas_call(
        body, name=name, grid=(s // ts,),
        in_specs=[row, pl.BlockSpec((1, d), lambda i: (0, 0)), row],
        out_specs=[row, pl.BlockSpec((8, d), lambda i: (0, 0)), pl.BlockSpec((8, 128), lambda i: (0, 0))],
        out_shape=[jax.ShapeDtypeStruct((s, d), F32), jax.ShapeDtypeStruct((8, d), F32),
                   jax.ShapeDtypeStruct((8, 128), F32)],
        compiler_params=_params("arbitrary"))(x, w, target)


def _prev_spec(ts, tc, col):
    return pl.BlockSpec((HALO, tc), lambda i, j: (jnp.maximum(i * (ts // HALO) - 1, 0), col(j)))


def _next_spec(ts, tc, col, n_tiles):
    return pl.BlockSpec((HALO, tc), lambda i, j: (jnp.minimum((i + 1) * (ts // HALO), n_tiles * (ts // HALO) - 1), col(j)))


def _with_prev(prev, cur, first):
    return jnp.concatenate([jnp.where(first, 0.0, prev), cur], axis=0)


def _with_next(cur, nxt, last):
    return jnp.concatenate([cur, jnp.where(last, 0.0, nxt)], axis=0)


def _gelu(x):
    return 0.5 * x * (1.0 + lax.erf(x * (1.0 / math.sqrt(2.0))))


def _gelu_grad(x):
    return 0.5 * (1.0 + lax.erf(x * (1.0 / math.sqrt(2.0)))) + x * jnp.exp(-0.5 * x * x) * (1.0 / math.sqrt(2.0 * math.pi))


def _ffn_conv(prev, cur, w, first):
    xx = _with_prev(prev, cur, first)
    return w[2:3] * cur + w[1:2] * _down(xx, 1)[HALO:] + w[0:1] * _down(xx, 2)[HALO:]


def _ffn_mid_fwd(gu, cw, cb, *, name):
    s, f2 = gu.shape
    f = f2 // 2
    ts, tc = _tile(s, 512, HALO), _tile(f, 512)
    nj = f // tc

    def body(g_ref, gp_ref, u_ref, w_ref, b_ref, o_ref):
        first = pl.program_id(0) == 0
        gc = _ffn_conv(gp_ref[...], g_ref[...], w_ref[...], first) + b_ref[...]
        o_ref[...] = (_gelu(gc) * u_ref[...]).astype(BF16)

    return pl.pallas_call(
        body, name=name, grid=(s // ts, nj),
        in_specs=[pl.BlockSpec((ts, tc), lambda i, j: (i, j)), _prev_spec(ts, tc, lambda j: j),
                  pl.BlockSpec((ts, tc), lambda i, j: (i, j + nj)),
                  pl.BlockSpec((3, tc), lambda i, j: (0, j)), pl.BlockSpec((1, tc), lambda i, j: (0, j))],
        out_specs=pl.BlockSpec((ts, tc), lambda i, j: (i, j)),
        out_shape=jax.ShapeDtypeStruct((s, f), BF16), compiler_params=_params("parallel", "parallel"))(gu, gu, gu, cw, cb)


def _ffn_mid_bwd(gu, cw, cb, dact, scatter=(), *, name):
    s, f2 = gu.shape
    f = f2 // 2
    ts, tc = _tile(s, 512, HALO), _tile(f, 512)
    nj, ni = f // tc, s // ts
    n_x = len(scatter)

    def body(*refs):
        g_ref, gp_ref, gn_ref, u_ref, un_ref, d_ref, dn_ref, w_ref, b_ref = refs[:9]
        dg_ref, du_ref, dw_ref, db_ref = refs[9 + n_x:13 + n_x]
        riders = (_all_to_all_step, refs[9:9 + n_x], refs[13 + n_x:13 + 2 * n_x], refs[13 + 2 * n_x:], (ni, nj))
        _ride(*riders, start=True)
        i = pl.program_id(0)
        first, last = i == 0, i == ni - 1
        w, b = w_ref[...], b_ref[...]
        g = g_ref[...]
        gx = jnp.concatenate([jnp.where(first, 0.0, gp_ref[...]), g, jnp.where(last, 0.0, gn_ref[...])], axis=0)
        g1, g2 = _down(gx, 1), _down(gx, 2)
        gc = (w[2:3] * gx + w[1:2] * g1 + w[0:1] * g2)[HALO:] + b
        ux = _with_next(u_ref[...], un_ref[...], last)
        dx = _with_next(d_ref[...], dn_ref[...], last)
        dgc = dx * ux * _gelu_grad(gc)
        du_ref[...] = (dx[:ts] * _gelu(gc[:ts])).astype(BF16)
        dg = w[2:3] * dgc + w[1:2] * _up(dgc, 1) + w[0:1] * _up(dgc, 2)
        dg_ref[...] = dg[:ts].astype(BF16)
        dgt = dgc[:ts]
        db_ref[...] = _fold8(dgt)
        dw_ref[:, 0:tc] = _fold8(dgt * g2[HALO:HALO + ts])
        dw_ref[:, tc:2 * tc] = _fold8(dgt * g1[HALO:HALO + ts])
        dw_ref[:, 2 * tc:3 * tc] = _fold8(dgt * g)
        _ride(*riders, start=False)

    cur = lambda off: pl.BlockSpec((ts, tc), lambda i, j: (i, j + off))
    outs = pl.pallas_call(
        body, name=name, grid=(ni, nj),
        in_specs=[cur(0), _prev_spec(ts, tc, lambda j: j), _next_spec(ts, tc, lambda j: j, ni),
                  cur(nj), _next_spec(ts, tc, lambda j: j + nj, ni),
                  cur(0), _next_spec(ts, tc, lambda j: j, ni),
                  pl.BlockSpec((3, tc), lambda i, j: (0, j)), pl.BlockSpec((1, tc), lambda i, j: (0, j))] + [_HBM] * n_x,
        out_specs=[cur(0), cur(0),
                   pl.BlockSpec((8, 3 * tc), lambda i, j: (i, j)), pl.BlockSpec((8, tc), lambda i, j: (i, j))] + [_HBM] * n_x,
        out_shape=[jax.ShapeDtypeStruct((s, f), BF16), jax.ShapeDtypeStruct((s, f), BF16),
                   jax.ShapeDtypeStruct((ni * 8, 3 * f), F32), jax.ShapeDtypeStruct((ni * 8, f), F32)]
        + [jax.ShapeDtypeStruct(t.shape, t.dtype) for t in scatter],
        scratch_shapes=_EXCHANGE_SEMS * n_x,
        compiler_params=_params(*(("arbitrary", "arbitrary") if n_x else ("parallel", "parallel"))))(
            gu, gu, gu, gu, gu, dact, dact, cw, cb, *scatter)
    return outs[0], outs[1], outs[2], outs[3], list(outs[4:])


def _qkv_fwd(pq, cw, n_heads, *, name):
    s, c3 = pq.shape
    ts = _tile(s, 512, HALO)

    def body(x_ref, xp_ref, w_ref, o_ref):
        j = pl.program_id(1)
        w = w_ref[...]
        x = x_ref[...]
        xx = _with_prev(xp_ref[...], x, pl.program_id(0) == 0)
        y = w[3:4] * x + w[2:3] * _down(xx, 1)[HALO:] + w[1:2] * _down(xx, 2)[HALO:] + w[0:1] * _down(xx, 3)[HALO:]
        c = y * _sigmoid(y)
        r = lax.rsqrt(jnp.sum(c * c, axis=-1, keepdims=True) + EPS)
        scale = jnp.where(j < n_heads, HEAD_DIM ** -0.5, 1.0)
        o_ref[...] = jnp.where(j < 2 * n_heads, c * (r * scale), c)

    return pl.pallas_call(
        body, name=name, grid=(s // ts, 3 * n_heads),
        in_specs=[pl.BlockSpec((ts, HEAD_DIM), lambda i, j: (i, j)), _prev_spec(ts, HEAD_DIM, lambda j: j),
                  pl.BlockSpec((4, HEAD_DIM), lambda i, j: (0, j))],
        out_specs=pl.BlockSpec((None, ts, HEAD_DIM), lambda i, j: (j // n_heads, i, j % n_heads)),
        out_shape=jax.ShapeDtypeStruct((3, s, c3 // 3), F32),
        compiler_params=_params("parallel", "parallel"))(pq, pq, cw)


def _qkv_bwd(pq, cw, dqkv3, n_heads, scatter=(), *, name):
    s, c3 = pq.shape
    ts = _tile(s, 512, HALO)
    ni = s // ts
    hd = HEAD_DIM

    n_x = len(scatter)

    def body(*refs):
        x_ref, xp_ref, xn_ref, w_ref, d_ref, dn_ref = refs[:6]
        dx_ref, dw_ref = refs[6 + n_x:8 + n_x]
        riders = (_all_to_all_step, refs[6:6 + n_x], refs[8 + n_x:8 + 2 * n_x], refs[8 + 2 * n_x:], (ni, 3 * n_heads))
        _ride(*riders, start=True)
        i, j = pl.program_id(0), pl.program_id(1)
        first, last = i == 0, i == ni - 1
        w = w_ref[...]
        x = x_ref[...]
        xx = jnp.concatenate([jnp.where(first, 0.0, xp_ref[...]), x, jnp.where(last, 0.0, xn_ref[...])], axis=0)
        x1, x2, x3 = _down(xx, 1), _down(xx, 2), _down(xx, 3)
        y = (w[3:4] * xx + w[2:3] * x1 + w[1:2] * x2 + w[0:1] * x3)[HALO:]
        sg = _sigmoid(y)
        c = y * sg
        dn = _with_next(d_ref[...], dn_ref[...], last)
        r = lax.rsqrt(jnp.sum(c * c, axis=-1, keepdims=True) + EPS)
        nrm = c * r
        dnn = dn * jnp.where(j < n_heads, hd ** -0.5, 1.0)
        dc = jnp.where(j < 2 * n_heads, r * (dnn - nrm * jnp.sum(dnn * nrm, axis=-1, keepdims=True)), dn)
        dy = dc * (sg * (1.0 + y * (1.0 - sg)))
        dx = w[3:4] * dy + w[2:3] * _up(dy, 1) + w[1:2] * _up(dy, 2) + w[0:1] * _up(dy, 3)
        dx_ref[...] = dx[:ts].astype(BF16)
        dyt = dy[:ts]
        dw_ref[:, 0:hd] = _fold8(dyt * x3[HALO:HALO + ts])
        dw_ref[:, hd:2 * hd] = _fold8(dyt * x2[HALO:HALO + ts])
        dw_ref[:, 2 * hd:3 * hd] = _fold8(dyt * x1[HALO:HALO + ts])
        dw_ref[:, 3 * hd:4 * hd] = _fold8(dyt * x)
        _ride(*riders, start=False)

    dspec = lambda rows, row_index: pl.BlockSpec(
        (None, rows, hd), lambda i, j: (j // n_heads, row_index(i), j % n_heads))
    outs = pl.pallas_call(
        body, name=name, grid=(ni, 3 * n_heads),
        in_specs=[pl.BlockSpec((ts, hd), lambda i, j: (i, j)), _prev_spec(ts, hd, lambda j: j),
                  _next_spec(ts, hd, lambda j: j, ni), pl.BlockSpec((4, hd), lambda i, j: (0, j)),
                  dspec(ts, lambda i: i),
                  dspec(HALO, lambda i: jnp.minimum((i + 1) * (ts // HALO), ni * (ts // HALO) - 1))] + [_HBM] * n_x,
        out_specs=[pl.BlockSpec((ts, hd), lambda i, j: (i, j)),
                   pl.BlockSpec((8, 4 * hd), lambda i, j: (i, j))] + [_HBM] * n_x,
        out_shape=[jax.ShapeDtypeStruct((s, c3), BF16), jax.ShapeDtypeStruct((ni * 8, 4 * c3), F32)]
        + [jax.ShapeDtypeStruct(t.shape, t.dtype) for t in scatter],
        scratch_shapes=_EXCHANGE_SEMS * n_x,
        compiler_params=_params(*(("arbitrary", "arbitrary") if n_x else ("parallel", "parallel"))))(
            pq, pq, pq, cw, dqkv3, dqkv3, *scatter)
    return outs[0], outs[1], list(outs[2:])


def _gate_terms(ba, al, dt, h, n_heads):
    lane = lax.broadcasted_iota(jnp.int32, ba.shape, 1)
    braw = jnp.sum(jnp.where(lane == h, ba, 0.0), axis=1, keepdims=True)
    araw = jnp.sum(jnp.where(lane == h + n_heads, ba, 0.0), axis=1, keepdims=True)
    beta = _sigmoid(braw)
    z = araw + dt
    sp = jnp.maximum(z, 0.0) + jnp.log(1.0 + jnp.exp(-jnp.abs(z)))
    ea = jnp.exp(jnp.zeros((1, 1), F32) + al)
    return beta, z, sp, ea


def _gates_fwd(ba, a_log, dt_bias, n_heads, *, name):
    s = ba.shape[0]
    ts = _tile(s, 512, CHUNK)

    def body(ba_ref, al_ref, dt_ref, o_ref):
        h = pl.program_id(1)
        beta, _, sp, ea = _gate_terms(ba_ref[...], al_ref[h], dt_ref[h], h, n_heads)
        gx = jnp.broadcast_to(-ea * sp, (ts, HEAD_DIM))
        rc = lax.broadcasted_iota(jnp.int32, (ts, HEAD_DIM), 0) & (CHUNK - 1)
        for sh in (1, 2, 4, 8, 16, 32):
            gx = gx + jnp.where(rc >= sh, _down(gx, sh), 0.0)
        o_ref[0] = jnp.broadcast_to(beta, (ts, HEAD_DIM))
        o_ref[1] = gx

    smem = pl.BlockSpec(memory_space=pltpu.SMEM)
    return pl.pallas_call(
        body, name=name, grid=(s // ts, n_heads),
        in_specs=[pl.BlockSpec((ts, 128), lambda i, h: (i, 0)), smem, smem],
        out_specs=pl.BlockSpec((2, ts, HEAD_DIM), lambda i, h: (0, i, h)),
        out_shape=jax.ShapeDtypeStruct((2, s, n_heads * HEAD_DIM), F32),
        compiler_params=_params("parallel", "parallel"))(ba, a_log, dt_bias)


def _gates_bwd(ba, a_log, dt_bias, dbg, n_heads, *, name):
    s = ba.shape[0]
    ts = _tile(s, 512, CHUNK)
    ni = s // ts

    def body(ba_ref, al_ref, dt_ref, d_ref, o_ref, p_ref):
        h = pl.program_id(1)
        beta, z, sp, ea = _gate_terms(ba_ref[...], al_ref[h], dt_ref[h], h, n_heads)
        dg = d_ref[1]
        rc = lax.broadcasted_iota(jnp.int32, (ts, HEAD_DIM), 0) & (CHUNK - 1)
        for sh in (1, 2, 4, 8, 16, 32):
            dg = dg + jnp.where(rc < CHUNK - sh, _up(dg, sh), 0.0)
        daraw = dg * (-ea * _sigmoid(z))
        dbraw = d_ref[0] * (beta * (1.0 - beta))

        @pl.when(h == 0)
        def _():
            o_ref[...] = jnp.zeros_like(o_ref)
            p_ref[...] = jnp.zeros_like(p_ref)

        lane = lax.broadcasted_iota(jnp.int32, (1, 128), 1)
        is_b, is_a = lane == h, lane == h + n_heads
        o_ref[...] += jnp.where(is_b, dbraw, 0.0) + jnp.where(is_a, daraw, 0.0)
        p_ref[...] += jnp.where(is_b, _fold8(dg * (-ea * sp)), 0.0) + jnp.where(is_a, _fold8(daraw), 0.0)

    smem = pl.BlockSpec(memory_space=pltpu.SMEM)
    return pl.pallas_call(
        body, name=name, grid=(ni, n_heads),
        in_specs=[pl.BlockSpec((ts, 128), lambda i, h: (i, 0)), smem, smem,
                  pl.BlockSpec((2, ts, HEAD_DIM), lambda i, h: (0, i, h))],
        out_specs=[pl.BlockSpec((ts, 128), lambda i, h: (i, 0)), pl.BlockSpec((8, 128), lambda i, h: (i, 0))],
        out_shape=[jax.ShapeDtypeStruct((s, 128), F32), jax.ShapeDtypeStruct((ni * 8, 128), F32)],
        compiler_params=_params("parallel", "arbitrary"))(ba, a_log, dt_bias, dbg)


BLK = 2 * CHUNK
HEADS_PER_STEP = 2
NN = (((1,), (0,)), ((), ()))
NT = (((1,), (1,)), ((), ()))
TN = (((0,), (0,)), ((), ()))


def _dot(a, b, dims):
    return lax.dot_general(a.astype(BF16), b.astype(BF16), dims, preferred_element_type=F32)


def _dot3(a, b, dims):
    ah, bh = a.astype(BF16), b.astype(BF16)
    al, bl = (a - ah.astype(F32)).astype(BF16), (b - bh.astype(F32)).astype(BF16)
    d = lambda u, v: lax.dot_general(u, v, dims, preferred_element_type=F32)
    return d(ah, bh) + (d(ah, bl) + d(al, bh))


def _pair_masks():
    row = lax.broadcasted_iota(jnp.int32, (BLK, BLK), 0)
    col = lax.broadcasted_iota(jnp.int32, (BLK, BLK), 1)
    same = (row < CHUNK) == (col < CHUNK)
    return same & (row >= col), same & (row > col), row == col


def _pair_terms(q, k, v, b, gam, masks):
    tril, strict, eye = masks
    g_cols = jnp.sum(jnp.where(eye, gam, 0.0), axis=0, keepdims=True)
    dmat = jnp.exp(jnp.where(tril, gam - g_cols, -jnp.inf))
    eg = jnp.exp(gam)
    rowi = lax.broadcasted_iota(jnp.int32, (BLK, HEAD_DIM), 0)
    elast = jnp.exp(jnp.where(rowi < CHUNK, gam[CHUNK - 1:CHUNK], gam[BLK - 1:BLK]) - gam)
    kb, vb = k * b, v * b
    kq = _dot(jnp.concatenate([kb, q], axis=0), k, NT)
    lmat = jnp.where(strict, kq[:BLK] * dmat, 0.0)
    attn = kq[BLK:] * dmat
    rhs = jnp.concatenate([vb, kb * eg], axis=1)
    return dmat, eg, elast, kb, lmat, attn, rhs


def _unit_lower_inverse(lmat, eye):
    p = -lmat
    t = jnp.where(eye, 1.0, 0.0) + p
    p = _dot(p, p, NN)
    for _ in range(4):
        r = _dot(jnp.concatenate([p, t], axis=0), p, NN)
        p, t = r[:BLK], t + r[BLK:]
    return t + _dot(t, p, NN)


def _ride(step, in_refs, out_refs, sems, n_steps, *, start):
    if not in_refs:
        return
    i, j = pl.program_id(0), pl.program_id(1)
    when = (i == 0) & (j == 0) if start else (i == n_steps[0] - 1) & (j == n_steps[1] - 1)

    @pl.when(when)
    def _():
        for t, (src, dst) in enumerate(zip(in_refs, out_refs)):
            step(src, dst, *sems[3 * t:3 * t + 3], start=start, finish=not start)


def _gdn_fwd(qkv3, bg, gather=(), *, name):
    _, s, dl = qkv3.shape
    n_heads = dl // HEAD_DIM
    sb = _tile(s, 1024, BLK)
    npair = sb // BLK
    c = CHUNK

    hpg = HEADS_PER_STEP if n_heads % HEADS_PER_STEP == 0 else 1
    wd = hpg * HEAD_DIM

    n_x = len(gather)
    n_steps = (n_heads // hpg, s // sb)

    def body(*refs):
        qkv_ref, bg_ref = refs[:2]
        o_ref, st_ref, ti_ref = refs[2 + n_x:5 + n_x]
        s_scr = refs[5 + 2 * n_x]
        _ride(_allgather_step, refs[2:2 + n_x], refs[5 + n_x:5 + 2 * n_x], refs[6 + 2 * n_x:], n_steps, start=True)

        @pl.when(pl.program_id(1) == 0)
        def _():
            s_scr[...] = jnp.zeros_like(s_scr)

        masks = _pair_masks()

        def head_pair(hh, p, rows, state):
            ls = slice(hh * HEAD_DIM, (hh + 1) * HEAD_DIM)
            q, k, v = qkv_ref[0, rows, ls], qkv_ref[1, rows, ls], qkv_ref[2, rows, ls]
            b, gam = bg_ref[0, rows, ls], bg_ref[1, rows, ls]
            _, eg, elast, _, lmat, attn, rhs = _pair_terms(q, k, v, b, gam, masks)
            tinv = _unit_lower_inverse(lmat, masks[2])
            ti_ref[hh, rows, :] = tinv
            sol = _dot3(tinv, rhs, NN)
            u, w = sol[:, :HEAD_DIM], sol[:, HEAD_DIM:]
            qd, ke = q * eg, k * elast
            st_ref[hh, 2 * p] = state
            wq = _dot(jnp.concatenate([w[:c], qd[:c]], axis=0), state, NN)
            vn_a, o_a = u[:c] - wq[:c], wq[c:]
            state = state * jnp.exp(gam[c - 1:c]) + _dot(ke[:c], vn_a, TN)
            st_ref[hh, 2 * p + 1] = state
            wq = _dot(jnp.concatenate([w[c:], qd[c:]], axis=0), state, NN)
            vn_b, o_b = u[c:] - wq[:c], wq[c:]
            state = state * jnp.exp(gam[BLK - 1:BLK]) + _dot(ke[c:], vn_b, TN)
            o_ref[rows, ls] = jnp.concatenate([o_a, o_b], axis=0) + _dot(attn, jnp.concatenate([vn_a, vn_b], axis=0), NN)
            return state

        def pair(p, states):
            rows = pl.ds(pl.multiple_of(p * BLK, BLK), BLK)
            return tuple(head_pair(hh, p, rows, states[hh]) for hh in range(hpg))

        states = lax.fori_loop(0, npair, pair, tuple(s_scr[hh] for hh in range(hpg)))
        for hh in range(hpg):
            s_scr[hh] = states[hh]
        _ride(_allgather_step, refs[2:2 + n_x], refs[5 + n_x:5 + 2 * n_x], refs[6 + 2 * n_x:], n_steps, start=False)

    outs = pl.pallas_call(
        body, name=name, grid=n_steps,
        in_specs=[pl.BlockSpec((3, sb, wd), lambda h, j: (0, j, h)),
                  pl.BlockSpec((2, sb, wd), lambda h, j: (0, j, h))] + [_HBM] * n_x,
        out_specs=[pl.BlockSpec((sb, wd), lambda h, j: (j, h)),
                   pl.BlockSpec((hpg, 2 * npair, HEAD_DIM, HEAD_DIM), lambda h, j: (h, j, 0, 0)),
                   pl.BlockSpec((hpg, sb, BLK), lambda h, j: (h, j, 0))] + [_HBM] * n_x,
        out_shape=[jax.ShapeDtypeStruct((s, dl), F32),
                   jax.ShapeDtypeStruct((n_heads, s // c, HEAD_DIM, HEAD_DIM), F32),
                   jax.ShapeDtypeStruct((n_heads, s, BLK), F32)] + [_allgather_shape(t) for t in gather],
        scratch_shapes=[pltpu.VMEM((hpg, HEAD_DIM, HEAD_DIM), F32)] + _EXCHANGE_SEMS * n_x,
        compiler_params=_params("arbitrary", "arbitrary"))(qkv3, bg, *gather)
    return outs[0], outs[1], outs[2], list(outs[3:])


def _gdn_bwd(qkv3, bg, st, ti, d_o, scatter=(), *, name):
    _, s, dl = qkv3.shape
    n_heads = dl // HEAD_DIM
    sb = _tile(s, 1024, BLK)
    npair, nsb = sb // BLK, s // sb
    c = CHUNK
    hpg = HEADS_PER_STEP if n_heads % HEADS_PER_STEP == 0 else 1
    wd = hpg * HEAD_DIM

    n_x = len(scatter)
    n_steps = (n_heads // hpg, nsb)

    def body(*refs):
        qkv_ref, bg_ref, st_ref, ti_ref, do_ref = refs[:5]
        dqkv_ref, dbg_ref = refs[5 + n_x:7 + n_x]
        ds_scr = refs[7 + 2 * n_x]
        _ride(_all_to_all_step, refs[5:5 + n_x], refs[7 + n_x:7 + 2 * n_x], refs[8 + 2 * n_x:], n_steps, start=True)

        @pl.when(pl.program_id(1) == 0)
        def _():
            ds_scr[...] = jnp.zeros_like(ds_scr)

        masks = _pair_masks()
        tril, strict, eye = masks
        rowc = lax.broadcasted_iota(jnp.int32, (BLK, 1), 0)

        def total(x):
            return jnp.sum(jnp.sum(x, axis=1, keepdims=True), axis=0, keepdims=True)

        def head_pair(hh, p, rows, ds2):
            ls = slice(hh * HEAD_DIM, (hh + 1) * HEAD_DIM)
            q, k, v = qkv_ref[0, rows, ls], qkv_ref[1, rows, ls], qkv_ref[2, rows, ls]
            b, gam = bg_ref[0, rows, ls], bg_ref[1, rows, ls]
            tinv, dout = ti_ref[hh, rows, :], do_ref[rows, ls]
            s0, s1 = st_ref[hh, 2 * p], st_ref[hh, 2 * p + 1]
            dmat, eg, elast, kb, lmat, attn, rhs = _pair_terms(q, k, v, b, gam, masks)
            sol = _dot3(tinv, rhs, NN)
            u, w = sol[:, :HEAD_DIM], sol[:, HEAD_DIM:]
            qd, ke = q * eg, k * elast
            dec_a, dec_b = jnp.exp(gam[c - 1:c]), jnp.exp(gam[BLK - 1:BLK])
            vn = u - jnp.concatenate([_dot(w[:c], s0, NN), _dot(w[c:], s1, NN)], axis=0)
            dvn_o = _dot(attn, dout, TN)
            dvn_b = dvn_o[c:] + _dot(ke[c:], ds2, NN)
            ds1 = _dot(qd[c:], dout[c:], TN) + ds2 * dec_b - _dot(w[c:], dvn_b, TN)
            dvn_a = dvn_o[:c] + _dot(ke[:c], ds1, NN)
            ds0 = _dot(qd[:c], dout[:c], TN) + ds1 * dec_a - _dot(w[:c], dvn_a, TN)
            dvn = jnp.concatenate([dvn_a, dvn_b], axis=0)
            dke = jnp.concatenate([_dot(vn[:c], ds1, NT), _dot(vn[c:], ds2, NT)], axis=0)
            dw = -jnp.concatenate([_dot(dvn_a, s0, NT), _dot(dvn_b, s1, NT)], axis=0)
            dqd = jnp.concatenate([_dot(dout[:c], s0, NT), _dot(dout[c:], s1, NT)], axis=0)
            dattn = jnp.where(tril, _dot(dout, vn, NT), 0.0)
            drhs = _dot(tinv, jnp.concatenate([dvn, dw], axis=1), TN)
            dl_ = jnp.where(strict, -_dot(drhs, sol, NT), 0.0)
            dm, dqk = dl_ * dmat, dattn * dmat
            dvb, drw = drhs[:, :HEAD_DIM], drhs[:, HEAD_DIM:]
            dkb = _dot(dm, k, NN) + drw * eg
            dq = _dot(dqk, k, NN) + dqd * eg
            dk = _dot(dm, kb, TN) + _dot(dqk, q, TN) + dke * elast + dkb * b
            dbeta = jnp.sum(dvb * v + dkb * k, axis=1, keepdims=True)
            e = dl_ * lmat + dattn * attn
            e_cols = jnp.sum(jnp.where(eye, jnp.sum(e, axis=0, keepdims=True), 0.0), axis=1, keepdims=True)
            dke_ke = dke * ke
            dgam = (jnp.sum(e, axis=1, keepdims=True) - e_cols
                    + jnp.sum(drw * (kb * eg) + dqd * qd - dke_ke, axis=1, keepdims=True))
            tot_a = total(dke_ke[:c]) + total(s0 * ds1) * dec_a[:, :1]
            tot_b = total(dke_ke[c:]) + total(s1 * ds2) * dec_b[:, :1]
            dgam = dgam + jnp.where(rowc == c - 1, tot_a, 0.0) + jnp.where(rowc == BLK - 1, tot_b, 0.0)
            dqkv_ref[0, rows, ls] = dq
            dqkv_ref[1, rows, ls] = dk
            dqkv_ref[2, rows, ls] = dvb * b
            dbg_ref[0, rows, ls] = jnp.broadcast_to(dbeta, (BLK, HEAD_DIM))
            dbg_ref[1, rows, ls] = jnp.broadcast_to(dgam, (BLK, HEAD_DIM))
            return ds0

        def pair(pp, dstates):
            p = npair - 1 - pp
            rows = pl.ds(pl.multiple_of(p * BLK, BLK), BLK)
            return tuple(head_pair(hh, p, rows, dstates[hh]) for hh in range(hpg))

        dstates = lax.fori_loop(0, npair, pair, tuple(ds_scr[hh] for hh in range(hpg)))
        for hh in range(hpg):
            ds_scr[hh] = dstates[hh]
        _ride(_all_to_all_step, refs[5:5 + n_x], refs[7 + n_x:7 + 2 * n_x], refs[8 + 2 * n_x:], n_steps, start=False)

    rev = lambda j: nsb - 1 - j
    outs = pl.pallas_call(
        body, name=name, grid=n_steps,
        in_specs=[pl.BlockSpec((3, sb, wd), lambda h, j: (0, rev(j), h)),
                  pl.BlockSpec((2, sb, wd), lambda h, j: (0, rev(j), h)),
                  pl.BlockSpec((hpg, 2 * npair, HEAD_DIM, HEAD_DIM), lambda h, j: (h, rev(j), 0, 0)),
                  pl.BlockSpec((hpg, sb, BLK), lambda h, j: (h, rev(j), 0)),
                  pl.BlockSpec((sb, wd), lambda h, j: (rev(j), h))] + [_HBM] * n_x,
        out_specs=[pl.BlockSpec((3, sb, wd), lambda h, j: (0, rev(j), h)),
                   pl.BlockSpec((2, sb, wd), lambda h, j: (0, rev(j), h))] + [_HBM] * n_x,
        out_shape=[jax.ShapeDtypeStruct((3, s, dl), F32), jax.ShapeDtypeStruct((2, s, dl), F32)]
        + [jax.ShapeDtypeStruct(t.shape, t.dtype) for t in scatter],
        scratch_shapes=[pltpu.VMEM((hpg, HEAD_DIM, HEAD_DIM), F32)] + _EXCHANGE_SEMS * n_x,
        compiler_params=_params("arbitrary", "arbitrary"))(qkv3, bg, st, ti, d_o, *scatter)
    return outs[0], outs[1], list(outs[2:])


N_GROUPS = len(POOL_WINDOWS)


def _pick(g, vals):
    out = vals[-1]
    for i in range(len(vals) - 2, -1, -1):
        out = jnp.where(g == i, vals[i], out)
    return out


def _head_norm(o, nw):
    hats, outs = [], []
    for h in range(o.shape[1] // HEAD_DIM):
        sl = slice(h * HEAD_DIM, (h + 1) * HEAD_DIM)
        oh = o[:, sl]
        r = lax.rsqrt(jnp.mean(oh * oh, axis=-1, keepdims=True) + EPS)
        hats.append((oh * r, r))
        outs.append(oh * r * nw[:, sl])
    return hats, jnp.concatenate(outs, axis=1) if len(outs) > 1 else outs[0]


def _pool_counts(g, t0, n):
    t = (lax.broadcasted_iota(jnp.int32, (n, 1), 0) + t0 + 1).astype(F32)
    return jnp.minimum(t, _pick(g, [float(w) for w in POOL_WINDOWS]))


def _pool(prev, cur, first, g, t0):
    s = _with_prev(prev, cur, first)
    sums = []
    for sh in (1, 2, 4, 8):
        s = s + _down(s, sh)
        sums.append(s)
    return _pick(g, sums)[HALO:] / _pool_counts(g, t0, cur.shape[0]) - cur


def _mix_specs(ts, gw, ni):
    seg = lambda k: pl.BlockSpec((ts, gw), lambda g, i: (i, k * N_GROUPS + g))
    per = ts // HALO
    prev = lambda k: pl.BlockSpec((HALO, gw), lambda g, i: (jnp.maximum(i * per - 1, 0), k * N_GROUPS + g))
    nxt = lambda k: pl.BlockSpec((HALO, gw), lambda g, i: (jnp.minimum((i + 1) * per, ni * per - 1), k * N_GROUPS + g))
    vec = pl.BlockSpec((1, gw), lambda g, i: (0, g))
    pw = pl.BlockSpec((None, gw, gw), lambda g, i: (g, 0, 0))
    return seg, prev, nxt, vec, pw


def _mix_fwd(o, zpg, nw, pw, ps, *, name):
    s, d = o.shape
    gw = d // N_GROUPS
    ts = _tile(s, 512, HALO)
    ni = s // ts
    seg, prev, _, vec, pwspec = _mix_specs(ts, gw, ni)

    def body(o_ref, z_ref, p_ref, pp_ref, ga_ref, gb_ref, nw_ref, pw_ref, ps_ref, out_ref):
        g, i = pl.program_id(0), pl.program_id(1)
        _, on = _head_norm(o_ref[...], nw_ref[...])
        z = z_ref[...]
        ya = on * (z * _sigmoid(z))
        pooled = _pool(pp_ref[...], p_ref[...], i == 0, g, i * ts)
        yb = _dot(pooled, pw_ref[...], NN) * ps_ref[...]
        out_ref[...] = (_sigmoid(ga_ref[...]) * ya + _sigmoid(gb_ref[...]) * yb).astype(BF16)

    return pl.pallas_call(
        body, name=name, grid=(N_GROUPS, ni),
        in_specs=[seg(0), seg(0), seg(1), prev(1), seg(2), seg(3), vec, pwspec, vec],
        out_specs=seg(0), out_shape=jax.ShapeDtypeStruct((s, d), BF16),
        compiler_params=_params("parallel", "parallel"))(o, zpg, zpg, zpg, zpg, zpg, nw, pw, ps)


def _mix_bwd(o, zpg, nw, pw, ps, dmix, *, name):
    s, d = o.shape
    gw = d // N_GROUPS
    ts = _tile(s, 512, HALO)
    ni = s // ts
    seg, prev, nxt, vec, pwspec = _mix_specs(ts, gw, ni)

    def body(o_ref, z_ref, p_ref, pp_ref, ga_ref, gb_ref, gbn_ref, nw_ref, pw_ref, ps_ref, dm_ref, dmn_ref,
             do_ref, d4_ref, dpw_ref, dnw_ref, dps_ref):
        g, i = pl.program_id(0), pl.program_id(1)
        last = i == ni - 1
        nw, ps, pwv = nw_ref[...], ps_ref[...], pw_ref[...]
        ov, z = o_ref[...], z_ref[...]
        hats, on = _head_norm(ov, nw)
        sz = _sigmoid(z)
        silu = z * sz
        ya = on * silu
        pooled = _pool(pp_ref[...], p_ref[...], i == 0, g, i * ts)
        yp = _dot(pooled, pwv, NN)
        sga, sgb = _sigmoid(ga_ref[...]), _sigmoid(gb_ref[...])
        dm = dm_ref[...]
        dya, dyb = dm * sga, dm * sgb
        d4_ref[2] = (dm * ya * (sga * (1.0 - sga))).astype(BF16)
        d4_ref[3] = (dm * (yp * ps) * (sgb * (1.0 - sgb))).astype(BF16)
        dps_ref[...] = _fold8(dyb * yp)
        dyp = dyb * ps

        @pl.when(i == 0)
        def _():
            dpw_ref[...] = jnp.zeros_like(dpw_ref)

        dpw_ref[...] += _dot(pooled, dyp, TN)
        dyp_next = jnp.where(last, 0.0, dmn_ref[...] * _sigmoid(gbn_ref[...]) * ps)
        dpool = _dot(jnp.concatenate([dyp, dyp_next], axis=0), pwv, NT)
        a = dpool / _pool_counts(g, i * ts, ts + HALO)
        sums = []
        for sh in (1, 2, 4, 8):
            a = a + _up(a, sh)
            sums.append(a)
        d4_ref[1] = (_pick(g, sums)[:ts] - dpool[:ts]).astype(BF16)
        d4_ref[0] = (dya * on * (sz * (1.0 + z * (1.0 - sz)))).astype(BF16)
        don = dya * silu
        dos, dnws = [], []
        for h, (ohat, r) in enumerate(hats):
            sl = slice(h * HEAD_DIM, (h + 1) * HEAD_DIM)
            dxh = don[:, sl] * nw[:, sl]
            dos.append(r * (dxh - ohat * jnp.mean(dxh * ohat, axis=-1, keepdims=True)))
            dnws.append(_fold8(don[:, sl] * ohat))
        do_ref[...] = jnp.concatenate(dos, axis=1) if len(dos) > 1 else dos[0]
        dnw_ref[...] = jnp.concatenate(dnws, axis=1) if len(dnws) > 1 else dnws[0]

    part = pl.BlockSpec((8, gw), lambda g, i: (i, g))
    return pl.pallas_call(
        body, name=name, grid=(N_GROUPS, ni),
        in_specs=[seg(0), seg(0), seg(1), prev(1), seg(2), seg(3), nxt(3), vec, pwspec, vec, seg(0), nxt(0)],
        out_specs=[seg(0), pl.BlockSpec((4, ts, gw), lambda g, i: (0, i, g)),
                   pl.BlockSpec((None, gw, gw), lambda g, i: (g, 0, 0)), part, part],
        out_shape=[jax.ShapeDtypeStruct((s, d), F32), jax.ShapeDtypeStruct((4, s, d), BF16),
                   jax.ShapeDtypeStruct((N_GROUPS, gw, gw), F32),
                   jax.ShapeDtypeStruct((ni * 8, d), F32), jax.ShapeDtypeStruct((ni * 8, d), F32)],
        compiler_params=_params("parallel", "arbitrary"))(o, zpg, zpg, zpg, zpg, zpg, zpg, nw, pw, ps, dmix, dmix)


def _adamw(w, g, m, v, *, name):
    r, c = w.shape
    tr = _tile(r, max(8, (1 << 19) // c // 8 * 8), 8)

    def body(w_ref, g_ref, m_ref, v_ref, d_ref, mo_ref, vo_ref):
        gv = g_ref[...]
        mn = ADAM_B1 * m_ref[...] + (1.0 - ADAM_B1) * gv
        vn = ADAM_B2 * v_ref[...] + (1.0 - ADAM_B2) * (gv * gv)
        m_hat = mn / (1.0 - ADAM_B1 ** ADAM_STEP)
        v_hat = vn / (1.0 - ADAM_B2 ** ADAM_STEP)
        d_ref[...] = -ADAM_LR * (m_hat / (jnp.sqrt(v_hat) + ADAM_EPS) + ADAM_WD * w_ref[...])
        mo_ref[...] = mn
        vo_ref[...] = vn

    blk = pl.BlockSpec((tr, c), lambda i: (i, 0))
    return pl.pallas_call(
        body, name=name, grid=(r // tr,), in_specs=[blk] * 4, out_specs=[blk] * 3,
        out_shape=[jax.ShapeDtypeStruct((r, c), F32)] * 3, compiler_params=_params("parallel"))(w, g, m, v)


def _sum_parts(x, *, slot=None, name):
    p, r, c = x.shape
    tc = 128 if c % 128 == 0 else c
    tr = _tile(r, max(16, SUM_BLOCK_BYTES // (p * tc * x.dtype.itemsize)), 16)

    def body(*refs):
        x_ref, o_ref = refs[-2:]
        acc = x_ref[0].astype(F32)
        for i in range(1, p):
            acc = acc + x_ref[i].astype(F32)
        o_ref[...] = acc

    if slot is None:
        return pl.pallas_call(
            body, name=name, grid=(r // tr, c // tc),
            in_specs=[pl.BlockSpec((p, tr, tc), lambda i, j: (0, i, j))],
            out_specs=pl.BlockSpec((tr, tc), lambda i, j: (i, j)),
            out_shape=jax.ShapeDtypeStruct((r, c), F32), compiler_params=_params("parallel", "parallel"))(x)
    return pl.pallas_call(
        body, name=name,
        grid_spec=pltpu.PrefetchScalarGridSpec(
            num_scalar_prefetch=1, grid=(r // tr, c // tc),
            in_specs=[pl.BlockSpec((p, tr, tc), lambda i, j, s: (0, i, j))],
            out_specs=pl.BlockSpec((None, tr, tc), lambda i, j, s: (s[0], i, j))),
        out_shape=jax.ShapeDtypeStruct((2, r, c), F32), compiler_params=_params("parallel", "parallel"))(slot, x)


_HBM = pl.BlockSpec(memory_space=pltpu.HBM)


def _place():
    return lax.axis_index("x"), lax.axis_index("y"), lax.axis_index("c")


def _allgather(x_shard, *, name):
    def body(x_ref, out_ref, send_sems, recv_sems, local_sem):
        _allgather_step(x_ref, out_ref, send_sems, recv_sems, local_sem, start=True, finish=True)

    return pl.pallas_call(
        body, name=name, out_shape=_allgather_shape(x_shard), in_specs=[_HBM], out_specs=_HBM,
        scratch_shapes=_EXCHANGE_SEMS)(x_shard)


_EXCHANGE_SEMS = [pltpu.SemaphoreType.DMA((7,)), pltpu.SemaphoreType.DMA((7,)), pltpu.SemaphoreType.DMA]


def _allgather_shape(x_shard):
    return jax.ShapeDtypeStruct((8 * x_shard.shape[0], x_shard.shape[1]), x_shard.dtype)


def _allgather_step(x_ref, out_ref, send_sems, recv_sems, local_sem, *, start, finish):
    m_per = x_ref.shape[0]
    x, y, c = _place()
    me, sibling = (x, y, c), (x, y, 1 - c)
    chips = [(1 - x, y), (x, 1 - y), (1 - x, 1 - y)]

    def rows(px, py, pc):
        return out_ref.at[pl.ds((4 * px + 2 * py + pc) * m_per, m_per), :]

    def copy(k, block, to, src=None):
        return pltpu.make_async_remote_copy(
            src_ref=rows(*block) if src is None else src, dst_ref=rows(*block),
            send_sem=send_sems.at[k], recv_sem=recv_sems.at[k], device_id=to, device_id_type=MESH)

    mine = pltpu.make_async_copy(x_ref, rows(*me), local_sem)
    first = [copy(0, me, sibling, src=x_ref)]
    first += [copy(1 + j, me, (*chip, c), src=x_ref) for j, chip in enumerate(chips)]
    if start:
        mine.start()
        for cp in first:
            cp.start()
    if finish:
        passed = [copy(4 + j, (*chip, c), sibling) for j, chip in enumerate(chips)]
        for j, chip in enumerate(chips):
            copy(1 + j, (*chip, c), me).wait_recv()
            passed[j].start()
        copy(0, sibling, me).wait_recv()
        for j, chip in enumerate(chips):
            copy(4 + j, (*chip, 1 - c), me).wait_recv()
        for cp in first + passed:
            cp.wait_send()
        mine.wait()


def _all_to_all(parts, *, name):
    def body(g_ref, out_ref, send_sems, recv_sems, local_sem):
        _all_to_all_step(g_ref, out_ref, send_sems, recv_sems, local_sem, start=True, finish=True)

    return pl.pallas_call(
        body, name=name, out_shape=jax.ShapeDtypeStruct(parts.shape, parts.dtype), in_specs=[_HBM], out_specs=_HBM,
        scratch_shapes=_EXCHANGE_SEMS)(parts)


def _all_to_all_step(g_ref, out_ref, send_sems, recv_sems, local_sem, *, start, finish):
    x, y, c = _place()
    me = 4 * x + 2 * y + c
    mine = pltpu.make_async_copy(g_ref.at[me], out_ref.at[me], local_sem)
    if start:
        mine.start()
    sends, peers = [], []
    for k in range(1, 8):
        px = 1 - x if k & 4 else x
        py = 1 - y if k & 2 else y
        pc = 1 - c if k & 1 else c
        peer = 4 * px + 2 * py + pc
        cp = pltpu.make_async_remote_copy(
            src_ref=g_ref.at[peer], dst_ref=out_ref.at[me], send_sem=send_sems.at[k - 1],
            recv_sem=recv_sems.at[k - 1], device_id=(px, py, pc), device_id_type=MESH)
        if start:
            cp.start()
        sends.append(cp)
        peers.append((peer, (px, py, pc)))
    if finish:
        for k, (peer, pid) in enumerate(peers):
            pltpu.make_async_remote_copy(
                src_ref=g_ref.at[peer], dst_ref=out_ref.at[peer], send_sem=send_sems.at[k],
                recv_sem=recv_sems.at[k], device_id=pid, device_id_type=MESH).wait_recv()
        for cp in sends:
            cp.wait_send()
        mine.wait()


def _share_halves(both, *, name):
    _, r, _ = both.shape
    n_split = 1
    while both.size // 2 * both.dtype.itemsize > n_split * MAX_COPY_BYTES and r % (2 * n_split * 16) == 0:
        n_split *= 2
    rs = r // n_split

    def body(in_ref, out_ref, send_sems, recv_sems):
        x, y, c = _place()

        def copy(k, slot):
            rows = pl.ds(k * rs, rs)
            return pltpu.make_async_remote_copy(
                src_ref=in_ref.at[slot, rows], dst_ref=out_ref.at[slot, rows], send_sem=send_sems.at[k],
                recv_sem=recv_sems.at[k], device_id=(x, y, 1 - c), device_id_type=MESH)

        sends = [copy(k, c) for k in range(n_split)]
        for cp in sends:
            cp.start()
        for k in range(n_split):
            copy(k, 1 - c).wait_recv()
        for cp in sends:
            cp.wait_send()

    return pl.pallas_call(
        body, name=name, out_shape=jax.ShapeDtypeStruct(both.shape, both.dtype), in_specs=[_HBM], out_specs=_HBM,
        input_output_aliases={0: 0},
        scratch_shapes=[pltpu.SemaphoreType.DMA((n_split,)), pltpu.SemaphoreType.DMA((n_split,))])(both)


def _piece_rows(shape):
    n = math.prod(shape)
    if n % 128 == 0:
        return n // 128, 128
    assert shape[-1] <= 128, shape
    return n // shape[-1], shape[-1]


def _pack_small(arrs, row_multiple):
    pieces = []
    for a in arrs:
        rows, lanes = _piece_rows(a.shape)
        t = a.astype(F32).reshape(rows, lanes)
        pieces.append(jnp.pad(t, ((0, -rows % 8), (0, 128 - lanes))))
    buf = jnp.concatenate(pieces, axis=0)
    return jnp.pad(buf, ((0, -buf.shape[0] % row_multiple), (0, 0)))


def _unpack_small(buf, shapes):
    out, off = [], 0
    for shp in shapes:
        rows, lanes = _piece_rows(shp)
        out.append(buf[off:off + rows, :lanes].reshape(shp))
        off += rows + (-rows % 8)
    return out


def _w_in_pieces(gathered, n_heads, *, name):
    d4, cw = gathered.shape
    d = d4 // N_CHIPS
    dl = n_heads * HEAD_DIM
    nba = 2 * n_heads
    tr = _tile(d, 128, 16)

    def body(g_ref, qkv_ref, zpg_ref, ba_ref):
        full = jnp.concatenate([g_ref[j] for j in range(N_CHIPS)], axis=1)
        qkv_ref[...] = full[:, :3 * dl]
        zpg_ref[...] = jnp.concatenate([full[:, 3 * dl:4 * dl], full[:, 4 * dl + nba:]], axis=1)
        ba_ref[...] = jnp.concatenate([full[:, 4 * dl:4 * dl + nba], jnp.zeros((tr, 128 - nba), BF16)], axis=1)

    row = lambda c: pl.BlockSpec((tr, c), lambda i: (i, 0))
    return pl.pallas_call(
        body, name=name, grid=(d // tr,),
        in_specs=[pl.BlockSpec((N_CHIPS, tr, cw), lambda i: (0, i, 0))],
        out_specs=[row(3 * dl), row(4 * d), row(128)],
        out_shape=[jax.ShapeDtypeStruct((d, 3 * dl), BF16), jax.ShapeDtypeStruct((d, 4 * d), BF16),
                   jax.ShapeDtypeStruct((d, 128), BF16)],
        compiler_params=_params("parallel"))(gathered.reshape(N_CHIPS, d, cw))


def _w_in_grad_parts(g_qkv, g_zpg, g_ba, n_heads, *, name):
    d = g_qkv.shape[0]
    cw = (g_qkv.shape[1] + 4 * d + 2 * n_heads) // N_CHIPS
    tr = _tile(d // 2, 128, 16)
    per_half = d // 2 // tr

    def body(a_ref, z_ref, p_ref, ga_ref, gb_ref, ba_ref, o_ref):
        full = jnp.concatenate([a_ref[...], z_ref[...], ba_ref[...][:, :2 * n_heads], p_ref[...], ga_ref[...],
                                gb_ref[...]], axis=1)
        for j in range(N_CHIPS):
            o_ref[j] = full[:, cw * j:cw * (j + 1)]

    row = lambda c: pl.BlockSpec((tr, c), lambda i: (i, 0))
    out = pl.pallas_call(
        body, name=name, grid=(d // tr,),
        in_specs=[row(g_qkv.shape[1]), row(d), row(d), row(d), row(d), row(128)],
        out_specs=pl.BlockSpec((N_CHIPS, None, tr, cw), lambda i: (0, i // per_half, i % per_half, 0)),
        out_shape=jax.ShapeDtypeStruct((N_CHIPS, 2, d // 2, cw), BF16),
        compiler_params=_params("parallel"))(g_qkv, *g_zpg, g_ba)
    return out.reshape(8, d // 2, cw)


def _layer_fwd(x, p, n_heads, gather=()):
    h = _rmsnorm_fwd(x, p["norm_mix_w"], name="norm_mix_fwd")
    pq = _matmul(h, p["w_qkv"], name="proj_qkv")
    zpg = _matmul(h, p["w_zpg"], name="proj_zpg")
    ba = _matmul(h, p["w_ba"], name="proj_ba")
    qkv3 = _qkv_fwd(pq, p["conv_qkv_w"], n_heads, name="qkv_fwd")
    bg = _gates_fwd(ba, p["a_log"], p["dt_bias"], n_heads, name="gates_fwd")
    o, st, ti, gathered = _gdn_fwd(qkv3, bg, gather, name="gdn_fwd")
    mixed = _mix_fwd(o, zpg, p["gdn_nw"], p["pool_w"], p["pool_scale"], name="mix_fwd")
    x1 = _matmul(mixed, p["w_out"], add=x, name="out_proj")
    h2 = _rmsnorm_fwd(x1, p["norm_ffn_w"], name="norm_ffn_fwd")
    gu = _matmul(h2, p["w_up"], name="up_proj")
    act = _ffn_mid_fwd(gu, p["conv_ffn_w"], p["conv_ffn_b"], name="ffn_mid_fwd")
    x2 = _matmul(act, p["w_down"], add=x1, name="down_proj")
    return x2, (x, h, pq, zpg, ba, o, st, ti, mixed, x1, h2, gu, act), gathered


def _layer_bwd(dx2, p, saved, n_heads, scatter=()):
    x, h, pq, zpg, ba, o, st, ti, mixed, x1, h2, gu, act = saved
    d = x.shape[1]
    f = act.shape[1]
    dact = _matmul(dx2, p["w_down"], tb=True, name="d_act")
    g_down = _matmul(act, dx2, ta=True, out_dtype=BF16, name="g_w_down")
    riders = lambda carrier: [scatter[n] for n in RIDERS[carrier] if scatter]
    scattered = {}
    dgate, dup, dcw_p, dcb_p, got = _ffn_mid_bwd(gu, p["conv_ffn_w"], p["conv_ffn_b"], dact, riders("ffn_mid_bwd"),
                                                 name="ffn_mid_bwd")
    scattered.update(zip(RIDERS["ffn_mid_bwd"], got))
    dh2 = _matmul(dgate, p["w_up"], tb=True, b_k0=0, name="d_h2_gate")
    dh2 = _matmul(dup, p["w_up"], tb=True, b_k0=f, add=dh2, name="d_h2_up")
    g_up = jnp.concatenate([_matmul(h2, dgate, ta=True, out_dtype=BF16, name="g_w_up_gate"),
                            _matmul(h2, dup, ta=True, out_dtype=BF16, name="g_w_up_up")], axis=1)
    dx1, dnf_p = _rmsnorm_bwd(x1, p["norm_ffn_w"], dh2, dx2, name="norm_ffn_bwd")
    dmix = _matmul(dx1, p["w_out"], tb=True, name="d_mixed")
    g_out = _matmul(mixed, dx1, ta=True, out_dtype=BF16, name="g_w_out")
    d_o, d4, g_pool, dnw_p, dps_p = _mix_bwd(o, zpg, p["gdn_nw"], p["pool_w"], p["pool_scale"], dmix, name="mix_bwd")
    qkv3 = _qkv_fwd(pq, p["conv_qkv_w"], n_heads, name="qkv_fwd")
    bg = _gates_fwd(ba, p["a_log"], p["dt_bias"], n_heads, name="gates_fwd")
    dqkv3, dbg, got = _gdn_bwd(qkv3, bg, st, ti, d_o, riders("gdn_bwd"), name="gdn_bwd")
    scattered.update(zip(RIDERS["gdn_bwd"], got))
    dpq, dcq_p, got = _qkv_bwd(pq, p["conv_qkv_w"], dqkv3, n_heads, riders("qkv_bwd"), name="qkv_bwd")
    scattered.update(zip(RIDERS["qkv_bwd"], got))
    dba, dgate_p = _gates_bwd(ba, p["a_log"], p["dt_bias"], dbg, n_heads, name="gates_bwd")
    dh = _matmul(dpq, p["w_qkv"], tb=True, name="d_h_qkv")
    for seg in range(4):
        dh = _matmul(d4, p["w_zpg"], tb=True, a_part=seg, b_k0=seg * d, add=dh, name="d_h_zpg")
    dh = _matmul(dba, p["w_ba"], tb=True, add=dh, name="d_h_ba")
    g_qkv = _matmul(h, dpq, ta=True, out_dtype=BF16, name="g_w_qkv")
    g_zpg = [_matmul(h, d4, ta=True, b_part=seg, out_dtype=BF16, name="g_w_zpg") for seg in range(4)]
    g_ba = _matmul(h, dba, ta=True, out_dtype=BF16, name="g_w_ba")
    dx, dnm_p = _rmsnorm_bwd(x, p["norm_mix_w"], dh, dx1, name="norm_mix_bwd")
    g_in = _w_in_grad_parts(g_qkv, g_zpg, g_ba, n_heads, name="g_w_in_parts")
    rows = lambda t: jnp.sum(t, axis=0)
    ni8, c12 = dcq_p.shape
    nj = dcw_p.shape[1] // (3 * _tile(f, 512))
    small = {
        "norm_mix_w": rows(dnm_p),
        "conv_qkv_w": rows(dcq_p).reshape(c12 // (4 * HEAD_DIM), 4, HEAD_DIM).transpose(1, 0, 2).reshape(4, c12 // 4),
        "a_log": rows(dgate_p)[:n_heads],
        "dt_bias": rows(dgate_p)[n_heads:2 * n_heads],
        "gdn_norm_w": jnp.sum(rows(dnw_p).reshape(d // HEAD_DIM, HEAD_DIM), axis=0),
        "pool_scale": rows(dps_p),
        "norm_ffn_w": rows(dnf_p),
        "conv_ffn_w": rows(dcw_p).reshape(nj, 3, f // nj).transpose(1, 0, 2).reshape(3, f),
        "conv_ffn_b": rows(dcb_p),
    }
    big = {"w_in": g_in, "pool_w": g_pool, "w_out": g_out, "w_up": g_up, "w_down": g_down}
    return dx, big, small, scattered


BIG = ("w_in", "pool_w", "w_out", "w_up", "w_down")
SMALL = ("norm_mix_w", "conv_qkv_w", "a_log", "dt_bias", "gdn_norm_w", "pool_scale", "norm_ffn_w", "conv_ffn_w",
         "conv_ffn_b", "norm_final_w")
WEIGHTS = ("norm_mix_w", "w_in", "conv_qkv_w", "a_log", "dt_bias", "gdn_norm_w", "pool_w", "pool_scale", "w_out",
           "norm_ffn_w", "w_up", "conv_ffn_w", "conv_ffn_b", "w_down", "norm_final_w")
N_CHIPS = 4
RIDERS = {"ffn_mid_bwd": ("w_down",), "gdn_bwd": ("w_in", "pool_w", "w_out"), "qkv_bwd": ("w_up",)}


def _my_half(local, cc):
    m = local.shape[0] // 2
    return lax.dynamic_slice_in_dim(local, cc * m, m, axis=0)


def _weight_halves(w, l, cc):
    pw = w["pool_w"][l].astype(BF16)
    local = dict(w_in=w["w_in"][l].astype(BF16), pool_w=pw.reshape(-1, pw.shape[-1]),
                 w_out=w["w_out"][l].astype(BF16), w_up=w["w_up"][l].astype(BF16), w_down=w["w_down"][l].astype(BF16))
    return [_my_half(local[n], cc) for n in BIG]


def _full_weights(w, gathered):
    g_in, g_pool, g_out, g_up, g_down = gathered
    d = w["w_in"].shape[1]
    g, r, c = w["pool_w"].shape[1:]
    return dict(
        w_in=g_in,
        pool_w=g_pool.reshape(N_CHIPS, g, r, c).transpose(1, 0, 2, 3).reshape(g, N_CHIPS * r, c),
        w_out=g_out, w_up=g_up.reshape(N_CHIPS, d, -1).transpose(1, 0, 2).reshape(d, -1), w_down=g_down)


def _grad_parts(name, g):
    if name in ("w_in", "w_up"):
        r, c = g.shape
        return g.reshape(2, r // 2, N_CHIPS, c // N_CHIPS).transpose(2, 0, 1, 3).reshape(8, r // 2, c // N_CHIPS)
    if name == "pool_w":
        ng, r, c = g.shape
        t = g.reshape(2, ng // 2, N_CHIPS, r // N_CHIPS, c).transpose(2, 0, 1, 3, 4)
        return t.reshape(8, (ng // 2) * (r // N_CHIPS), c)
    r, c = g.shape
    return g.reshape(8, r // 8, c)


def _layer_parts(big):
    return {n: big[n] if n == "w_in" else _grad_parts(n, big[n].astype(BF16)) for n in BIG}


def _finish_reduce(name, got, shard_shape, slot):
    both = _share_halves(_sum_parts(got, slot=slot, name="sum_" + name), name="share_" + name)
    return both.reshape(shard_shape)


def kernel(x, norm_mix_w, w_in, conv_qkv_w, a_log, dt_bias, gdn_norm_w, pool_w, pool_scale, w_out, norm_ffn_w, w_up, conv_ffn_w, conv_ffn_b, w_down, norm_final_w, loss_target, m_norm_mix_w, m_w_in, m_conv_qkv_w, m_a_log, m_dt_bias, m_gdn_norm_w, m_pool_w, m_pool_scale, m_w_out, m_norm_ffn_w, m_w_up, m_conv_ffn_w, m_conv_ffn_b, m_w_down, m_norm_final_w, v_norm_mix_w, v_w_in, v_conv_qkv_w, v_a_log, v_dt_bias, v_gdn_norm_w, v_pool_w, v_pool_scale, v_w_out, v_norm_ffn_w, v_w_up, v_conv_ffn_w, v_conv_ffn_b, v_w_down, v_norm_final_w):
    w = dict(norm_mix_w=norm_mix_w, w_in=w_in, conv_qkv_w=conv_qkv_w, a_log=a_log, dt_bias=dt_bias,
             gdn_norm_w=gdn_norm_w, pool_w=pool_w, pool_scale=pool_scale, w_out=w_out, norm_ffn_w=norm_ffn_w,
             w_up=w_up, conv_ffn_w=conv_ffn_w, conv_ffn_b=conv_ffn_b, w_down=w_down, norm_final_w=norm_final_w)
    m = dict(norm_mix_w=m_norm_mix_w, w_in=m_w_in, conv_qkv_w=m_conv_qkv_w, a_log=m_a_log, dt_bias=m_dt_bias,
             gdn_norm_w=m_gdn_norm_w, pool_w=m_pool_w, pool_scale=m_pool_scale, w_out=m_w_out,
             norm_ffn_w=m_norm_ffn_w, w_up=m_w_up, conv_ffn_w=m_conv_ffn_w, conv_ffn_b=m_conv_ffn_b,
             w_down=m_w_down, norm_final_w=m_norm_final_w)
    v = dict(norm_mix_w=v_norm_mix_w, w_in=v_w_in, conv_qkv_w=v_conv_qkv_w, a_log=v_a_log, dt_bias=v_dt_bias,
             gdn_norm_w=v_gdn_norm_w, pool_w=v_pool_w, pool_scale=v_pool_scale, w_out=v_w_out,
             norm_ffn_w=v_norm_ffn_w, w_up=v_w_up, conv_ffn_w=v_conv_ffn_w, conv_ffn_b=v_conv_ffn_b,
             w_down=v_w_down, norm_final_w=v_norm_final_w)
    depth, n_heads = a_log.shape
    d = x.shape[-1]
    dl = n_heads * HEAD_DIM
    assert dl == d and gdn_norm_w.shape[1] == HEAD_DIM
    cx, cy, cc = _place()
    chip = 2 * cx + cy

    conv_packed = _pack_small([conv_qkv_w, conv_ffn_w], 16)
    conv_all = _allgather(conv_packed, name="gather_conv").reshape(N_CHIPS, 2, -1, 128)[:, 0]
    conv_j = [_unpack_small(conv_all[j], [conv_qkv_w.shape, conv_ffn_w.shape]) for j in range(N_CHIPS)]
    conv_q = jnp.concatenate([t[0] for t in conv_j], axis=-1)
    conv_f = jnp.concatenate([t[1] for t in conv_j], axis=-1)

    def layer_params(l, full):
        w_qkv, w_zpg, w_ba = _w_in_pieces(full["w_in"], n_heads, name="w_in_pieces")
        return dict(
            norm_mix_w=norm_mix_w[l][None], norm_ffn_w=norm_ffn_w[l][None],
            w_qkv=w_qkv, w_zpg=w_zpg, w_ba=w_ba,
            conv_qkv_w=conv_q[l], a_log=a_log[l], dt_bias=dt_bias[l],
            gdn_nw=jnp.tile(gdn_norm_w[l], d // HEAD_DIM)[None], pool_w=full["pool_w"], pool_scale=pool_scale[l][None],
            w_out=full["w_out"], w_up=full["w_up"], conv_ffn_w=conv_f[l], conv_ffn_b=conv_ffn_b[l][None],
            w_down=full["w_down"])

    xs = x[0]
    saved, params = [], []
    gathered = [_allgather(t, name="gather_" + n) for n, t in zip(BIG, _weight_halves(w, 0, cc))]
    for l in range(depth):
        params.append(layer_params(l, _full_weights(w, gathered)))
        nxt = _weight_halves(w, l + 1, cc) if l + 1 < depth else ()
        xs, sv, gathered = _layer_fwd(xs, params[l], n_heads, nxt)
        saved.append(sv)
    dx, dnf_p, loss_p = _final_loss(xs, norm_final_w[None], loss_target[0], name="final_loss")
    loss = lax.psum(jnp.sum(loss_p) * (0.5 / d), ("x", "y", "c"))

    slot = jnp.reshape(cc, (1,)).astype(jnp.int32)
    small_g = [None] * depth
    reduced = [dict() for _ in range(depth)]
    riding = {}
    for l in reversed(range(depth)):
        dx, big, small_g[l], got = _layer_bwd(dx, params[l], saved[l], n_heads, riding)
        for n, t in got.items():
            reduced[l + 1][n] = _finish_reduce(n, t, w[n].shape[1:], slot)
        riding = _layer_parts(big) if l > 0 else {}
        if l == 0:
            for n, part in _layer_parts(big).items():
                reduced[0][n] = _finish_reduce(n, _all_to_all(part, name="scatter_" + n), w[n].shape[1:], slot)
    grads = {n: jnp.stack([reduced[l][n] for l in range(depth)]) for n in BIG}

    small_shapes = {n: ((depth,) + small_g[0][n].shape if n != "norm_final_w" else (d,)) for n in SMALL}
    small_local = [jnp.stack([small_g[l][n] for l in range(depth)]) for n in SMALL[:-1]] + [jnp.sum(dnf_p, axis=0)]
    sp = _pack_small(small_local, 512)
    sg = _allgather(sp, name="gather_small").reshape(8, sp.shape[0], 128)
    small_sum = _unpack_small(_sum_parts(sg, name="sum_small"), [small_shapes[n] for n in SMALL])
    for n, g in zip(SMALL, small_sum):
        if n in ("conv_qkv_w", "conv_ffn_w"):
            cols = w[n].shape[-1]
            g = lax.dynamic_slice_in_dim(g, chip * cols, cols, axis=2)
        grads[n] = g

    delta, new_m, new_v = {}, {}, {}
    for n in BIG:
        shp = w[n].shape
        r2 = lambda t: t.reshape(-1, shp[-1])
        dd, mm, vv = _adamw(r2(w[n]), r2(grads[n]), r2(m[n]), r2(v[n]), name="adamw_" + n)
        delta[n], new_m[n], new_v[n] = dd.reshape(shp), mm.reshape(shp), vv.reshape(shp)
    pk = lambda src: _pack_small([src[n] for n in SMALL], 8)
    outs = _adamw(pk(w), pk(grads), pk(m), pk(v), name="adamw_small")
    for dst, buf in zip((delta, new_m, new_v), outs):
        for n, t in zip(SMALL, _unpack_small(buf, [w[n].shape for n in SMALL])):
            dst[n] = t

    return (loss, dx[None], *[grads[n] for n in WEIGHTS], *[delta[n] for n in WEIGHTS],
            *[new_m[n] for n in WEIGHTS], *[new_v[n] for n in WEIGHTS])
```

```python
import functools
import math

import jax
import jax.numpy as jnp
from jax import lax
from jax.experimental import pallas as pl
from jax.experimental.pallas import tpu as pltpu

F32 = jnp.float32
BF16 = jnp.bfloat16
EPS = 1e-6
CHUNK = 64
HEAD_DIM = 128
POOL_WINDOWS = (2, 4, 8, 16)
HALO = 16
ADAM_LR, ADAM_B1, ADAM_B2, ADAM_EPS, ADAM_WD, ADAM_STEP = 0.001, 0.9, 0.999, 1e-08, 0.01, 10
V7X_VMEM_LIMIT = 56 * 1024 * 1024
SUM_BLOCK_BYTES = 6 * 1024 * 1024
MAX_COPY_BYTES = 8 * 1024 * 1024
MATMUL_TILE_BYTES = 10 * 1024 * 1024
MESH = pl.DeviceIdType.MESH


def _tile(n, cap, align=128):
    if n <= cap:
        return n
    t = (cap // align) * align
    while t >= align:
        if n % t == 0:
            return t
        t -= align
    return n


def _params(*sem):
    return pltpu.CompilerParams(dimension_semantics=sem, vmem_limit_bytes=V7X_VMEM_LIMIT)


def _sigmoid(x):
    return 0.5 * (1.0 + jnp.tanh(0.5 * x))


def _down(x, j):
    return pltpu.roll(x, j, 0)


def _up(x, j):
    return pltpu.roll(x, x.shape[0] - j, 0)


def _fold8(x):
    n, c = x.shape
    return jnp.sum(x.reshape(n // 8, 8, c), axis=0)


def _matmul(a, b, *, ta=False, tb=False, add=None, out_dtype=F32, b_k0=0, b_n0=0, n=None, a_part=None, b_part=None,
            a_kparts=False, name):
    a2, b2 = a.shape[-2:], b.shape[-2:]
    m, k = (a2[1], a2[0]) if ta else a2
    if n is None:
        n = b2[0] if tb else b2[1]
    tm, tn = _tile(m, 1024), _tile(n, 1024)
    per_k = tm * a.dtype.itemsize + tn * b.dtype.itemsize
    tk = _tile(k, max(128, MATMUL_TILE_BYTES // per_k // 128 * 128))
    if a_kparts:
        assert not ta and a_part is None
        per_slab = k // tk
        k = k * a.shape[0]
    nk = k // tk
    assert b_k0 % tk == 0 and b_n0 % tn == 0, (b_k0, b_n0, tk, tn)
    ko, no = b_k0 // tk, b_n0 // tn

    def spec(shape, index, part):
        if part is None:
            return pl.BlockSpec(shape, index)
        return pl.BlockSpec((None,) + shape, lambda i, j, kk: (part,) + index(i, j, kk))

    a_spec = spec((tk, tm), lambda i, j, kk: (kk, i), a_part) if ta else spec((tm, tk), lambda i, j, kk: (i, kk), a_part)
    if a_kparts:
        a_spec = pl.BlockSpec((None, tm, tk), lambda i, j, kk: (kk // per_slab, i, kk % per_slab))
    b_spec = (spec((tn, tk), lambda i, j, kk: (j + no, kk + ko), b_part) if tb
              else spec((tk, tn), lambda i, j, kk: (kk + ko, j + no), b_part))
    o_spec = pl.BlockSpec((tm, tn), lambda i, j, kk: (i, j))
    dims = (((0 if ta else 1,), (1 if tb else 0,)), ((), ()))
    has_add = add is not None

    def body(*refs):
        a_ref, b_ref = refs[:2]
        add_ref = refs[2] if has_add else None
        o_ref = refs[3 if has_add else 2]
        part = lax.dot_general(a_ref[...].astype(BF16), b_ref[...].astype(BF16), dims, preferred_element_type=F32)

        def finish(r):
            if has_add:
                r = r + add_ref[...]
            o_ref[...] = r.astype(out_dtype)

        if nk == 1:
            finish(part)
            return
        acc = refs[-1]
        kk = pl.program_id(2)

        @pl.when(kk == 0)
        def _():
            acc[...] = part

        @pl.when(kk > 0)
        def _():
            acc[...] += part

        @pl.when(kk == nk - 1)
        def _():
            finish(acc[...])

    ins = [a, b] + ([add] if has_add else [])
    specs = [a_spec, b_spec] + ([o_spec] if has_add else [])
    return pl.pallas_call(
        body, name=name, grid=(m // tm, n // tn, nk), in_specs=specs, out_specs=o_spec,
        out_shape=jax.ShapeDtypeStruct((m, n), out_dtype),
        scratch_shapes=[pltpu.VMEM((tm, tn), F32)] if nk > 1 else [],
        compiler_params=_params("parallel", "parallel", "arbitrary"))(*ins)


def _rmsnorm_fwd(x, w, *, name):
    s, d = x.shape
    ts = _tile(s, 512, 8)

    def body(x_ref, w_ref, o_ref):
        xv = x_ref[...]
        r = lax.rsqrt(jnp.mean(xv * xv, axis=-1, keepdims=True) + EPS)
        o_ref[...] = (xv * r * w_ref[...]).astype(BF16)

    return pl.pallas_call(
        body, name=name, grid=(s // ts,),
        in_specs=[pl.BlockSpec((ts, d), lambda i: (i, 0)), pl.BlockSpec((1, d), lambda i: (0, 0))],
        out_specs=pl.BlockSpec((ts, d), lambda i: (i, 0)),
        out_shape=jax.ShapeDtypeStruct((s, d), BF16), compiler_params=_params("parallel"))(x, w)


def _rmsnorm_bwd(x, w, dh, dres, *, name):
    s, d = x.shape
    ts = _tile(s, 512, 8)

    def body(x_ref, w_ref, dh_ref, dres_ref, dx_ref, dw_ref):
        xv = x_ref[...]
        r = lax.rsqrt(jnp.mean(xv * xv, axis=-1, keepdims=True) + EPS)
        xh = xv * r
        dhv = dh_ref[...]
        dxh = dhv * w_ref[...]
        dx_ref[...] = dres_ref[...] + r * (dxh - xh * jnp.mean(dxh * xh, axis=-1, keepdims=True))

        @pl.when(pl.program_id(0) == 0)
        def _():
            dw_ref[...] = jnp.zeros_like(dw_ref)

        dw_ref[...] += _fold8(dhv * xh)

    row = pl.BlockSpec((ts, d), lambda i: (i, 0))
    return pl.pallas_call(
        body, name=name, grid=(s // ts,),
        in_specs=[row, pl.BlockSpec((1, d), lambda i: (0, 0)), row, row],
        out_specs=[row, pl.BlockSpec((8, d), lambda i: (0, 0))],
        out_shape=[jax.ShapeDtypeStruct((s, d), F32), jax.ShapeDtypeStruct((8, d), F32)],
        compiler_params=_params("arbitrary"))(x, w, dh, dres)


def _final_loss(x, w, target, *, name):
    s, d = x.shape
    ts = _tile(s, 512, 8)

    def body(x_ref, w_ref, t_ref, dx_ref, dw_ref, loss_ref):
        xv = x_ref[...]
        r = lax.rsqrt(jnp.mean(xv * xv, axis=-1, keepdims=True) + EPS)
        xh = xv * r
        wv = w_ref[...]
        err = xh * wv - t_ref[...]
        dy = err * (1.0 / d)
        dxh = dy * wv
        dx_ref[...] = r * (dxh - xh * jnp.mean(dxh * xh, axis=-1, keepdims=True))

        @pl.when(pl.program_id(0) == 0)
        def _():
            dw_ref[...] = jnp.zeros_like(dw_ref)
            loss_ref[...] = jnp.zeros_like(loss_ref)

        dw_ref[...] += _fold8(dy * xh)
        e2 = _fold8(err * err)
        part = e2[:, 0:128]
        for j in range(1, d // 128):
            part = part + e2[:, j * 128:(j + 1) * 128]
        loss_ref[...] += part

    row = pl.BlockSpec((ts, d), lambda i: (i, 0))
    return pl.pallas_call(
        body, name=name, grid=(s // ts,),
        in_specs=[row, pl.BlockSpec((1, d), lambda i: (0, 0)), row],
        out_specs=[row, pl.BlockSpec((8, d), lambda i: (0, 0)), pl.BlockSpec((8, 128), lambda i: (0, 0))],
        out_shape=[jax.ShapeDtypeStruct((s, d), F32), jax.ShapeDtypeStruct((8, d), F32),
                   jax.ShapeDtypeStruct((8, 128), F32)],
        compiler_params=_params("arbitrary"))(x, w, target)


def _prev_spec(ts, tc, col):
    return pl.BlockSpec((HALO, tc), lambda i, j: (jnp.maximum(i * (ts // HALO) - 1, 0), col(j)))


def _next_spec(ts, tc, col, n_tiles):
    return pl.BlockSpec((HALO, tc), lambda i, j: (jnp.minimum((i + 1) * (ts // HALO), n_tiles * (ts // HALO) - 1), col(j)))


def _with_prev(prev, cur, first):
    return jnp.concatenate([jnp.where(first, 0.0, prev), cur], axis=0)


def _with_next(cur, nxt, last):
    return jnp.concatenate([cur, jnp.where(last, 0.0, nxt)], axis=0)


def _gelu(x):
    return 0.5 * x * (1.0 + lax.erf(x * (1.0 / math.sqrt(2.0))))


def _gelu_grad(x):
    return 0.5 * (1.0 + lax.erf(x * (1.0 / math.sqrt(2.0)))) + x * jnp.exp(-0.5 * x * x) * (1.0 / math.sqrt(2.0 * math.pi))


def _ffn_conv(prev, cur, w, first):
    xx = _with_prev(prev, cur, first)
    return w[2:3] * cur + w[1:2] * _down(xx, 1)[HALO:] + w[0:1] * _down(xx, 2)[HALO:]


def _ffn_mid_fwd(gu, cw, cb, *, name):
    s, f2 = gu.shape
    f = f2 // 2
    ts, tc = _tile(s, 512, HALO), _tile(f, 512)
    nj = f // tc

    def body(g_ref, gp_ref, u_ref, w_ref, b_ref, o_ref):
        first = pl.program_id(0) == 0
        gc = _ffn_conv(gp_ref[...], g_ref[...], w_ref[...], first) + b_ref[...]
        o_ref[...] = (_gelu(gc) * u_ref[...]).astype(BF16)

    return pl.pallas_call(
        body, name=name, grid=(s // ts, nj),
        in_specs=[pl.BlockSpec((ts, tc), lambda i, j: (i, j)), _prev_spec(ts, tc, lambda j: j),
                  pl.BlockSpec((ts, tc), lambda i, j: (i, j + nj)),
                  pl.BlockSpec((3, tc), lambda i, j: (0, j)), pl.BlockSpec((1, tc), lambda i, j: (0, j))],
        out_specs=pl.BlockSpec((ts, tc), lambda i, j: (i, j)),
        out_shape=jax.ShapeDtypeStruct((s, f), BF16), compiler_params=_params("parallel", "parallel"))(gu, gu, gu, cw, cb)


def _ffn_mid_bwd(gu, cw, cb, dact, scatter=(), *, name):
    s, f2 = gu.shape
    f = f2 // 2
    ts, tc = _tile(s, 512, HALO), _tile(f, 512)
    nj, ni = f // tc, s // ts
    n_x = len(scatter)

    def body(*refs):
        g_ref, gp_ref, gn_ref, u_ref, un_ref, d_ref, dn_ref, w_ref, b_ref = refs[:9]
        dg_ref, du_ref, dw_ref, db_ref = refs[9 + n_x:13 + n_x]
        riders = (_all_to_all_step, refs[9:9 + n_x], refs[13 + n_x:13 + 2 * n_x], refs[13 + 2 * n_x:], (ni, nj))
        _ride(*riders, start=True)
        i = pl.program_id(0)
        first, last = i == 0, i == ni - 1
        w, b = w_ref[...], b_ref[...]
        g = g_ref[...]
        gx = jnp.concatenate([jnp.where(first, 0.0, gp_ref[...]), g, jnp.where(last, 0.0, gn_ref[...])], axis=0)
        g1, g2 = _down(gx, 1), _down(gx, 2)
        gc = (w[2:3] * gx + w[1:2] * g1 + w[0:1] * g2)[HALO:] + b
        ux = _with_next(u_ref[...], un_ref[...], last)
        dx = _with_next(d_ref[...], dn_ref[...], last)
        dgc = dx * ux * _gelu_grad(gc)
        du_ref[...] = (dx[:ts] * _gelu(gc[:ts])).astype(BF16)
        dg = w[2:3] * dgc + w[1:2] * _up(dgc, 1) + w[0:1] * _up(dgc, 2)
        dg_ref[...] = dg[:ts].astype(BF16)
        dgt = dgc[:ts]
        db_ref[...] = _fold8(dgt)
        dw_ref[:, 0:tc] = _fold8(dgt * g2[HALO:HALO + ts])
        dw_ref[:, tc:2 * tc] = _fold8(dgt * g1[HALO:HALO + ts])
        dw_ref[:, 2 * tc:3 * tc] = _fold8(dgt * g)
        _ride(*riders, start=False)

    cur = lambda off: pl.BlockSpec((ts, tc), lambda i, j: (i, j + off))
    outs = pl.pallas_call(
        body, name=name, grid=(ni, nj),
        in_specs=[cur(0), _prev_spec(ts, tc, lambda j: j), _next_spec(ts, tc, lambda j: j, ni),
                  cur(nj), _next_spec(ts, tc, lambda j: j + nj, ni),
                  cur(0), _next_spec(ts, tc, lambda j: j, ni),
                  pl.BlockSpec((3, tc), lambda i, j: (0, j)), pl.BlockSpec((1, tc), lambda i, j: (0, j))] + [_HBM] * n_x,
        out_specs=[cur(0), cur(0),
                   pl.BlockSpec((8, 3 * tc), lambda i, j: (i, j)), pl.BlockSpec((8, tc), lambda i, j: (i, j))] + [_HBM] * n_x,
        out_shape=[jax.ShapeDtypeStruct((s, f), BF16), jax.ShapeDtypeStruct((s, f), BF16),
                   jax.ShapeDtypeStruct((ni * 8, 3 * f), F32), jax.ShapeDtypeStruct((ni * 8, f), F32)]
        + [jax.ShapeDtypeStruct(t.shape, t.dtype) for t in scatter],
        scratch_shapes=_EXCHANGE_SEMS * n_x,
        compiler_params=_params(*(("arbitrary", "arbitrary") if n_x else ("parallel", "parallel"))))(
            gu, gu, gu, gu, gu, dact, dact, cw, cb, *scatter)
    return outs[0], outs[1], outs[2], outs[3], list(outs[4:])


def _qkv_fwd(pq, cw, n_heads, *, name):
    s, c3 = pq.shape
    ts = _tile(s, 512, HALO)

    def body(x_ref, xp_ref, w_ref, o_ref):
        j = pl.program_id(1)
        w = w_ref[...]
        x = x_ref[...]
        xx = _with_prev(xp_ref[...], x, pl.program_id(0) == 0)
        y = w[3:4] * x + w[2:3] * _down(xx, 1)[HALO:] + w[1:2] * _down(xx, 2)[HALO:] + w[0:1] * _down(xx, 3)[HALO:]
        c = y * _sigmoid(y)
        r = lax.rsqrt(jnp.sum(c * c, axis=-1, keepdims=True) + EPS)
        scale = jnp.where(j < n_heads, HEAD_DIM ** -0.5, 1.0)
        o_ref[...] = jnp.where(j < 2 * n_heads, c * (r * scale), c)

    return pl.pallas_call(
        body, name=name, grid=(s // ts, 3 * n_heads),
        in_specs=[pl.BlockSpec((ts, HEAD_DIM), lambda i, j: (i, j)), _prev_spec(ts, HEAD_DIM, lambda j: j),
                  pl.BlockSpec((4, HEAD_DIM), lambda i, j: (0, j))],
        out_specs=pl.BlockSpec((None, ts, HEAD_DIM), lambda i, j: (j // n_heads, i, j % n_heads)),
        out_shape=jax.ShapeDtypeStruct((3, s, c3 // 3), F32),
        compiler_params=_params("parallel", "parallel"))(pq, pq, cw)


def _qkv_bwd(pq, cw, dqkv3, n_heads, scatter=(), *, name):
    s, c3 = pq.shape
    ts = _tile(s, 512, HALO)
    ni = s // ts
    hd = HEAD_DIM

    n_x = len(scatter)

    def body(*refs):
        x_ref, xp_ref, xn_ref, w_ref, d_ref, dn_ref = refs[:6]
        dx_ref, dw_ref = refs[6 + n_x:8 + n_x]
        riders = (_all_to_all_step, refs[6:6 + n_x], refs[8 + n_x:8 + 2 * n_x], refs[8 + 2 * n_x:], (ni, 3 * n_heads))
        _ride(*riders, start=True)
        i, j = pl.program_id(0), pl.program_id(1)
        first, last = i == 0, i == ni - 1
        w = w_ref[...]
        x = x_ref[...]
        xx = jnp.concatenate([jnp.where(first, 0.0, xp_ref[...]), x, jnp.where(last, 0.0, xn_ref[...])], axis=0)
        x1, x2, x3 = _down(xx, 1), _down(xx, 2), _down(xx, 3)
        y = (w[3:4] * xx + w[2:3] * x1 + w[1:2] * x2 + w[0:1] * x3)[HALO:]
        sg = _sigmoid(y)
        c = y * sg
        dn = _with_next(d_ref[...], dn_ref[...], last)
        r = lax.rsqrt(jnp.sum(c * c, axis=-1, keepdims=True) + EPS)
        nrm = c * r
        dnn = dn * jnp.where(j < n_heads, hd ** -0.5, 1.0)
        dc = jnp.where(j < 2 * n_heads, r * (dnn - nrm * jnp.sum(dnn * nrm, axis=-1, keepdims=True)), dn)
        dy = dc * (sg * (1.0 + y * (1.0 - sg)))
        dx = w[3:4] * dy + w[2:3] * _up(dy, 1) + w[1:2] * _up(dy, 2) + w[0:1] * _up(dy, 3)
        dx_ref[...] = dx[:ts].astype(BF16)
        dyt = dy[:ts]
        dw_ref[:, 0:hd] = _fold8(dyt * x3[HALO:HALO + ts])
        dw_ref[:, hd:2 * hd] = _fold8(dyt * x2[HALO:HALO + ts])
        dw_ref[:, 2 * hd:3 * hd] = _fold8(dyt * x1[HALO:HALO + ts])
        dw_ref[:, 3 * hd:4 * hd] = _fold8(dyt * x)
        _ride(*riders, start=False)

    dspec = lambda rows, row_index: pl.BlockSpec(
        (None, rows, hd), lambda i, j: (j // n_heads, row_index(i), j % n_heads))
    outs = pl.pallas_call(
        body, name=name, grid=(ni, 3 * n_heads),
        in_specs=[pl.BlockSpec((ts, hd), lambda i, j: (i, j)), _prev_spec(ts, hd, lambda j: j),
                  _next_spec(ts, hd, lambda j: j, ni), pl.BlockSpec((4, hd), lambda i, j: (0, j)),
                  dspec(ts, lambda i: i),
                  dspec(HALO, lambda i: jnp.minimum((i + 1) * (ts // HALO), ni * (ts // HALO) - 1))] + [_HBM] * n_x,
        out_specs=[pl.BlockSpec((ts, hd), lambda i, j: (i, j)),
                   pl.BlockSpec((8, 4 * hd), lambda i, j: (i, j))] + [_HBM] * n_x,
        out_shape=[jax.ShapeDtypeStruct((s, c3), BF16), jax.ShapeDtypeStruct((ni * 8, 4 * c3), F32)]
        + [jax.ShapeDtypeStruct(t.shape, t.dtype) for t in scatter],
        scratch_shapes=_EXCHANGE_SEMS * n_x,
        compiler_params=_params(*(("arbitrary", "arbitrary") if n_x else ("parallel", "parallel"))))(
            pq, pq, pq, cw, dqkv3, dqkv3, *scatter)
    return outs[0], outs[1], list(outs[2:])


def _gate_terms(ba, al, dt, h, n_heads):
    lane = lax.broadcasted_iota(jnp.int32, ba.shape, 1)
    braw = jnp.sum(jnp.where(lane == h, ba, 0.0), axis=1, keepdims=True)
    araw = jnp.sum(jnp.where(lane == h + n_heads, ba, 0.0), axis=1, keepdims=True)
    beta = _sigmoid(braw)
    z = araw + dt
    sp = jnp.maximum(z, 0.0) + jnp.log(1.0 + jnp.exp(-jnp.abs(z)))
    ea = jnp.exp(jnp.zeros((1, 1), F32) + al)
    return beta, z, sp, ea


def _gates_fwd(ba, a_log, dt_bias, n_heads, *, name):
    s = ba.shape[0]
    ts = _tile(s, 512, CHUNK)

    def body(ba_ref, al_ref, dt_ref, o_ref):
        h = pl.program_id(1)
        beta, _, sp, ea = _gate_terms(ba_ref[...], al_ref[h], dt_ref[h], h, n_heads)
        gx = jnp.broadcast_to(-ea * sp, (ts, HEAD_DIM))
        rc = lax.broadcasted_iota(jnp.int32, (ts, HEAD_DIM), 0) & (CHUNK - 1)
        for sh in (1, 2, 4, 8, 16, 32):
            gx = gx + jnp.where(rc >= sh, _down(gx, sh), 0.0)
        o_ref[0] = jnp.broadcast_to(beta, (ts, HEAD_DIM))
        o_ref[1] = gx

    smem = pl.BlockSpec(memory_space=pltpu.SMEM)
    return pl.pallas_call(
        body, name=name, grid=(s // ts, n_heads),
        in_specs=[pl.BlockSpec((ts, 128), lambda i, h: (i, 0)), smem, smem],
        out_specs=pl.BlockSpec((2, ts, HEAD_DIM), lambda i, h: (0, i, h)),
        out_shape=jax.ShapeDtypeStruct((2, s, n_heads * HEAD_DIM), F32),
        compiler_params=_params("parallel", "parallel"))(ba, a_log, dt_bias)


def _gates_bwd(ba, a_log, dt_bias, dbg, n_heads, *, name):
    s = ba.shape[0]
    ts = _tile(s, 512, CHUNK)
    ni = s // ts

    def body(ba_ref, al_ref, dt_ref, d_ref, o_ref, p_ref):
        h = pl.program_id(1)
        beta, z, sp, ea = _gate_terms(ba_ref[...], al_ref[h], dt_ref[h], h, n_heads)
        dg = d_ref[1]
        rc = lax.broadcasted_iota(jnp.int32, (ts, HEAD_DIM), 0) & (CHUNK - 1)
        for sh in (1, 2, 4, 8, 16, 32):
            dg = dg + jnp.where(rc < CHUNK - sh, _up(dg, sh), 0.0)
        daraw = dg * (-ea * _sigmoid(z))
        dbraw = d_ref[0] * (beta * (1.0 - beta))

        @pl.when(h == 0)
        def _():
            o_ref[...] = jnp.zeros_like(o_ref)
            p_ref[...] = jnp.zeros_like(p_ref)

        lane = lax.broadcasted_iota(jnp.int32, (1, 128), 1)
        is_b, is_a = lane == h, lane == h + n_heads
        o_ref[...] += jnp.where(is_b, dbraw, 0.0) + jnp.where(is_a, daraw, 0.0)
        p_ref[...] += jnp.where(is_b, _fold8(dg * (-ea * sp)), 0.0) + jnp.where(is_a, _fold8(daraw), 0.0)

    smem = pl.BlockSpec(memory_space=pltpu.SMEM)
    return pl.pallas_call(
        body, name=name, grid=(ni, n_heads),
        in_specs=[pl.BlockSpec((ts, 128), lambda i, h: (i, 0)), smem, smem,
                  pl.BlockSpec((2, ts, HEAD_DIM), lambda i, h: (0, i, h))],
        out_specs=[pl.BlockSpec((ts, 128), lambda i, h: (i, 0)), pl.BlockSpec((8, 128), lambda i, h: (i, 0))],
        out_shape=[jax.ShapeDtypeStruct((s, 128), F32), jax.ShapeDtypeStruct((ni * 8, 128), F32)],
        compiler_params=_params("parallel", "arbitrary"))(ba, a_log, dt_bias, dbg)


BLK = 2 * CHUNK
HEADS_PER_STEP = 2
NN = (((1,), (0,)), ((), ()))
NT = (((1,), (1,)), ((), ()))
TN = (((0,), (0,)), ((), ()))


def _dot(a, b, dims):
    return lax.dot_general(a.astype(BF16), b.astype(BF16), dims, preferred_element_type=F32)


def _dot3(a, b, dims):
    ah, bh = a.astype(BF16), b.astype(BF16)
    al, bl = (a - ah.astype(F32)).astype(BF16), (b - bh.astype(F32)).astype(BF16)
    d = lambda u, v: lax.dot_general(u, v, dims, preferred_element_type=F32)
    return d(ah, bh) + (d(ah, bl) + d(al, bh))


def _pair_masks():
    row = lax.broadcasted_iota(jnp.int32, (BLK, BLK), 0)
    col = lax.broadcasted_iota(jnp.int32, (BLK, BLK), 1)
    same = (row < CHUNK) == (col < CHUNK)
    return same & (row >= col), same & (row > col), row == col


def _pair_terms(q, k, v, b, gam, masks):
    tril, strict, eye = masks
    g_cols = jnp.sum(jnp.where(eye, gam, 0.0), axis=0, keepdims=True)
    dmat = jnp.exp(jnp.where(tril, gam - g_cols, -jnp.inf))
    eg = jnp.exp(gam)
    rowi = lax.broadcasted_iota(jnp.int32, (BLK, HEAD_DIM), 0)
    elast = jnp.exp(jnp.where(rowi < CHUNK, gam[CHUNK - 1:CHUNK], gam[BLK - 1:BLK]) - gam)
    kb, vb = k * b, v * b
    kq = _dot(jnp.concatenate([kb, q], axis=0), k, NT)
    lmat = jnp.where(strict, kq[:BLK] * dmat, 0.0)
    attn = kq[BLK:] * dmat
    rhs = jnp.concatenate([vb, kb * eg], axis=1)
    return dmat, eg, elast, kb, lmat, attn, rhs


def _unit_lower_inverse(lmat, eye):
    p = -lmat
    t = jnp.where(eye, 1.0, 0.0) + p
    p = _dot(p, p, NN)
    for _ in range(4):
        r = _dot(jnp.concatenate([p, t], axis=0), p, NN)
        p, t = r[:BLK], t + r[BLK:]
    return t + _dot(t, p, NN)


def _ride(step, in_refs, out_refs, sems, n_steps, *, start):
    if not in_refs:
        return
    i, j = pl.program_id(0), pl.program_id(1)
    when = (i == 0) & (j == 0) if start else (i == n_steps[0] - 1) & (j == n_steps[1] - 1)

    @pl.when(when)
    def _():
        for t, (src, dst) in enumerate(zip(in_refs, out_refs)):
            step(src, dst, *sems[3 * t:3 * t + 3], start=start, finish=not start)


def _gdn_fwd(qkv3, bg, gather=(), *, name):
    _, s, dl = qkv3.shape
    n_heads = dl // HEAD_DIM
    sb = _tile(s, 1024, BLK)
    npair = sb // BLK
    c = CHUNK

    hpg = HEADS_PER_STEP if n_heads % HEADS_PER_STEP == 0 else 1
    wd = hpg * HEAD_DIM

    n_x = len(gather)
    n_steps = (n_heads // hpg, s // sb)

    def body(*refs):
        qkv_ref, bg_ref = refs[:2]
        o_ref, st_ref, ti_ref = refs[2 + n_x:5 + n_x]
        s_scr = refs[5 + 2 * n_x]
        _ride(_allgather_step, refs[2:2 + n_x], refs[5 + n_x:5 + 2 * n_x], refs[6 + 2 * n_x:], n_steps, start=True)

        @pl.when(pl.program_id(1) == 0)
        def _():
            s_scr[...] = jnp.zeros_like(s_scr)

        masks = _pair_masks()

        def head_pair(hh, p, rows, state):
            ls = slice(hh * HEAD_DIM, (hh + 1) * HEAD_DIM)
            q, k, v = qkv_ref[0, rows, ls], qkv_ref[1, rows, ls], qkv_ref[2, rows, ls]
            b, gam = bg_ref[0, rows, ls], bg_ref[1, rows, ls]
            _, eg, elast, _, lmat, attn, rhs = _pair_terms(q, k, v, b, gam, masks)
            tinv = _unit_lower_inverse(lmat, masks[2])
            ti_ref[hh, rows, :] = tinv
            sol = _dot3(tinv, rhs, NN)
            u, w = sol[:, :HEAD_DIM], sol[:, HEAD_DIM:]
            qd, ke = q * eg, k * elast
            st_ref[hh, 2 * p] = state
            wq = _dot(jnp.concatenate([w[:c], qd[:c]], axis=0), state, NN)
            vn_a, o_a = u[:c] - wq[:c], wq[c:]
            state = state * jnp.exp(gam[c - 1:c]) + _dot(ke[:c], vn_a, TN)
            st_ref[hh, 2 * p + 1] = state
            wq = _dot(jnp.concatenate([w[c:], qd[c:]], axis=0), state, NN)
            vn_b, o_b = u[c:] - wq[:c], wq[c:]
            state = state * jnp.exp(gam[BLK - 1:BLK]) + _dot(ke[c:], vn_b, TN)
            o_ref[rows, ls] = jnp.concatenate([o_a, o_b], axis=0) + _dot(attn, jnp.concatenate([vn_a, vn_b], axis=0), NN)
            return state

        def pair(p, states):
            rows = pl.ds(pl.multiple_of(p * BLK, BLK), BLK)
            return tuple(head_pair(hh, p, rows, states[hh]) for hh in range(hpg))

        states = lax.fori_loop(0, npair, pair, tuple(s_scr[hh] for hh in range(hpg)))
        for hh in range(hpg):
            s_scr[hh] = states[hh]
        _ride(_allgather_step, refs[2:2 + n_x], refs[5 + n_x:5 + 2 * n_x], refs[6 + 2 * n_x:], n_steps, start=False)

    outs = pl.pallas_call(
        body, name=name, grid=n_steps,
        in_specs=[pl.BlockSpec((3, sb, wd), lambda h, j: (0, j, h)),
                  pl.BlockSpec((2, sb, wd), lambda h, j: (0, j, h))] + [_HBM] * n_x,
        out_specs=[pl.BlockSpec((sb, wd), lambda h, j: (j, h)),
                   pl.BlockSpec((hpg, 2 * npair, HEAD_DIM, HEAD_DIM), lambda h, j: (h, j, 0, 0)),
                   pl.BlockSpec((hpg, sb, BLK), lambda h, j: (h, j, 0))] + [_HBM] * n_x,
        out_shape=[jax.ShapeDtypeStruct((s, dl), F32),
                   jax.ShapeDtypeStruct((n_heads, s // c, HEAD_DIM, HEAD_DIM), F32),
                   jax.ShapeDtypeStruct((n_heads, s, BLK), F32)] + [_allgather_shape(t) for t in gather],
        scratch_shapes=[pltpu.VMEM((hpg, HEAD_DIM, HEAD_DIM), F32)] + _EXCHANGE_SEMS * n_x,
        compiler_params=_params("arbitrary", "arbitrary"))(qkv3, bg, *gather)
    return outs[0], outs[1], outs[2], list(outs[3:])


def _gdn_bwd(qkv3, bg, st, ti, d_o, scatter=(), *, name):
    _, s, dl = qkv3.shape
    n_heads = dl // HEAD_DIM
    sb = _tile(s, 1024, BLK)
    npair, nsb = sb // BLK, s // sb
    c = CHUNK
    hpg = HEADS_PER_STEP if n_heads % HEADS_PER_STEP == 0 else 1
    wd = hpg * HEAD_DIM

    n_x = len(scatter)
    n_steps = (n_heads // hpg, nsb)

    def body(*refs):
        qkv_ref, bg_ref, st_ref, ti_ref, do_ref = refs[:5]
        dqkv_ref, dbg_ref = refs[5 + n_x:7 + n_x]
        ds_scr = refs[7 + 2 * n_x]
        _ride(_all_to_all_step, refs[5:5 + n_x], refs[7 + n_x:7 + 2 * n_x], refs[8 + 2 * n_x:], n_steps, start=True)

        @pl.when(pl.program_id(1) == 0)
        def _():
            ds_scr[...] = jnp.zeros_like(ds_scr)

        masks = _pair_masks()
        tril, strict, eye = masks
        rowc = lax.broadcasted_iota(jnp.int32, (BLK, 1), 0)

        def total(x):
            return jnp.sum(jnp.sum(x, axis=1, keepdims=True), axis=0, keepdims=True)

        def head_pair(hh, p, rows, ds2):
            ls = slice(hh * HEAD_DIM, (hh + 1) * HEAD_DIM)
            q, k, v = qkv_ref[0, rows, ls], qkv_ref[1, rows, ls], qkv_ref[2, rows, ls]
            b, gam = bg_ref[0, rows, ls], bg_ref[1, rows, ls]
            tinv, dout = ti_ref[hh, rows, :], do_ref[rows, ls]
            s0, s1 = st_ref[hh, 2 * p], st_ref[hh, 2 * p + 1]
            dmat, eg, elast, kb, lmat, attn, rhs = _pair_terms(q, k, v, b, gam, masks)
            sol = _dot3(tinv, rhs, NN)
            u, w = sol[:, :HEAD_DIM], sol[:, HEAD_DIM:]
            qd, ke = q * eg, k * elast
            dec_a, dec_b = jnp.exp(gam[c - 1:c]), jnp.exp(gam[BLK - 1:BLK])
            vn = u - jnp.concatenate([_dot(w[:c], s0, NN), _dot(w[c:], s1, NN)], axis=0)
            dvn_o = _dot(attn, dout, TN)
            dvn_b = dvn_o[c:] + _dot(ke[c:], ds2, NN)
            ds1 = _dot(qd[c:], dout[c:], TN) + ds2 * dec_b - _dot(w[c:], dvn_b, TN)
            dvn_a = dvn_o[:c] + _dot(ke[:c], ds1, NN)
            ds0 = _dot(qd[:c], dout[:c], TN) + ds1 * dec_a - _dot(w[:c], dvn_a, TN)
            dvn = jnp.concatenate([dvn_a, dvn_b], axis=0)
            dke = jnp.concatenate([_dot(vn[:c], ds1, NT), _dot(vn[c:], ds2, NT)], axis=0)
            dw = -jnp.concatenate([_dot(dvn_a, s0, NT), _dot(dvn_b, s1, NT)], axis=0)
            dqd = jnp.concatenate([_dot(dout[:c], s0, NT), _dot(dout[c:], s1, NT)], axis=0)
            dattn = jnp.where(tril, _dot(dout, vn, NT), 0.0)
            drhs = _dot(tinv, jnp.concatenate([dvn, dw], axis=1), TN)
            dl_ = jnp.where(strict, -_dot(drhs, sol, NT), 0.0)
            dm, dqk = dl_ * dmat, dattn * dmat
            dvb, drw = drhs[:, :HEAD_DIM], drhs[:, HEAD_DIM:]
            dkb = _dot(dm, k, NN) + drw * eg
            dq = _dot(dqk, k, NN) + dqd * eg
            dk = _dot(dm, kb, TN) + _dot(dqk, q, TN) + dke * elast + dkb * b
            dbeta = jnp.sum(dvb * v + dkb * k, axis=1, keepdims=True)
            e = dl_ * lmat + dattn * attn
            e_cols = jnp.sum(jnp.where(eye, jnp.sum(e, axis=0, keepdims=True), 0.0), axis=1, keepdims=True)
            dke_ke = dke * ke
            dgam = (jnp.sum(e, axis=1, keepdims=True) - e_cols
                    + jnp.sum(drw * (kb * eg) + dqd * qd - dke_ke, axis=1, keepdims=True))
            tot_a = total(dke_ke[:c]) + total(s0 * ds1) * dec_a[:, :1]
            tot_b = total(dke_ke[c:]) + total(s1 * ds2) * dec_b[:, :1]
            dgam = dgam + jnp.where(rowc == c - 1, tot_a, 0.0) + jnp.where(rowc == BLK - 1, tot_b, 0.0)
            dqkv_ref[0, rows, ls] = dq
            dqkv_ref[1, rows, ls] = dk
            dqkv_ref[2, rows, ls] = dvb * b
            dbg_ref[0, rows, ls] = jnp.broadcast_to(dbeta, (BLK, HEAD_DIM))
            dbg_ref[1, rows, ls] = jnp.broadcast_to(dgam, (BLK, HEAD_DIM))
            return ds0

        def pair(pp, dstates):
            p = npair - 1 - pp
            rows = pl.ds(pl.multiple_of(p * BLK, BLK), BLK)
            return tuple(head_pair(hh, p, rows, dstates[hh]) for hh in range(hpg))

        dstates = lax.fori_loop(0, npair, pair, tuple(ds_scr[hh] for hh in range(hpg)))
        for hh in range(hpg):
            ds_scr[hh] = dstates[hh]
        _ride(_all_to_all_step, refs[5:5 + n_x], refs[7 + n_x:7 + 2 * n_x], refs[8 + 2 * n_x:], n_steps, start=False)

    rev = lambda j: nsb - 1 - j
    outs = pl.pallas_call(
        body, name=name, grid=n_steps,
        in_specs=[pl.BlockSpec((3, sb, wd), lambda h, j: (0, rev(j), h)),
                  pl.BlockSpec((2, sb, wd), lambda h, j: (0, rev(j), h)),
                  pl.BlockSpec((hpg, 2 * npair, HEAD_DIM, HEAD_DIM), lambda h, j: (h, rev(j), 0, 0)),
                  pl.BlockSpec((hpg, sb, BLK), lambda h, j: (h, rev(j), 0)),
                  pl.BlockSpec((sb, wd), lambda h, j: (rev(j), h))] + [_HBM] * n_x,
        out_specs=[pl.BlockSpec((3, sb, wd), lambda h, j: (0, rev(j), h)),
                   pl.BlockSpec((2, sb, wd), lambda h, j: (0, rev(j), h))] + [_HBM] * n_x,
        out_shape=[jax.ShapeDtypeStruct((3, s, dl), F32), jax.ShapeDtypeStruct((2, s, dl), F32)]
        + [jax.ShapeDtypeStruct(t.shape, t.dtype) for t in scatter],
        scratch_shapes=[pltpu.VMEM((hpg, HEAD_DIM, HEAD_DIM), F32)] + _EXCHANGE_SEMS * n_x,
        compiler_params=_params("arbitrary", "arbitrary"))(qkv3, bg, st, ti, d_o, *scatter)
    return outs[0], outs[1], list(outs[2:])


N_GROUPS = len(POOL_WINDOWS)


def _pick(g, vals):
    out = vals[-1]
    for i in range(len(vals) - 2, -1, -1):
        out = jnp.where(g == i, vals[i], out)
    return out


def _head_norm(o, nw):
    hats, outs = [], []
    for h in range(o.shape[1] // HEAD_DIM):
        sl = slice(h * HEAD_DIM, (h + 1) * HEAD_DIM)
        oh = o[:, sl]
        r = lax.rsqrt(jnp.mean(oh * oh, axis=-1, keepdims=True) + EPS)
        hats.append((oh * r, r))
        outs.append(oh * r * nw[:, sl])
    return hats, jnp.concatenate(outs, axis=1) if len(outs) > 1 else outs[0]


def _pool_counts(g, t0, n):
    t = (lax.broadcasted_iota(jnp.int32, (n, 1), 0) + t0 + 1).astype(F32)
    return jnp.minimum(t, _pick(g, [float(w) for w in POOL_WINDOWS]))


def _pool(prev, cur, first, g, t0):
    s = _with_prev(prev, cur, first)
    sums = []
    for sh in (1, 2, 4, 8):
        s = s + _down(s, sh)
        sums.append(s)
    return _pick(g, sums)[HALO:] / _pool_counts(g, t0, cur.shape[0]) - cur


def _mix_specs(ts, gw, ni):
    seg = lambda k: pl.BlockSpec((ts, gw), lambda g, i: (i, k * N_GROUPS + g))
    per = ts // HALO
    prev = lambda k: pl.BlockSpec((HALO, gw), lambda g, i: (jnp.maximum(i * per - 1, 0), k * N_GROUPS + g))
    nxt = lambda k: pl.BlockSpec((HALO, gw), lambda g, i: (jnp.minimum((i + 1) * per, ni * per - 1), k * N_GROUPS + g))
    vec = pl.BlockSpec((1, gw), lambda g, i: (0, g))
    pw = pl.BlockSpec((None, gw, gw), lambda g, i: (g, 0, 0))
    return seg, prev, nxt, vec, pw


def _mix_fwd(o, zpg, nw, pw, ps, *, name):
    s, d = o.shape
    gw = d // N_GROUPS
    ts = _tile(s, 512, HALO)
    ni = s // ts
    seg, prev, _, vec, pwspec = _mix_specs(ts, gw, ni)

    def body(o_ref, z_ref, p_ref, pp_ref, ga_ref, gb_ref, nw_ref, pw_ref, ps_ref, out_ref):
        g, i = pl.program_id(0), pl.program_id(1)
        _, on = _head_norm(o_ref[...], nw_ref[...])
        z = z_ref[...]
        ya = on * (z * _sigmoid(z))
        pooled = _pool(pp_ref[...], p_ref[...], i == 0, g, i * ts)
        yb = _dot(pooled, pw_ref[...], NN) * ps_ref[...]
        out_ref[...] = (_sigmoid(ga_ref[...]) * ya + _sigmoid(gb_ref[...]) * yb).astype(BF16)

    return pl.pallas_call(
        body, name=name, grid=(N_GROUPS, ni),
        in_specs=[seg(0), seg(0), seg(1), prev(1), seg(2), seg(3), vec, pwspec, vec],
        out_specs=seg(0), out_shape=jax.ShapeDtypeStruct((s, d), BF16),
        compiler_params=_params("parallel", "parallel"))(o, zpg, zpg, zpg, zpg, zpg, nw, pw, ps)


def _mix_bwd(o, zpg, nw, pw, ps, dmix, *, name):
    s, d = o.shape
    gw = d // N_GROUPS
    ts = _tile(s, 512, HALO)
    ni = s // ts
    seg, prev, nxt, vec, pwspec = _mix_specs(ts, gw, ni)

    def body(o_ref, z_ref, p_ref, pp_ref, ga_ref, gb_ref, gbn_ref, nw_ref, pw_ref, ps_ref, dm_ref, dmn_ref,
             do_ref, d4_ref, dpw_ref, dnw_ref, dps_ref):
        g, i = pl.program_id(0), pl.program_id(1)
        last = i == ni - 1
        nw, ps, pwv = nw_ref[...], ps_ref[...], pw_ref[...]
        ov, z = o_ref[...], z_ref[...]
        hats, on = _head_norm(ov, nw)
        sz = _sigmoid(z)
        silu = z * sz
        ya = on * silu
        pooled = _pool(pp_ref[...], p_ref[...], i == 0, g, i * ts)
        yp = _dot(pooled, pwv, NN)
        sga, sgb = _sigmoid(ga_ref[...]), _sigmoid(gb_ref[...])
        dm = dm_ref[...]
        dya, dyb = dm * sga, dm * sgb
        d4_ref[2] = (dm * ya * (sga * (1.0 - sga))).astype(BF16)
        d4_ref[3] = (dm * (yp * ps) * (sgb * (1.0 - sgb))).astype(BF16)
        dps_ref[...] = _fold8(dyb * yp)
        dyp = dyb * ps

        @pl.when(i == 0)
        def _():
            dpw_ref[...] = jnp.zeros_like(dpw_ref)

        dpw_ref[...] += _dot(pooled, dyp, TN)
        dyp_next = jnp.where(last, 0.0, dmn_ref[...] * _sigmoid(gbn_ref[...]) * ps)
        dpool = _dot(jnp.concatenate([dyp, dyp_next], axis=0), pwv, NT)
        a = dpool / _pool_counts(g, i * ts, ts + HALO)
        sums = []
        for sh in (1, 2, 4, 8):
            a = a + _up(a, sh)
            sums.append(a)
        d4_ref[1] = (_pick(g, sums)[:ts] - dpool[:ts]).astype(BF16)
        d4_ref[0] = (dya * on * (sz * (1.0 + z * (1.0 - sz)))).astype(BF16)
        don = dya * silu
        dos, dnws = [], []
        for h, (ohat, r) in enumerate(hats):
            sl = slice(h * HEAD_DIM, (h + 1) * HEAD_DIM)
            dxh = don[:, sl] * nw[:, sl]
            dos.append(r * (dxh - ohat * jnp.mean(dxh * ohat, axis=-1, keepdims=True)))
            dnws.append(_fold8(don[:, sl] * ohat))
        do_ref[...] = jnp.concatenate(dos, axis=1) if len(dos) > 1 else dos[0]
        dnw_ref[...] = jnp.concatenate(dnws, axis=1) if len(dnws) > 1 else dnws[0]

    part = pl.BlockSpec((8, gw), lambda g, i: (i, g))
    return pl.pallas_call(
        body, name=name, grid=(N_GROUPS, ni),
        in_specs=[seg(0), seg(0), seg(1), prev(1), seg(2), seg(3), nxt(3), vec, pwspec, vec, seg(0), nxt(0)],
        out_specs=[seg(0), pl.BlockSpec((4, ts, gw), lambda g, i: (0, i, g)),
                   pl.BlockSpec((None, gw, gw), lambda g, i: (g, 0, 0)), part, part],
        out_shape=[jax.ShapeDtypeStruct((s, d), F32), jax.ShapeDtypeStruct((4, s, d), BF16),
                   jax.ShapeDtypeStruct((N_GROUPS, gw, gw), F32),
                   jax.ShapeDtypeStruct((ni * 8, d), F32), jax.ShapeDtypeStruct((ni * 8, d), F32)],
        compiler_params=_params("parallel", "arbitrary"))(o, zpg, zpg, zpg, zpg, zpg, zpg, nw, pw, ps, dmix, dmix)


def _adamw(w, g, m, v, *, name):
    r, c = w.shape
    tr = _tile(r, max(8, (1 << 19) // c // 8 * 8), 8)

    def body(w_ref, g_ref, m_ref, v_ref, d_ref, mo_ref, vo_ref):
        gv = g_ref[...]
        mn = ADAM_B1 * m_ref[...] + (1.0 - ADAM_B1) * gv
        vn = ADAM_B2 * v_ref[...] + (1.0 - ADAM_B2) * (gv * gv)
        m_hat = mn / (1.0 - ADAM_B1 ** ADAM_STEP)
        v_hat = vn / (1.0 - ADAM_B2 ** ADAM_STEP)
        d_ref[...] = -ADAM_LR * (m_hat / (jnp.sqrt(v_hat) + ADAM_EPS) + ADAM_WD * w_ref[...])
        mo_ref[...] = mn
        vo_ref[...] = vn

    blk = pl.BlockSpec((tr, c), lambda i: (i, 0))
    return pl.pallas_call(
        body, name=name, grid=(r // tr,), in_specs=[blk] * 4, out_specs=[blk] * 3,
        out_shape=[jax.ShapeDtypeStruct((r, c), F32)] * 3, compiler_params=_params("parallel"))(w, g, m, v)


def _sum_parts(x, *, slot=None, name):
    p, r, c = x.shape
    tc = 128 if c % 128 == 0 else c
    tr = _tile(r, max(16, SUM_BLOCK_BYTES // (p * tc * x.dtype.itemsize)), 16)

    def body(*refs):
        x_ref, o_ref = refs[-2:]
        acc = x_ref[0].astype(F32)
        for i in range(1, p):
            acc = acc + x_ref[i].astype(F32)
        o_ref[...] = acc

    if slot is None:
        return pl.pallas_call(
            body, name=name, grid=(r // tr, c // tc),
            in_specs=[pl.BlockSpec((p, tr, tc), lambda i, j: (0, i, j))],
            out_specs=pl.BlockSpec((tr, tc), lambda i, j: (i, j)),
            out_shape=jax.ShapeDtypeStruct((r, c), F32), compiler_params=_params("parallel", "parallel"))(x)
    return pl.pallas_call(
        body, name=name,
        grid_spec=pltpu.PrefetchScalarGridSpec(
            num_scalar_prefetch=1, grid=(r // tr, c // tc),
            in_specs=[pl.BlockSpec((p, tr, tc), lambda i, j, s: (0, i, j))],
            out_specs=pl.BlockSpec((None, tr, tc), lambda i, j, s: (s[0], i, j))),
        out_shape=jax.ShapeDtypeStruct((2, r, c), F32), compiler_params=_params("parallel", "parallel"))(slot, x)


_HBM = pl.BlockSpec(memory_space=pltpu.HBM)


def _place():
    return lax.axis_index("x"), lax.axis_index("y"), lax.axis_index("c")


def _allgather(x_shard, *, name):
    def body(x_ref, out_ref, send_sems, recv_sems, local_sem):
        _allgather_step(x_ref, out_ref, send_sems, recv_sems, local_sem, start=True, finish=True)

    return pl.pallas_call(
        body, name=name, out_shape=_allgather_shape(x_shard), in_specs=[_HBM], out_specs=_HBM,
        scratch_shapes=_EXCHANGE_SEMS)(x_shard)


_EXCHANGE_SEMS = [pltpu.SemaphoreType.DMA((7,)), pltpu.SemaphoreType.DMA((7,)), pltpu.SemaphoreType.DMA]


def _allgather_shape(x_shard):
    return jax.ShapeDtypeStruct((8 * x_shard.shape[0], x_shard.shape[1]), x_shard.dtype)


def _allgather_step(x_ref, out_ref, send_sems, recv_sems, local_sem, *, start, finish):
    m_per = x_ref.shape[0]
    x, y, c = _place()
    me, sibling = (x, y, c), (x, y, 1 - c)
    chips = [(1 - x, y), (x, 1 - y), (1 - x, 1 - y)]

    def rows(px, py, pc):
        return out_ref.at[pl.ds((4 * px + 2 * py + pc) * m_per, m_per), :]

    def copy(k, block, to, src=None):
        return pltpu.make_async_remote_copy(
            src_ref=rows(*block) if src is None else src, dst_ref=rows(*block),
            send_sem=send_sems.at[k], recv_sem=recv_sems.at[k], device_id=to, device_id_type=MESH)

    mine = pltpu.make_async_copy(x_ref, rows(*me), local_sem)
    first = [copy(0, me, sibling, src=x_ref)]
    first += [copy(1 + j, me, (*chip, c), src=x_ref) for j, chip in enumerate(chips)]
    if start:
        mine.start()
        for cp in first:
            cp.start()
    if finish:
        passed = [copy(4 + j, (*chip, c), sibling) for j, chip in enumerate(chips)]
        for j, chip in enumerate(chips):
            copy(1 + j, (*chip, c), me).wait_recv()
            passed[j].start()
        copy(0, sibling, me).wait_recv()
        for j, chip in enumerate(chips):
            copy(4 + j, (*chip, 1 - c), me).wait_recv()
        for cp in first + passed:
            cp.wait_send()
        mine.wait()


def _all_to_all(parts, *, name):
    def body(g_ref, out_ref, send_sems, recv_sems, local_sem):
        _all_to_all_step(g_ref, out_ref, send_sems, recv_sems, local_sem, start=True, finish=True)

    return pl.pallas_call(
        body, name=name, out_shape=jax.ShapeDtypeStruct(parts.shape, parts.dtype), in_specs=[_HBM], out_specs=_HBM,
        scratch_shapes=_EXCHANGE_SEMS)(parts)


def _all_to_all_step(g_ref, out_ref, send_sems, recv_sems, local_sem, *, start, finish):
    x, y, c = _place()
    me = 4 * x + 2 * y + c
    mine = pltpu.make_async_copy(g_ref.at[me], out_ref.at[me], local_sem)
    if start:
        mine.start()
    sends, peers = [], []
    for k in range(1, 8):
        px = 1 - x if k & 4 else x
        py = 1 - y if k & 2 else y
        pc = 1 - c if k & 1 else c
        peer = 4 * px + 2 * py + pc
        cp = pltpu.make_async_remote_copy(
            src_ref=g_ref.at[peer], dst_ref=out_ref.at[me], send_sem=send_sems.at[k - 1],
            recv_sem=recv_sems.at[k - 1], device_id=(px, py, pc), device_id_type=MESH)
        if start:
            cp.start()
        sends.append(cp)
        peers.append((peer, (px, py, pc)))
    if finish:
        for k, (peer, pid) in enumerate(peers):
            pltpu.make_async_remote_copy(
                src_ref=g_ref.at[peer], dst_ref=out_ref.at[peer], send_sem=send_sems.at[k],
                recv_sem=recv_sems.at[k], device_id=pid, device_id_type=MESH).wait_recv()
        for cp in sends:
            cp.wait_send()
        mine.wait()


def _share_halves(both, *, name):
    _, r, _ = both.shape
    n_split = 1
    while both.size // 2 * both.dtype.itemsize > n_split * MAX_COPY_BYTES and r % (2 * n_split * 16) == 0:
        n_split *= 2
    rs = r // n_split

    def body(in_ref, out_ref, send_sems, recv_sems):
        x, y, c = _place()

        def copy(k, slot):
            rows = pl.ds(k * rs, rs)
            return pltpu.make_async_remote_copy(
                src_ref=in_ref.at[slot, rows], dst_ref=out_ref.at[slot, rows], send_sem=send_sems.at[k],
                recv_sem=recv_sems.at[k], device_id=(x, y, 1 - c), device_id_type=MESH)

        sends = [copy(k, c) for k in range(n_split)]
        for cp in sends:
            cp.start()
        for k in range(n_split):
            copy(k, 1 - c).wait_recv()
        for cp in sends:
            cp.wait_send()

    return pl.pallas_call(
        body, name=name, out_shape=jax.ShapeDtypeStruct(both.shape, both.dtype), in_specs=[_HBM], out_specs=_HBM,
        input_output_aliases={0: 0},
        scratch_shapes=[pltpu.SemaphoreType.DMA((n_split,)), pltpu.SemaphoreType.DMA((n_split,))])(both)


def _piece_rows(shape):
    n = math.prod(shape)
    if n % 128 == 0:
        return n // 128, 128
    assert shape[-1] <= 128, shape
    return n // shape[-1], shape[-1]


def _pack_small(arrs, row_multiple):
    pieces = []
    for a in arrs:
        rows, lanes = _piece_rows(a.shape)
        t = a.astype(F32).reshape(rows, lanes)
        pieces.append(jnp.pad(t, ((0, -rows % 8), (0, 128 - lanes))))
    buf = jnp.concatenate(pieces, axis=0)
    return jnp.pad(buf, ((0, -buf.shape[0] % row_multiple), (0, 0)))


def _unpack_small(buf, shapes):
    out, off = [], 0
    for shp in shapes:
        rows, lanes = _piece_rows(shp)
        out.append(buf[off:off + rows, :lanes].reshape(shp))
        off += rows + (-rows % 8)
    return out


def _w_in_pieces(gathered, n_heads, *, name):
    d4, cw = gathered.shape
    d = d4 // N_CHIPS
    dl = n_heads * HEAD_DIM
    nba = 2 * n_heads
    tr = _tile(d, 128, 16)

    def body(g_ref, qkv_ref, zpg_ref, ba_ref):
        full = jnp.concatenate([g_ref[j] for j in range(N_CHIPS)], axis=1)
        qkv_ref[...] = full[:, :3 * dl]
        zpg_ref[...] = jnp.concatenate([full[:, 3 * dl:4 * dl], full[:, 4 * dl + nba:]], axis=1)
        ba_ref[...] = jnp.concatenate([full[:, 4 * dl:4 * dl + nba], jnp.zeros((tr, 128 - nba), BF16)], axis=1)

    row = lambda c: pl.BlockSpec((tr, c), lambda i: (i, 0))
    return pl.pallas_call(
        body, name=name, grid=(d // tr,),
        in_specs=[pl.BlockSpec((N_CHIPS, tr, cw), lambda i: (0, i, 0))],
        out_specs=[row(3 * dl), row(4 * d), row(128)],
        out_shape=[jax.ShapeDtypeStruct((d, 3 * dl), BF16), jax.ShapeDtypeStruct((d, 4 * d), BF16),
                   jax.ShapeDtypeStruct((d, 128), BF16)],
        compiler_params=_params("parallel"))(gathered.reshape(N_CHIPS, d, cw))


def _w_in_grad_parts(g_qkv, g_zpg, g_ba, n_heads, *, name):
    d = g_qkv.shape[0]
    cw = (g_qkv.shape[1] + 4 * d + 2 * n_heads) // N_CHIPS
    tr = _tile(d // 2, 128, 16)
    per_half = d // 2 // tr

    def body(a_ref, z_ref, p_ref, ga_ref, gb_ref, ba_ref, o_ref):
        full = jnp.concatenate([a_ref[...], z_ref[...], ba_ref[...][:, :2 * n_heads], p_ref[...], ga_ref[...],
                                gb_ref[...]], axis=1)
        for j in range(N_CHIPS):
            o_ref[j] = full[:, cw * j:cw * (j + 1)]

    row = lambda c: pl.BlockSpec((tr, c), lambda i: (i, 0))
    out = pl.pallas_call(
        body, name=name, grid=(d // tr,),
        in_specs=[row(g_qkv.shape[1]), row(d), row(d), row(d), row(d), row(128)],
        out_specs=pl.BlockSpec((N_CHIPS, None, tr, cw), lambda i: (0, i // per_half, i % per_half, 0)),
        out_shape=jax.ShapeDtypeStruct((N_CHIPS, 2, d // 2, cw), BF16),
        compiler_params=_params("parallel"))(g_qkv, *g_zpg, g_ba)
    return out.reshape(8, d // 2, cw)


def _layer_fwd(x, p, n_heads, gather=()):
    h = _rmsnorm_fwd(x, p["norm_mix_w"], name="norm_mix_fwd")
    pq = _matmul(h, p["w_qkv"], name="proj_qkv")
    zpg = _matmul(h, p["w_zpg"], name="proj_zpg")
    ba = _matmul(h, p["w_ba"], name="proj_ba")
    qkv3 = _qkv_fwd(pq, p["conv_qkv_w"], n_heads, name="qkv_fwd")
    bg = _gates_fwd(ba, p["a_log"], p["dt_bias"], n_heads, name="gates_fwd")
    o, st, ti, gathered = _gdn_fwd(qkv3, bg, gather, name="gdn_fwd")
    mixed = _mix_fwd(o, zpg, p["gdn_nw"], p["pool_w"], p["pool_scale"], name="mix_fwd")
    x1 = _matmul(mixed, p["w_out"], add=x, name="out_proj")
    h2 = _rmsnorm_fwd(x1, p["norm_ffn_w"], name="norm_ffn_fwd")
    gu = _matmul(h2, p["w_up"], name="up_proj")
    act = _ffn_mid_fwd(gu, p["conv_ffn_w"], p["conv_ffn_b"], name="ffn_mid_fwd")
    x2 = _matmul(act, p["w_down"], add=x1, name="down_proj")
    return x2, (x, h, pq, zpg, ba, o, st, ti, mixed, x1, h2, gu, act), gathered


def _layer_bwd(dx2, p, saved, n_heads, scatter=()):
    x, h, pq, zpg, ba, o, st, ti, mixed, x1, h2, gu, act = saved
    d = x.shape[1]
    f = act.shape[1]
    dact = _matmul(dx2, p["w_down"], tb=True, name="d_act")
    g_down = _matmul(act, dx2, ta=True, out_dtype=BF16, name="g_w_down")
    riders = lambda carrier: [scatter[n] for n in RIDERS[carrier] if scatter]
    scattered = {}
    dgate, dup, dcw_p, dcb_p, got = _ffn_mid_bwd(gu, p["conv_ffn_w"], p["conv_ffn_b"], dact, riders("ffn_mid_bwd"),
                                                 name="ffn_mid_bwd")
    scattered.update(zip(RIDERS["ffn_mid_bwd"], got))
    dh2 = _matmul(dgate, p["w_up"], tb=True, b_k0=0, name="d_h2_gate")
    dh2 = _matmul(dup, p["w_up"], tb=True, b_k0=f, add=dh2, name="d_h2_up")
    g_up = jnp.concatenate([_matmul(h2, dgate, ta=True, out_dtype=BF16, name="g_w_up_gate"),
                            _matmul(h2, dup, ta=True, out_dtype=BF16, name="g_w_up_up")], axis=1)
    dx1, dnf_p = _rmsnorm_bwd(x1, p["norm_ffn_w"], dh2, dx2, name="norm_ffn_bwd")
    dmix = _matmul(dx1, p["w_out"], tb=True, name="d_mixed")
    g_out = _matmul(mixed, dx1, ta=True, out_dtype=BF16, name="g_w_out")
    d_o, d4, g_pool, dnw_p, dps_p = _mix_bwd(o, zpg, p["gdn_nw"], p["pool_w"], p["pool_scale"], dmix, name="mix_bwd")
    qkv3 = _qkv_fwd(pq, p["conv_qkv_w"], n_heads, name="qkv_fwd")
    bg = _gates_fwd(ba, p["a_log"], p["dt_bias"], n_heads, name="gates_fwd")
    dqkv3, dbg, got = _gdn_bwd(qkv3, bg, st, ti, d_o, riders("gdn_bwd"), name="gdn_bwd")
    scattered.update(zip(RIDERS["gdn_bwd"], got))
    dpq, dcq_p, got = _qkv_bwd(pq, p["conv_qkv_w"], dqkv3, n_heads, riders("qkv_bwd"), name="qkv_bwd")
    scattered.update(zip(RIDERS["qkv_bwd"], got))
    dba, dgate_p = _gates_bwd(ba, p["a_log"], p["dt_bias"], dbg, n_heads, name="gates_bwd")
    dh = _matmul(dpq, p["w_qkv"], tb=True, name="d_h_qkv")
    dh = _matmul(d4, p["w_zpg"], tb=True, a_kparts=True, add=dh, name="d_h_zpg")
    dh = _matmul(dba, p["w_ba"], tb=True, add=dh, name="d_h_ba")
    g_qkv = _matmul(h, dpq, ta=True, out_dtype=BF16, name="g_w_qkv")
    g_zpg = [_matmul(h, d4, ta=True, b_part=seg, out_dtype=BF16, name="g_w_zpg") for seg in range(4)]
    g_ba = _matmul(h, dba, ta=True, out_dtype=BF16, name="g_w_ba")
    dx, dnm_p = _rmsnorm_bwd(x, p["norm_mix_w"], dh, dx1, name="norm_mix_bwd")
    g_in = _w_in_grad_parts(g_qkv, g_zpg, g_ba, n_heads, name="g_w_in_parts")
    rows = lambda t: jnp.sum(t, axis=0)
    ni8, c12 = dcq_p.shape
    nj = dcw_p.shape[1] // (3 * _tile(f, 512))
    small = {
        "norm_mix_w": rows(dnm_p),
        "conv_qkv_w": rows(dcq_p).reshape(c12 // (4 * HEAD_DIM), 4, HEAD_DIM).transpose(1, 0, 2).reshape(4, c12 // 4),
        "a_log": rows(dgate_p)[:n_heads],
        "dt_bias": rows(dgate_p)[n_heads:2 * n_heads],
        "gdn_norm_w": jnp.sum(rows(dnw_p).reshape(d // HEAD_DIM, HEAD_DIM), axis=0),
        "pool_scale": rows(dps_p),
        "norm_ffn_w": rows(dnf_p),
        "conv_ffn_w": rows(dcw_p).reshape(nj, 3, f // nj).transpose(1, 0, 2).reshape(3, f),
        "conv_ffn_b": rows(dcb_p),
    }
    big = {"w_in": g_in, "pool_w": g_pool, "w_out": g_out, "w_up": g_up, "w_down": g_down}
    return dx, big, small, scattered


BIG = ("w_in", "pool_w", "w_out", "w_up", "w_down")
SMALL = ("norm_mix_w", "conv_qkv_w", "a_log", "dt_bias", "gdn_norm_w", "pool_scale", "norm_ffn_w", "conv_ffn_w",
         "conv_ffn_b", "norm_final_w")
WEIGHTS = ("norm_mix_w", "w_in", "conv_qkv_w", "a_log", "dt_bias", "gdn_norm_w", "pool_w", "pool_scale", "w_out",
           "norm_ffn_w", "w_up", "conv_ffn_w", "conv_ffn_b", "w_down", "norm_final_w")
N_CHIPS = 4
RIDERS = {"ffn_mid_bwd": ("w_down",), "gdn_bwd": ("w_in", "pool_w", "w_out"), "qkv_bwd": ("w_up",)}


def _my_half(local, cc):
    m = local.shape[0] // 2
    return lax.dynamic_slice_in_dim(local, cc * m, m, axis=0)


def _weight_halves(w, l, cc):
    pw = w["pool_w"][l].astype(BF16)
    local = dict(w_in=w["w_in"][l].astype(BF16), pool_w=pw.reshape(-1, pw.shape[-1]),
                 w_out=w["w_out"][l].astype(BF16), w_up=w["w_up"][l].astype(BF16), w_down=w["w_down"][l].astype(BF16))
    return [_my_half(local[n], cc) for n in BIG]


def _full_weights(w, gathered):
    g_in, g_pool, g_out, g_up, g_down = gathered
    d = w["w_in"].shape[1]
    g, r, c = w["pool_w"].shape[1:]
    return dict(
        w_in=g_in,
        pool_w=g_pool.reshape(N_CHIPS, g, r, c).transpose(1, 0, 2, 3).reshape(g, N_CHIPS * r, c),
        w_out=g_out, w_up=g_up.reshape(N_CHIPS, d, -1).transpose(1, 0, 2).reshape(d, -1), w_down=g_down)


def _grad_parts(name, g):
    if name in ("w_in", "w_up"):
        r, c = g.shape
        return g.reshape(2, r // 2, N_CHIPS, c // N_CHIPS).transpose(2, 0, 1, 3).reshape(8, r // 2, c // N_CHIPS)
    if name == "pool_w":
        ng, r, c = g.shape
        t = g.reshape(2, ng // 2, N_CHIPS, r // N_CHIPS, c).transpose(2, 0, 1, 3, 4)
        return t.reshape(8, (ng // 2) * (r // N_CHIPS), c)
    r, c = g.shape
    return g.reshape(8, r // 8, c)


def _layer_parts(big):
    return {n: big[n] if n == "w_in" else _grad_parts(n, big[n].astype(BF16)) for n in BIG}


def _finish_reduce(name, got, shard_shape, slot):
    both = _share_halves(_sum_parts(got, slot=slot, name="sum_" + name), name="share_" + name)
    return both.reshape(shard_shape)


def kernel(x, norm_mix_w, w_in, conv_qkv_w, a_log, dt_bias, gdn_norm_w, pool_w, pool_scale, w_out, norm_ffn_w, w_up, conv_ffn_w, conv_ffn_b, w_down, norm_final_w, loss_target, m_norm_mix_w, m_w_in, m_conv_qkv_w, m_a_log, m_dt_bias, m_gdn_norm_w, m_pool_w, m_pool_scale, m_w_out, m_norm_ffn_w, m_w_up, m_conv_ffn_w, m_conv_ffn_b, m_w_down, m_norm_final_w, v_norm_mix_w, v_w_in, v_conv_qkv_w, v_a_log, v_dt_bias, v_gdn_norm_w, v_pool_w, v_pool_scale, v_w_out, v_norm_ffn_w, v_w_up, v_conv_ffn_w, v_conv_ffn_b, v_w_down, v_norm_final_w):
    w = dict(norm_mix_w=norm_mix_w, w_in=w_in, conv_qkv_w=conv_qkv_w, a_log=a_log, dt_bias=dt_bias,
             gdn_norm_w=gdn_norm_w, pool_w=pool_w, pool_scale=pool_scale, w_out=w_out, norm_ffn_w=norm_ffn_w,
             w_up=w_up, conv_ffn_w=conv_ffn_w, conv_ffn_b=conv_ffn_b, w_down=w_down, norm_final_w=norm_final_w)
    m = dict(norm_mix_w=m_norm_mix_w, w_in=m_w_in, conv_qkv_w=m_conv_qkv_w, a_log=m_a_log, dt_bias=m_dt_bias,
             gdn_norm_w=m_gdn_norm_w, pool_w=m_pool_w, pool_scale=m_pool_scale, w_out=m_w_out,
             norm_ffn_w=m_norm_ffn_w, w_up=m_w_up, conv_ffn_w=m_conv_ffn_w, conv_ffn_b=m_conv_ffn_b,
             w_down=m_w_down, norm_final_w=m_norm_final_w)
    v = dict(norm_mix_w=v_norm_mix_w, w_in=v_w_in, conv_qkv_w=v_conv_qkv_w, a_log=v_a_log, dt_bias=v_dt_bias,
             gdn_norm_w=v_gdn_norm_w, pool_w=v_pool_w, pool_scale=v_pool_scale, w_out=v_w_out,
             norm_ffn_w=v_norm_ffn_w, w_up=v_w_up, conv_ffn_w=v_conv_ffn_w, conv_ffn_b=v_conv_ffn_b,
             w_down=v_w_down, norm_final_w=v_norm_final_w)
    depth, n_heads = a_log.shape
    d = x.shape[-1]
    dl = n_heads * HEAD_DIM
    assert dl == d and gdn_norm_w.shape[1] == HEAD_DIM
    cx, cy, cc = _place()
    chip = 2 * cx + cy

    conv_packed = _pack_small([conv_qkv_w, conv_ffn_w], 16)
    conv_all = _allgather(conv_packed, name="gather_conv").reshape(N_CHIPS, 2, -1, 128)[:, 0]
    conv_j = [_unpack_small(conv_all[j], [conv_qkv_w.shape, conv_ffn_w.shape]) for j in range(N_CHIPS)]
    conv_q = jnp.concatenate([t[0] for t in conv_j], axis=-1)
    conv_f = jnp.concatenate([t[1] for t in conv_j], axis=-1)

    def layer_params(l, full):
        w_qkv, w_zpg, w_ba = _w_in_pieces(full["w_in"], n_heads, name="w_in_pieces")
        return dict(
            norm_mix_w=norm_mix_w[l][None], norm_ffn_w=norm_ffn_w[l][None],
            w_qkv=w_qkv, w_zpg=w_zpg, w_ba=w_ba,
            conv_qkv_w=conv_q[l], a_log=a_log[l], dt_bias=dt_bias[l],
            gdn_nw=jnp.tile(gdn_norm_w[l], d // HEAD_DIM)[None], pool_w=full["pool_w"], pool_scale=pool_scale[l][None],
            w_out=full["w_out"], w_up=full["w_up"], conv_ffn_w=conv_f[l], conv_ffn_b=conv_ffn_b[l][None],
            w_down=full["w_down"])

    xs = x[0]
    saved, params = [], []
    gathered = [_allgather(t, name="gather_" + n) for n, t in zip(BIG, _weight_halves(w, 0, cc))]
    for l in range(depth):
        params.append(layer_params(l, _full_weights(w, gathered)))
        nxt = _weight_halves(w, l + 1, cc) if l + 1 < depth else ()
        xs, sv, gathered = _layer_fwd(xs, params[l], n_heads, nxt)
        saved.append(sv)
    dx, dnf_p, loss_p = _final_loss(xs, norm_final_w[None], loss_target[0], name="final_loss")
    loss = lax.psum(jnp.sum(loss_p) * (0.5 / d), ("x", "y", "c"))

    slot = jnp.reshape(cc, (1,)).astype(jnp.int32)
    small_g = [None] * depth
    reduced = [dict() for _ in range(depth)]
    riding = {}
    for l in reversed(range(depth)):
        dx, big, small_g[l], got = _layer_bwd(dx, params[l], saved[l], n_heads, riding)
        for n, t in got.items():
            reduced[l + 1][n] = _finish_reduce(n, t, w[n].shape[1:], slot)
        riding = _layer_parts(big) if l > 0 else {}
        if l == 0:
            for n, part in _layer_parts(big).items():
                reduced[0][n] = _finish_reduce(n, _all_to_all(part, name="scatter_" + n), w[n].shape[1:], slot)
    grads = {n: jnp.stack([reduced[l][n] for l in range(depth)]) for n in BIG}

    small_shapes = {n: ((depth,) + small_g[0][n].shape if n != "norm_final_w" else (d,)) for n in SMALL}
    small_local = [jnp.stack([small_g[l][n] for l in range(depth)]) for n in SMALL[:-1]] + [jnp.sum(dnf_p, axis=0)]
    sp = _pack_small(small_local, 512)
    sg = _allgather(sp, name="gather_small").reshape(8, sp.shape[0], 128)
    small_sum = _unpack_small(_sum_parts(sg, name="sum_small"), [small_shapes[n] for n in SMALL])
    for n, g in zip(SMALL, small_sum):
        if n in ("conv_qkv_w", "conv_ffn_w"):
            cols = w[n].shape[-1]
            g = lax.dynamic_slice_in_dim(g, chip * cols, cols, axis=2)
        grads[n] = g

    delta, new_m, new_v = {}, {}, {}
    for n in BIG:
        shp = w[n].shape
        r2 = lambda t: t.reshape(-1, shp[-1])
        dd, mm, vv = _adamw(r2(w[n]), r2(grads[n]), r2(m[n]), r2(v[n]), name="adamw_" + n)
        delta[n], new_m[n], new_v[n] = dd.reshape(shp), mm.reshape(shp), vv.reshape(shp)
    pk = lambda src: _pack_small([src[n] for n in SMALL], 8)
    outs = _adamw(pk(w), pk(grads), pk(m), pk(v), name="adamw_small")
    for dst, buf in zip((delta, new_m, new_v), outs):
        for n, t in zip(SMALL, _unpack_small(buf, [w[n].shape for n in SMALL])):
            dst[n] = t

    return (loss, dx[None], *[grads[n] for n in WEIGHTS], *[delta[n] for n in WEIGHTS],
            *[new_m[n] for n in WEIGHTS], *[new_v[n] for n in WEIGHTS])
```
